```python
import math, functools
import jax, jax.numpy as jnp
from jax import lax
import numpy as np

D_MODEL = 1024
BATCH = 4
SEQ = 4096
DEPTH = 4
DEC_BATCH = 32
DEC_SEQ = 4
PAST_LEN = 8192
PAGE_SIZE = 128

N_MIXERS = 3
LAYER_KINDS = tuple(i % N_MIXERS for i in range(DEPTH))
N_RET = LAYER_KINDS.count(0)
N_SSD = LAYER_KINDS.count(1)
N_DIFF = LAYER_KINDS.count(2)
EPS = 1e-6
BLOCK = 128

RET_HEADS = 4
RET_DK = D_MODEL // RET_HEADS
RET_DV = 2 * RET_DK
RET_VDIM = RET_HEADS * RET_DV
ROPE_BASE = 10000.0

SSD_INNER = 2 * D_MODEL
SSD_HEADDIM = 64
SSD_HEADS = SSD_INNER // SSD_HEADDIM
SSD_GROUPS = 8
SSD_STATE = 128
SSD_CONV = 4
SSD_CONV_CH = SSD_INNER + 2 * SSD_GROUPS * SSD_STATE
SSD_IN = SSD_INNER + SSD_CONV_CH + SSD_HEADS

DIFF_HEADS = 8
DIFF_DH = D_MODEL // DIFF_HEADS // 2
DIFF_DV = 2 * DIFF_DH

REL_BUCKETS = 32
REL_MAX_DIST = 128

FFN_HIDDEN = -(-8 * D_MODEL // (3 * 256)) * 256

kernel_name = 'hybrid_retention_ssd_diffattn_decoder_step'


def rmsnorm(x, g):
    xf = x.astype(jnp.float32)
    y = xf * lax.rsqrt(jnp.mean(xf * xf, axis=-1, keepdims=True) + EPS)
    return (y * g.astype(jnp.float32)).astype(x.dtype)


def rotary(x, pos):
    half = x.shape[-1] // 2
    inv = 1.0 / (ROPE_BASE ** jnp.linspace(0.0, 1.0, half, dtype=jnp.float32))
    ang = pos.astype(jnp.float32)[:, None] * inv[None, :]
    cos = jnp.cos(ang)[None, :, None, :]
    sin = jnp.sin(ang)[None, :, None, :]
    xf = x.astype(jnp.float32)
    x1, x2 = xf[..., :half], xf[..., half:]
    return jnp.concatenate([x1 * cos - x2 * sin, x2 * cos + x1 * sin], axis=-1).astype(x.dtype)


def scan_chunks(chunk_fn, carry, seqs):
    L = seqs[0].shape[1]
    c = BLOCK if L % BLOCK == 0 else L
    n = L // c
    xs = tuple(jnp.moveaxis(s.reshape(s.shape[0], n, c, *s.shape[2:]), 1, 0) for s in seqs)
    carry, ys = lax.scan(lambda cr, xc: chunk_fn(cr, *xc), carry, xs)
    y = jnp.moveaxis(ys, 0, 1)
    return y.reshape(y.shape[0], L, *y.shape[3:]), carry


def retention_log_decay():
    return jnp.log(1.0 - 2.0 ** (-5.0 - jnp.arange(RET_HEADS, dtype=jnp.float32)))


def retention_chunk(S0, q, k, v, log_gamma):
    q, k, v = (t.astype(jnp.float32) for t in (q, k, v))
    C = q.shape[1]
    i = jnp.arange(C, dtype=jnp.float32)
    rel = i[:, None] - i[None, :]
    decay = jnp.where(rel[..., None] >= 0, jnp.exp(jnp.maximum(rel, 0.0)[..., None] * log_gamma), 0.0)
    scores = jnp.einsum('bihd,bjhd->bhij', q, k) * jnp.transpose(decay, (2, 0, 1))[None]
    o = jnp.einsum('bhij,bjhe->bihe', scores, v)
    o = o + jnp.einsum('bihd,bhde->bihe', q, S0) * jnp.exp((i + 1.0)[:, None] * log_gamma)[None, :, :, None]
    w_end = jnp.exp((C - 1.0 - i)[:, None] * log_gamma)
    S1 = jnp.exp(C * log_gamma)[None, :, None, None] * S0 + jnp.einsum('bjhd,bjhe->bhde', k * w_end[None, :, :, None], v)
    return S1, o


def retention_mixer(h, pos, S0, w_in, w_out):
    b, L, _ = h.shape
    qk = RET_HEADS * RET_DK
    q, k, v, g = jnp.split(h @ w_in, [qk, 2 * qk, 2 * qk + RET_VDIM], axis=-1)
    q = rotary(q.reshape(b, L, RET_HEADS, RET_DK), pos)
    k = rotary(k.reshape(b, L, RET_HEADS, RET_DK), pos) * (RET_DK ** -0.5)
    v = v.reshape(b, L, RET_HEADS, RET_DV)
    chunk = functools.partial(retention_chunk, log_gamma=retention_log_decay())
    o, S1 = scan_chunks(chunk, S0.astype(jnp.float32), (q, k, v))
    o = o * lax.rsqrt(jnp.mean(o * o, axis=-1, keepdims=True) + EPS)
    y = jax.nn.silu(g.astype(jnp.float32)) * o.reshape(b, L, RET_VDIM)
    return y.astype(h.dtype) @ w_out, S1


def ssd_chunk(h0, x, dt, Bm, Cm, A):
    x, Bm, Cm = (t.astype(jnp.float32) for t in (x, Bm, Cm))
    b, C = x.shape[:2]
    G, R = SSD_GROUPS, SSD_HEADS // SSD_GROUPS
    cs = jnp.cumsum(dt * A, axis=1)
    seg = cs[:, :, None, :] - cs[:, None, :, :]
    causal = jnp.tril(jnp.ones((C, C), dtype=bool))
    Lm = jnp.exp(jnp.where(causal[None, :, :, None], seg, -jnp.inf))
    CB = jnp.einsum('bign,bjgn->bgij', Cm, Bm)
    W = (CB[:, :, None] * jnp.transpose(Lm.reshape(b, C, C, G, R), (0, 3, 4, 1, 2))
         * jnp.transpose(dt.reshape(b, C, G, R), (0, 2, 3, 1))[:, :, :, None, :])
    xr = x.reshape(b, C, G, R, SSD_HEADDIM)
    y = jnp.einsum('bgrij,bjgrp->bigrp', W, xr)
    h0r = h0.reshape(b, G, R, SSD_HEADDIM, SSD_STATE)
    y = y + jnp.einsum('bign,bgrpn->bigrp', Cm, h0r) * jnp.exp(cs).reshape(b, C, G, R)[..., None]
    w_end = (jnp.exp(cs[:, -1:] - cs) * dt).reshape(b, C, G, R)
    h1 = (jnp.exp(cs[:, -1]).reshape(b, G, R)[..., None, None] * h0r
          + jnp.einsum('bjgn,bjgr,bjgrp->bgrpn', Bm, w_end, xr))
    return h1.reshape(b, SSD_HEADS, SSD_HEADDIM, SSD_STATE), y.reshape(b, C, SSD_HEADS, SSD_HEADDIM)


def ssd_mixer(h, conv_buf, s0, w_in, conv_w, conv_b, dt_bias, A_log, D_skip, norm_g, w_out):
    b, L, _ = h.shape
    z, xbc, dt = jnp.split(h @ w_in, [SSD_INNER, SSD_INNER + SSD_CONV_CH], axis=-1)
    xpad = jnp.concatenate([conv_buf.astype(xbc.dtype), xbc], axis=1)
    conv = conv_b + sum(xpad[:, j:j + L] * conv_w[j] for j in range(SSD_CONV))
    new_buf = xpad[:, L:]
    xbc = jax.nn.silu(conv)
    x, Bm, Cm = jnp.split(xbc, [SSD_INNER, SSD_INNER + SSD_GROUPS * SSD_STATE], axis=-1)
    x = x.reshape(b, L, SSD_HEADS, SSD_HEADDIM)
    Bm = Bm.reshape(b, L, SSD_GROUPS, SSD_STATE)
    Cm = Cm.reshape(b, L, SSD_GROUPS, SSD_STATE)
    dt = jax.nn.softplus(dt.astype(jnp.float32) + dt_bias.astype(jnp.float32))
    A = -jnp.exp(A_log.astype(jnp.float32))
    chunk = functools.partial(ssd_chunk, A=A)
    y, s1 = scan_chunks(chunk, s0.astype(jnp.float32), (x, dt, Bm, Cm))
    y = y + D_skip.astype(jnp.float32)[None, None, :, None] * x.astype(jnp.float32)
    y = y.reshape(b, L, SSD_INNER) * jax.nn.silu(z.astype(jnp.float32))
    yg = y.reshape(b, L, SSD_GROUPS, SSD_INNER // SSD_GROUPS)
    yg = yg * lax.rsqrt(jnp.mean(yg * yg, axis=-1, keepdims=True) + EPS)
    y = yg.reshape(b, L, SSD_INNER) * norm_g.astype(jnp.float32)
    return y.astype(h.dtype) @ w_out, new_buf, s1


def t5_bucket(dist):
    n = jnp.maximum(dist, 0)
    exact = REL_BUCKETS // 2
    nf = jnp.maximum(n, 1).astype(jnp.float32)
    large = exact + (jnp.log(nf / exact) / math.log(REL_MAX_DIST / exact) * (REL_BUCKETS - exact)).astype(jnp.int32)
    large = jnp.minimum(large, REL_BUCKETS - 1)
    return jnp.where(n < exact, n, large)


def diff_attn_core(q, k, v, q_pos, k_pos, lam, rel_table):
    dist = q_pos[:, None] - k_pos[None, :]
    bias = jnp.transpose(rel_table[t5_bucket(dist)].astype(jnp.float32), (2, 0, 1))
    s = jnp.einsum('bqhcd,bkhcd->bhcqk', q, k).astype(jnp.float32) * (DIFF_DH ** -0.5) + bias[None, :, None]
    s = jnp.where((dist >= 0)[None, None, None], s, -jnp.inf)
    p = jax.nn.softmax(s, axis=-1)
    a = p[:, :, 0] - lam * p[:, :, 1]
    return jnp.einsum('bhqk,bkhe->bqhe', a.astype(v.dtype), v)


def diff_mixer(h, k_past, v_past, w_in, lq1, lk1, lq2, lk2, subln_g, w_out, rel_table, lam_init):
    b, L, _ = h.shape
    q, k, v = jnp.split(h @ w_in, [D_MODEL, 2 * D_MODEL], axis=-1)
    q = q.reshape(b, L, DIFF_HEADS, 2, DIFF_DH)
    k = k.reshape(b, L, DIFF_HEADS, 2, DIFF_DH)
    v = v.reshape(b, L, DIFF_HEADS, DIFF_DV)
    f32 = jnp.float32
    lam = (jnp.exp(jnp.sum(lq1.astype(f32) * lk1.astype(f32)))
           - jnp.exp(jnp.sum(lq2.astype(f32) * lk2.astype(f32))) + lam_init)
    if k_past is None:
        k_all, v_all = k, v
    else:
        k_all = jnp.concatenate([k_past.astype(k.dtype), k], axis=1)
        v_all = jnp.concatenate([v_past.astype(v.dtype), v], axis=1)
    Lk = k_all.shape[1]
    k_pos = jnp.arange(Lk, dtype=jnp.int32)
    q_pos = (Lk - L) + jnp.arange(L, dtype=jnp.int32)
    qb = BLOCK if L % BLOCK == 0 else L
    nb = L // qb
    q_blocks = jnp.moveaxis(q.reshape(b, nb, qb, DIFF_HEADS, 2, DIFF_DH), 1, 0)
    o = lax.map(lambda blk: diff_attn_core(blk[0], k_all, v_all, blk[1], k_pos, lam, rel_table),
                (q_blocks, q_pos.reshape(nb, qb)))
    o = jnp.moveaxis(o, 0, 1).reshape(b, L, DIFF_HEADS, DIFF_DV)
    o = rmsnorm(o, subln_g) * (1.0 - lam_init)
    return o.reshape(b, L, D_MODEL) @ w_out, k, v


def swiglu(h, w_in, w_out):
    gate, up = jnp.split(h @ w_in, 2, axis=-1)
    return (jax.nn.silu(gate) * up) @ w_out


def setup_inputs(seed: int = 0) -> dict:
    key = jax.random.key(seed)
    keys = iter(jax.random.split(key, 48))

    def nrm(shape, scale):
        return jax.random.normal(next(keys), shape, jnp.float32) * scale

    def gain(shape):
        return 1.0 + nrm(shape, 0.01)

    D = D_MODEL
    n_pages = PAST_LEN // PAGE_SIZE
    n_used = DEC_BATCH * n_pages
    n_pool = n_used + max(1, n_used // 4)
    x_prompt = nrm((BATCH, SEQ, D), 1.0)
    x_sample = nrm((DEC_BATCH, DEC_SEQ, D), 1.0)
    state_ret = nrm((N_RET, DEC_BATCH, RET_HEADS, RET_DK, RET_DV), 0.1)
    state_ssm = nrm((N_SSD, DEC_BATCH, SSD_HEADS, SSD_HEADDIM, SSD_STATE), 0.1)
    state_conv = nrm((N_SSD, DEC_BATCH, SSD_CONV - 1, SSD_CONV_CH), 1.0)
    cache_k_diff = nrm((N_DIFF, n_pool, PAGE_SIZE, DIFF_HEADS, 2, DIFF_DH), 1.0)
    cache_v_diff = nrm((N_DIFF, n_pool, PAGE_SIZE, DIFF_HEADS, DIFF_DV), 1.0)
    page_table = jax.random.permutation(next(keys), n_pool)[:n_used].reshape(DEC_BATCH, n_pages).astype(jnp.int32)
    norm_mix_g = gain((DEPTH, D))
    norm_ffn_g = gain((DEPTH, D))
    norm_final_g = gain((D,))
    ret_w_in = nrm((N_RET, D, 2 * RET_HEADS * RET_DK + 2 * RET_VDIM), D ** -0.5)
    ret_w_out = nrm((N_RET, RET_VDIM, D), RET_VDIM ** -0.5)
    ssd_w_in = nrm((N_SSD, D, SSD_IN), D ** -0.5)
    ssd_conv_w = nrm((N_SSD, SSD_CONV, SSD_CONV_CH), SSD_CONV ** -0.5)
    ssd_conv_b = nrm((N_SSD, SSD_CONV_CH), 0.01)
    dt0 = jnp.exp(jax.random.uniform(next(keys), (N_SSD, SSD_HEADS), jnp.float32, math.log(1e-3), math.log(1e-1)))
    ssd_dt_bias = dt0 + jnp.log(-jnp.expm1(-dt0))
    ssd_A_log = jnp.log(jax.random.uniform(next(keys), (N_SSD, SSD_HEADS), jnp.float32, 1.0, 16.0))
    ssd_D = 1.0 + nrm((N_SSD, SSD_HEADS), 0.01)
    ssd_norm_g = gain((N_SSD, SSD_INNER))
    ssd_w_out = nrm((N_SSD, SSD_INNER, D), SSD_INNER ** -0.5)
    diff_w_in = nrm((N_DIFF, D, 3 * D), D ** -0.5)
    diff_lam_q1 = nrm((N_DIFF, DIFF_DH), 0.1)
    diff_lam_k1 = nrm((N_DIFF, DIFF_DH), 0.1)
    diff_lam_q2 = nrm((N_DIFF, DIFF_DH), 0.1)
    diff_lam_k2 = nrm((N_DIFF, DIFF_DH), 0.1)
    diff_subln_g = gain((N_DIFF, DIFF_DV))
    diff_w_out = nrm((N_DIFF, D, D), D ** -0.5)
    rel_bias_table = nrm((REL_BUCKETS, DIFF_HEADS), 0.5)
    ffn_w_in = nrm((DEPTH, D, 2 * FFN_HIDDEN), D ** -0.5)
    ffn_w_out = nrm((DEPTH, FFN_HIDDEN, D), FFN_HIDDEN ** -0.5)
    return {'x_prompt': x_prompt, 'x_sample': x_sample,
            'state_ret': state_ret, 'state_ssm': state_ssm, 'state_conv': state_conv,
            'cache_k_diff': cache_k_diff, 'cache_v_diff': cache_v_diff, 'page_table': page_table,
            'norm_mix_g': norm_mix_g, 'norm_ffn_g': norm_ffn_g, 'norm_final_g': norm_final_g,
            'ret_w_in': ret_w_in, 'ret_w_out': ret_w_out,
            'ssd_w_in': ssd_w_in, 'ssd_conv_w': ssd_conv_w, 'ssd_conv_b': ssd_conv_b,
            'ssd_dt_bias': ssd_dt_bias, 'ssd_A_log': ssd_A_log, 'ssd_D': ssd_D,
            'ssd_norm_g': ssd_norm_g, 'ssd_w_out': ssd_w_out,
            'diff_w_in': diff_w_in, 'diff_lam_q1': diff_lam_q1, 'diff_lam_k1': diff_lam_k1,
            'diff_lam_q2': diff_lam_q2, 'diff_lam_k2': diff_lam_k2,
            'diff_subln_g': diff_subln_g, 'diff_w_out': diff_w_out,
            'rel_bias_table': rel_bias_table,
            'ffn_w_in': ffn_w_in, 'ffn_w_out': ffn_w_out}


def reference(x_prompt, x_sample, state_ret, state_ssm, state_conv, cache_k_diff, cache_v_diff, page_table,
              norm_mix_g, norm_ffn_g, norm_final_g,
              ret_w_in, ret_w_out,
              ssd_w_in, ssd_conv_w, ssd_conv_b, ssd_dt_bias, ssd_A_log, ssd_D, ssd_norm_g, ssd_w_out,
              diff_w_in, diff_lam_q1, diff_lam_k1, diff_lam_q2, diff_lam_k2, diff_subln_g, diff_w_out,
              rel_bias_table,
              ffn_w_in, ffn_w_out):

    def run_group(x, sample):
        b, L, _ = x.shape
        offset = page_table.shape[1] * PAGE_SIZE if sample else 0
        pos = offset + jnp.arange(L, dtype=jnp.int32)
        ret_new, ssm_new, conv_new, k_new, v_new = [], [], [], [], []
        h = x
        for i in range(DEPTH):
            kind = LAYER_KINDS[i]
            j = LAYER_KINDS[:i].count(kind)
            hn = rmsnorm(h, norm_mix_g[i])
            if kind == 0:
                S0 = state_ret[j] if sample else jnp.zeros((b, RET_HEADS, RET_DK, RET_DV), jnp.float32)
                y, S1 = retention_mixer(hn, pos, S0, ret_w_in[j], ret_w_out[j])
                ret_new.append(S1)
            elif kind == 1:
                buf = state_conv[j] if sample else jnp.zeros((b, SSD_CONV - 1, SSD_CONV_CH), x.dtype)
                s0 = state_ssm[j] if sample else jnp.zeros((b, SSD_HEADS, SSD_HEADDIM, SSD_STATE), jnp.float32)
                y, buf1, s1 = ssd_mixer(hn, buf, s0, ssd_w_in[j], ssd_conv_w[j], ssd_conv_b[j],
                                        ssd_dt_bias[j], ssd_A_log[j], ssd_D[j], ssd_norm_g[j], ssd_w_out[j])
                conv_new.append(buf1)
                ssm_new.append(s1)
            else:
                if sample:
                    kp = cache_k_diff[j][page_table]
                    vp = cache_v_diff[j][page_table]
                    k_past = kp.reshape(b, -1, DIFF_HEADS, 2, DIFF_DH)
                    v_past = vp.reshape(b, -1, DIFF_HEADS, DIFF_DV)
                else:
                    k_past, v_past = None, None
                lam_init = 0.8 - 0.6 * math.exp(-0.3 * i)
                y, kn, vn = diff_mixer(hn, k_past, v_past, diff_w_in[j], diff_lam_q1[j], diff_lam_k1[j],
                                       diff_lam_q2[j], diff_lam_k2[j], diff_subln_g[j], diff_w_out[j],
                                       rel_bias_table, lam_init)
                k_new.append(kn)
                v_new.append(vn)
            h = h + y
            h = h + swiglu(rmsnorm(h, norm_ffn_g[i]), ffn_w_in[i], ffn_w_out[i])
        return (rmsnorm(h, norm_final_g), jnp.stack(ret_new), jnp.stack(ssm_new), jnp.stack(conv_new),
                jnp.stack(k_new), jnp.stack(v_new))

    y_prompt, ret_p, ssm_p, conv_p, k_p, v_p = run_group(x_prompt, False)
    y_sample, ret_s, ssm_s, conv_s, k_s, v_s = run_group(x_sample, True)
    return (y_prompt, y_sample, ret_p, ssm_p, conv_p, k_p, v_p, ret_s, ssm_s, conv_s, k_s, v_s)
```

```python
import functools
import math

import jax
import jax.numpy as jnp
import numpy as np
from jax import lax
from jax.experimental import pallas as pl
from jax.experimental.pallas import tpu as pltpu

F32 = jnp.float32
BF16 = jnp.bfloat16

D_MODEL = 1024
DEPTH = 4
PAGE_SIZE = 128
EPS = 1e-6
CHUNK = 128

RET_HEADS = 4
RET_DK = D_MODEL // RET_HEADS
RET_DV = 2 * RET_DK
RET_VDIM = RET_HEADS * RET_DV
ROPE_BASE = 10000.0
ROPE_HALF = RET_DK // 2

SSD_INNER = 2 * D_MODEL
SSD_HEADDIM = 64
SSD_HEADS = SSD_INNER // SSD_HEADDIM
SSD_GROUPS = 8
SSD_HPG = SSD_HEADS // SSD_GROUPS
SSD_STATE = 128
SSD_CONV = 4
SSD_BC = SSD_GROUPS * SSD_STATE
SSD_CONV_CH = SSD_INNER + 2 * SSD_BC
SSD_GW = SSD_HPG * SSD_HEADDIM

DIFF_HEADS = 8
DIFF_DH = D_MODEL // DIFF_HEADS // 2
DIFF_DV = 2 * DIFF_DH
REL_BUCKETS = 32
REL_MAX_DIST = 128

FFN_HIDDEN = -(-8 * D_MODEL // (3 * 256)) * 256

V7X_VMEM_BYTES = 64 * 1024 * 1024
LANES = 128
SUBLANES = 8
VMEM_LIMIT = 56 * 1024 * 1024


def _cparams(sem):
    return pltpu.CompilerParams(dimension_semantics=sem, vmem_limit_bytes=VMEM_LIMIT)


def _dot(a, b):
    return jnp.dot(a, b, preferred_element_type=F32)


def _dot_nt(a, b):
    return lax.dot_general(a, b, (((1,), (1,)), ((), ())), preferred_element_type=F32)


def _dot_tn(a, b):
    return lax.dot_general(a, b, (((0,), (0,)), ((), ())), preferred_element_type=F32)


def _rms(x, g):
    return x * lax.rsqrt(jnp.mean(x * x, axis=-1, keepdims=True) + EPS) * g


def _silu(x):
    return x * jax.nn.sigmoid(x)


def _resident(shape):
    return pl.BlockSpec(shape, lambda *_: (0,) * len(shape), pipeline_mode=pl.Buffered(1))


def _norm_linear_kernel(x_ref, g_ref, w_ref, *o_refs, outs, tn):
    xn = _rms(x_ref[...], g_ref[...]).astype(BF16)
    n = w_ref.shape[1]
    for c0 in range(0, n, tn):
        acc = _dot(xn, w_ref[:, c0:c0 + tn])
        for o_ref, (lo, hi, _) in zip(o_refs, outs):
            if lo <= c0 and c0 + tn <= hi:
                o_ref[:, c0 - lo:c0 - lo + tn] = acc.astype(o_ref.dtype)


def norm_linear(x, g, w, outs, tm, tn=512):
    t, d = x.shape
    n = w.shape[1]
    tn = min(tn, n)
    assert t % tm == 0 and n % tn == 0 and all(lo % tn == 0 and hi % tn == 0 for lo, hi, _ in outs)
    return pl.pallas_call(
        functools.partial(_norm_linear_kernel, outs=tuple(outs), tn=tn),
        grid=(t // tm,),
        in_specs=[pl.BlockSpec((tm, d), lambda i: (i, 0)), _resident((1, d)), _resident((d, n))],
        out_specs=[pl.BlockSpec((tm, hi - lo), lambda i: (i, 0)) for lo, hi, _ in outs],
        out_shape=[jax.ShapeDtypeStruct((t, hi - lo), dt) for lo, hi, dt in outs],
        compiler_params=_cparams(("parallel",)),
        name="norm_linear",
    )(x, g.reshape(1, d), w)


def _linear_residual_kernel(y_ref, w_ref, h_ref, o_ref):
    o_ref[...] = h_ref[...] + _dot(y_ref[...].astype(BF16), w_ref[...])


def linear_residual(y, w, h, tm):
    t, k = y.shape
    d = w.shape[1]
    return pl.pallas_call(
        _linear_residual_kernel,
        grid=(t // tm,),
        in_specs=[pl.BlockSpec((tm, k), lambda i: (i, 0)), _resident((k, d)), pl.BlockSpec((tm, d), lambda i: (i, 0))],
        out_specs=pl.BlockSpec((tm, d), lambda i: (i, 0)),
        out_shape=jax.ShapeDtypeStruct((t, d), F32),
        compiler_params=_cparams(("parallel",)),
        name="linear_residual",
    )(y, w, h)


def _ffn_kernel(h_ref, g_ref, wgu_ref, wo_ref, o_ref, *, th):
    h = h_ref[...]
    xn = _rms(h, g_ref[...]).astype(BF16)
    hidden = wo_ref.shape[0]
    o_ref[...] = h
    for c0 in range(0, hidden, th):
        gate = _dot(xn, wgu_ref[:, c0:c0 + th])
        up = _dot(xn, wgu_ref[:, hidden + c0:hidden + c0 + th])
        act = (_silu(gate) * up).astype(BF16)
        o_ref[...] += _dot(act, wo_ref[c0:c0 + th, :])


def ffn(h, g, wgu, wo, tm, th=256):
    t, d = h.shape
    hidden = wo.shape[0]
    assert hidden % th == 0 and t % tm == 0
    return pl.pallas_call(
        functools.partial(_ffn_kernel, th=th),
        grid=(t // tm,),
        in_specs=[pl.BlockSpec((tm, d), lambda i: (i, 0)), _resident((1, d)), _resident((d, 2 * hidden)),
                  _resident((hidden, d))],
        out_specs=pl.BlockSpec((tm, d), lambda i: (i, 0)),
        out_shape=jax.ShapeDtypeStruct((t, d), F32),
        compiler_params=_cparams(("parallel",)),
        name="ffn",
    )(h, g.reshape(1, d), wgu, wo)


def _final_norm_kernel(h_ref, g_ref, o_ref):
    o_ref[...] = _rms(h_ref[...], g_ref[...])


def final_norm(h, g, tm):
    t, d = h.shape
    return pl.pallas_call(
        _final_norm_kernel,
        grid=(t // tm,),
        in_specs=[pl.BlockSpec((tm, d), lambda i: (i, 0)), _resident((1, d))],
        out_specs=pl.BlockSpec((tm, d), lambda i: (i, 0)),
        out_shape=jax.ShapeDtypeStruct((t, d), F32),
        compiler_params=_cparams(("parallel",)),
        name="final_norm",
    )(h, g.reshape(1, d))


def _rope_kernel(inv_ref, cos_ref, sin_ref, *, offset):
    rows = cos_ref.shape[0]
    pos = (offset + pl.program_id(0) * rows + lax.broadcasted_iota(jnp.int32, (rows, 1), 0)).astype(F32)
    ang = pos * inv_ref[...]
    cos_ref[...] = jnp.cos(ang)
    sin_ref[...] = jnp.sin(ang)


def rope_tables(offset, rows):
    inv = (1.0 / (ROPE_BASE ** jnp.linspace(0.0, 1.0, ROPE_HALF, dtype=F32))).reshape(1, ROPE_HALF)
    tr = min(rows, 512)
    return pl.pallas_call(
        functools.partial(_rope_kernel, offset=offset),
        grid=(rows // tr,),
        in_specs=[_resident((1, ROPE_HALF))],
        out_specs=[pl.BlockSpec((tr, ROPE_HALF), lambda i: (i, 0))] * 2,
        out_shape=[jax.ShapeDtypeStruct((rows, ROPE_HALF), F32)] * 2,
        compiler_params=_cparams(("parallel",)),
        name="rope_tables",
    )(inv)


def _ret_kernel(lg_ref, q_ref, k_ref, v_ref, g_ref, cos_ref, sin_ref, s0_ref, y_ref, s1_ref, *stage, C, c_real):
    hd = pl.program_id(1)
    c = pl.program_id(2)
    lg = lg_ref[hd]

    @pl.when(c == 0)
    def _():
        s1_ref[...] = s0_ref[...]

    def load(ref, st):
        if c_real == C:
            return ref[0].astype(F32)
        st[...] = jnp.zeros(st.shape, F32)
        st[0:c_real, :] = ref[0].astype(F32)
        return st[...]

    st = stage if stage else (None,) * 3
    q = load(q_ref, st[0])
    k = load(k_ref, st[1])
    v = load(v_ref, st[2]).astype(BF16)
    cos = cos_ref[...]
    sin = sin_ref[...]

    def rot(x):
        x1, x2 = x[:, :ROPE_HALF], x[:, ROPE_HALF:]
        return jnp.concatenate([x1 * cos - x2 * sin, x2 * cos + x1 * sin], axis=-1)

    qr = rot(q).astype(BF16)
    kr = rot(k) * (RET_DK ** -0.5)
    i = lax.broadcasted_iota(jnp.int32, (C, 1), 0).astype(F32)
    j = lax.broadcasted_iota(jnp.int32, (1, C), 1).astype(F32)
    rel = i - j
    decay = jnp.where(rel >= 0, jnp.exp(jnp.maximum(rel, 0.0) * lg), 0.0)
    scores = _dot_nt(qr, kr.astype(BF16)) * decay
    s_prev = s1_ref[0, 0]
    o = _dot(scores.astype(BF16), v) + _dot(qr, s_prev.astype(BF16)) * jnp.exp((i + 1.0) * lg)
    kw = (kr * jnp.exp((c_real - 1.0 - i) * lg)).astype(BF16)
    chunk_decay = jnp.exp(jnp.zeros((1, 1), F32) + c_real * lg)
    s1_ref[0, 0] = chunk_decay * s_prev + _dot_tn(kw, v)
    o = o * lax.rsqrt(jnp.mean(o * o, axis=-1, keepdims=True) + EPS)
    o = o[0:c_real]
    y_ref[0] = (_silu(g_ref[0].astype(F32)) * o).astype(y_ref.dtype)


def retention_core(qkvg, cos, sin, s0, c_real, out_dtype):
    b, l, _ = qkvg.shape
    C = CHUNK
    nc = l // c_real
    kb, vb = RET_DK, RET_DV
    k_off = RET_HEADS
    v_off = 2 * RET_HEADS * RET_DK // RET_DV
    g_off = v_off + RET_HEADS
    lg = jnp.asarray(np.log(1.0 - 2.0 ** (-5.0 - np.arange(RET_HEADS))), F32)
    stage = [] if c_real == C else [pltpu.VMEM((C, kb), F32), pltpu.VMEM((C, kb), F32), pltpu.VMEM((C, vb), F32)]
    return pl.pallas_call(
        functools.partial(_ret_kernel, C=C, c_real=c_real),
        grid=(b, RET_HEADS, nc),
        in_specs=[
            pl.BlockSpec(memory_space=pltpu.SMEM),
            pl.BlockSpec((1, c_real, kb), lambda bi, h, c: (bi, c, h)),
            pl.BlockSpec((1, c_real, kb), lambda bi, h, c: (bi, c, k_off + h)),
            pl.BlockSpec((1, c_real, vb), lambda bi, h, c: (bi, c, v_off + h)),
            pl.BlockSpec((1, c_real, vb), lambda bi, h, c: (bi, c, g_off + h)),
            pl.BlockSpec((C, ROPE_HALF), lambda bi, h, c: (c, 0)),
            pl.BlockSpec((C, ROPE_HALF), lambda bi, h, c: (c, 0)),
            pl.BlockSpec((1, 1, RET_DK, RET_DV), lambda bi, h, c: (bi, h, 0, 0)),
        ],
        out_specs=[
            pl.BlockSpec((1, c_real, vb), lambda bi, h, c: (bi, c, h)),
            pl.BlockSpec((1, 1, RET_DK, RET_DV), lambda bi, h, c: (bi, h, 0, 0)),
        ],
        out_shape=[jax.ShapeDtypeStruct((b, l, RET_VDIM), out_dtype),
                   jax.ShapeDtypeStruct((b, RET_HEADS, RET_DK, RET_DV), F32)],
        scratch_shapes=stage,
        compiler_params=_cparams(("parallel", "parallel", "arbitrary")),
        name="retention_core",
    )(lg, qkvg, qkvg, qkvg, qkvg, cos, sin, s0)


def _softplus(x):
    return jnp.maximum(x, 0.0) + jnp.log1p(jnp.exp(-jnp.abs(x)))


def _cumsum(x, axis):
    n = x.shape[axis]
    idx = lax.broadcasted_iota(jnp.int32, x.shape, axis)
    s = 1
    while s < n:
        x = x + jnp.where(idx >= s, pltpu.roll(x, s, axis), 0.0)
        s *= 2
    return x


def _ssd_kernel(z_ref, x_ref, bc_ref, dt_ref, dtt_ref, cw_ref, cb_ref, dtb_ref, dtbt_ref, al_ref, alt_ref,
                dsk_ref, ng_ref, buf_ref, s0_ref, y_ref, s1_ref, extx, extbc, xs_ref, bcs_ref, *, C, c_real):
    c = pl.program_id(1)
    halo = SUBLANES

    @pl.when(c == 0)
    def _():
        s1_ref[...] = s0_ref[...]
        extx[...] = jnp.zeros(extx.shape, F32)
        extbc[...] = jnp.zeros(extbc.shape, F32)
        extx[0:halo, :] = buf_ref[0, :, 0:SSD_INNER]
        extbc[0:halo, :] = buf_ref[0, :, SSD_INNER:SSD_CONV_CH]

    extx[halo:halo + c_real, :] = x_ref[0].astype(F32)
    extbc[halo:halo + c_real, :] = bc_ref[0].astype(F32)

    cw = 512
    first = halo - (SSD_CONV - 1)
    for ext, dst, ch0 in ((extx, xs_ref, 0), (extbc, bcs_ref, SSD_INNER)):
        for col in range(0, SSD_INNER, cw):
            acc = cb_ref[:, ch0 + col:ch0 + col + cw]
            for tap in range(SSD_CONV):
                acc = acc + ext[first + tap:first + tap + C, col:col + cw] * cw_ref[tap:tap + 1, ch0 + col:ch0 + col + cw]
            dst[:, col:col + cw] = _silu(acc).astype(dst.dtype)
    extx[0:halo, :] = extx[C:C + halo, :]
    extbc[0:halo, :] = extbc[C:C + halo, :]

    row = lax.broadcasted_iota(jnp.int32, (C, 1), 0)
    col_i = lax.broadcasted_iota(jnp.int32, (1, C), 1)
    dt = jnp.where(row < c_real, _softplus(dt_ref[0, 0] + dtb_ref[...]), 0.0)
    dtt = jnp.where(col_i < c_real, _softplus(dtt_ref[0, 0] + dtbt_ref[...]), 0.0)
    cs = _cumsum(dt * (-jnp.exp(al_ref[...])), 0)
    cst = _cumsum(dtt * (-jnp.exp(alt_ref[...])), 1)
    causal = row >= col_i
    lane_r = lax.broadcasted_iota(jnp.int32, (1, SSD_GW), 1) // SSD_HEADDIM
    sub_r = lax.broadcasted_iota(jnp.int32, (SSD_GW, 1), 0) // SSD_HEADDIM

    def by_head(vals, sel):
        out = vals[SSD_HPG - 1]
        for r in range(SSD_HPG - 2, -1, -1):
            out = jnp.where(sel == r, vals[r], out)
        return out

    for g in range(SSD_GROUPS):
        bm = bcs_ref[:, g * SSD_STATE:(g + 1) * SSD_STATE]
        cm = bcs_ref[:, SSD_BC + g * SSD_STATE:SSD_BC + (g + 1) * SSD_STATE]
        cb = _dot_nt(cm, bm)
        h0 = s1_ref[0, g * SSD_HPG:(g + 1) * SSD_HPG].reshape(SSD_GW, SSD_STATE)
        xg = xs_ref[:, g * SSD_GW:(g + 1) * SSD_GW]
        heads = range(g * SSD_HPG, (g + 1) * SSD_HPG)
        cs_cols = [cs[:, hh:hh + 1] for hh in heads]
        cs_last = [cs[C - 1:C, hh:hh + 1] for hh in heads]
        yg = _dot_nt(cm, h0.astype(BF16)) * by_head([jnp.exp(cc) for cc in cs_cols], lane_r)
        for r, hh in enumerate(heads):
            seg = cs_cols[r] - cst[hh:hh + 1, :]
            w = cb * jnp.exp(jnp.where(causal, seg, -jnp.inf)) * dtt[hh:hh + 1, :]
            yg = yg + _dot(w.astype(BF16), jnp.where(lane_r == r, xg, 0.0).astype(BF16))
        w_end = by_head([jnp.exp(cl - cc) * dt[:, hh:hh + 1] for cl, cc, hh in zip(cs_last, cs_cols, heads)], lane_r)
        h1 = by_head([jnp.exp(cl) for cl in cs_last], sub_r) * h0 + _dot_tn((xg * w_end).astype(BF16), bm)
        s1_ref[0, g * SSD_HPG:(g + 1) * SSD_HPG] = h1.reshape(SSD_HPG, SSD_HEADDIM, SSD_STATE)
        sl = slice(g * SSD_GW, (g + 1) * SSD_GW)
        yo = (yg[0:c_real] + dsk_ref[:, sl] * xg[0:c_real]) * _silu(z_ref[0, :, sl].astype(F32))
        yo = yo * lax.rsqrt(jnp.mean(yo * yo, axis=-1, keepdims=True) + EPS) * ng_ref[:, sl]
        y_ref[0, :, sl] = yo.astype(y_ref.dtype)


def ssd_core(zxbc, dt_raw, conv_buf, s0, conv_w, conv_b, dt_bias, a_log, d_skip, norm_g, c_real, out_dtype):
    b, l, _ = zxbc.shape
    C = CHUNK
    nc = l // c_real
    halo = SUBLANES
    dt4 = dt_raw.reshape(b, nc, c_real, LANES)
    dtt = jnp.swapaxes(dt4[..., :SSD_HEADS], -1, -2)
    if c_real < C:
        dt4 = jnp.pad(dt4, ((0, 0), (0, 0), (0, C - c_real), (0, 0)))
        dtt = jnp.pad(dtt, ((0, 0), (0, 0), (0, 0), (0, C - c_real)))
    lane_pad = LANES - SSD_HEADS
    buf8 = jnp.pad(conv_buf.astype(F32), ((0, 0), (halo - (SSD_CONV - 1), 0), (0, 0)))
    small = [
        conv_w.astype(F32), conv_b.reshape(1, -1).astype(F32),
        jnp.pad(dt_bias.astype(F32), (0, lane_pad)).reshape(1, LANES), dt_bias.astype(F32).reshape(SSD_HEADS, 1),
        jnp.pad(a_log.astype(F32), (0, lane_pad)).reshape(1, LANES), a_log.astype(F32).reshape(SSD_HEADS, 1),
        jnp.repeat(d_skip.astype(F32), SSD_HEADDIM).reshape(1, SSD_INNER), norm_g.astype(F32).reshape(1, SSD_INNER),
    ]
    w = SSD_INNER
    return pl.pallas_call(
        functools.partial(_ssd_kernel, C=C, c_real=c_real),
        grid=(b, nc),
        in_specs=[
            pl.BlockSpec((1, c_real, w), lambda bi, c: (bi, c, 0)),
            pl.BlockSpec((1, c_real, w), lambda bi, c: (bi, c, 1)),
            pl.BlockSpec((1, c_real, w), lambda bi, c: (bi, c, 2)),
            pl.BlockSpec((1, 1, C, LANES), lambda bi, c: (bi, c, 0, 0)),
            pl.BlockSpec((1, 1, SSD_HEADS, C), lambda bi, c: (bi, c, 0, 0)),
        ] + [_resident(a.shape) for a in small] + [
            pl.BlockSpec((1, halo, SSD_CONV_CH), lambda bi, c: (bi, 0, 0)),
            pl.BlockSpec((1, SSD_HEADS, SSD_HEADDIM, SSD_STATE), lambda bi, c: (bi, 0, 0, 0)),
        ],
        out_specs=[
            pl.BlockSpec((1, c_real, w), lambda bi, c: (bi, c, 0)),
            pl.BlockSpec((1, SSD_HEADS, SSD_HEADDIM, SSD_STATE), lambda bi, c: (bi, 0, 0, 0)),
        ],
        out_shape=[jax.ShapeDtypeStruct((b, l, SSD_INNER), out_dtype),
                   jax.ShapeDtypeStruct((b, SSD_HEADS, SSD_HEADDIM, SSD_STATE), F32)],
        scratch_shapes=[pltpu.VMEM((C + halo, w), F32), pltpu.VMEM((C + halo, w), F32),
                        pltpu.VMEM((C, w), F32), pltpu.VMEM((C, w), BF16)],
        compiler_params=_cparams(("parallel", "arbitrary")),
        name="ssd_core",
    )(zxbc, zxbc, zxbc, dt4, dtt, *small, buf8, s0)


def _t5_bias(dist, table_at):
    n = jnp.maximum(dist, 0)
    exact = REL_BUCKETS // 2
    nf = jnp.maximum(n, 1).astype(F32)
    large = exact + (jnp.log(nf / exact) / math.log(REL_MAX_DIST / exact) * (REL_BUCKETS - exact)).astype(jnp.int32)
    bucket = jnp.where(n < exact, n, jnp.minimum(large, REL_BUCKETS - 1))
    bias = jnp.zeros(dist.shape, F32)
    for bkt in range(REL_BUCKETS):
        bias = jnp.where(bucket == bkt, table_at(bkt), bias)
    return bias


def _lambda(lq1, lk1, lq2, lk2, lam_init):
    s1 = jnp.sum(lq1[...] * lk1[...], axis=-1, keepdims=True)
    s2 = jnp.sum(lq2[...] * lk2[...], axis=-1, keepdims=True)
    return jnp.exp(s1) - jnp.exp(s2) + lam_init


def _softmax_step(s, vt, m_ref, l_ref, acc_ref):
    m_prev = m_ref[...]
    m_new = jnp.maximum(m_prev, jnp.max(s, axis=-1, keepdims=True))
    alpha = jnp.exp(m_prev - m_new)
    p = jnp.exp(s - m_new)
    l_ref[...] = alpha * l_ref[...] + jnp.sum(p, axis=-1, keepdims=True)
    acc_ref[...] = alpha * acc_ref[...] + _dot(p.astype(BF16), vt)
    m_ref[...] = m_new


def _attn_prompt_kernel(tbl_ref, q_ref, k_ref, v_ref, lq1, lk1, lq2, lk2, sg_ref, o_ref,
                        bias_sc, m1, l1, a1, m2, l2, a2, *, T, lam_init):
    hd = pl.program_id(1)
    qi = pl.program_id(2)

    @pl.when(qi == 0)
    def _():
        i = lax.broadcasted_iota(jnp.int32, (T, T), 0)
        j = lax.broadcasted_iota(jnp.int32, (T, T), 1)
        for d in range(2):
            dist = i - j + d * T
            bias = _t5_bias(dist, lambda bkt: tbl_ref[bkt, hd])
            bias_sc[d] = jnp.where(dist >= 0, bias, -jnp.inf)

    lane = lax.broadcasted_iota(jnp.int32, (1, DIFF_DV), 1)
    q = q_ref[0].astype(F32) * (DIFF_DH ** -0.5)
    qa = jnp.where(lane < DIFF_DH, q, 0.0).astype(BF16)
    qb = jnp.where(lane >= DIFF_DH, q, 0.0).astype(BF16)
    for m_ref, l_ref, a_ref in ((m1, l1, a1), (m2, l2, a2)):
        m_ref[...] = jnp.full(m_ref.shape, -jnp.inf, F32)
        l_ref[...] = jnp.zeros(l_ref.shape, F32)
        a_ref[...] = jnp.zeros(a_ref.shape, F32)

    def update(ki, bias):
        off = pl.multiple_of(ki * T, T)
        kt = k_ref[0, pl.ds(off, T), :]
        vt = v_ref[0, pl.ds(off, T), :]
        _softmax_step(_dot_nt(qa, kt) + bias, vt, m1, l1, a1)
        _softmax_step(_dot_nt(qb, kt) + bias, vt, m2, l2, a2)

    far_bias = tbl_ref[REL_BUCKETS - 1, hd]

    def far_body(ki, carry):
        update(ki, far_bias)
        return carry

    lax.fori_loop(0, jnp.maximum(qi - 1, 0), far_body, 0)

    @pl.when(qi >= 1)
    def _():
        update(qi - 1, bias_sc[1])

    update(qi, bias_sc[0])

    lam = _lambda(lq1, lk1, lq2, lk2, lam_init)
    o = a1[...] / l1[...] - lam * (a2[...] / l2[...])
    o_ref[0] = (_rms(o, sg_ref[...]) * (1.0 - lam_init)).astype(o_ref.dtype)


def diff_attention_prompt(q, kv, rel_table, lams, subln_g, lam_init, T=256):
    b, l, _ = q.shape
    assert T >= REL_MAX_DIST and l % T == 0
    vec = lambda a: a.astype(F32).reshape(1, -1)
    return pl.pallas_call(
        functools.partial(_attn_prompt_kernel, T=T, lam_init=lam_init),
        grid=(b, DIFF_HEADS, l // T),
        in_specs=[
            pl.BlockSpec(memory_space=pltpu.SMEM),
            pl.BlockSpec((1, T, DIFF_DV), lambda bi, h, qi: (bi, qi, h)),
            pl.BlockSpec((1, l, DIFF_DV), lambda bi, h, qi: (bi, 0, h)),
            pl.BlockSpec((1, l, DIFF_DV), lambda bi, h, qi: (bi, 0, DIFF_HEADS + h)),
        ] + [_resident((1, DIFF_DH))] * 4 + [_resident((1, DIFF_DV))],
        out_specs=pl.BlockSpec((1, T, DIFF_DV), lambda bi, h, qi: (bi, qi, h)),
        out_shape=jax.ShapeDtypeStruct((b, l, DIFF_HEADS * DIFF_DV), BF16),
        scratch_shapes=[pltpu.VMEM((2, T, T), F32)] + [pltpu.VMEM((T, 1), F32), pltpu.VMEM((T, 1), F32),
                                                      pltpu.VMEM((T, DIFF_DV), F32)] * 2,
        compiler_params=_cparams(("parallel", "parallel", "arbitrary")),
        name="diff_attention_prompt",
    )(rel_table.astype(F32), q, kv, kv, *[vec(a) for a in lams], vec(subln_g))


DEC_ROWS = 16


def _attn_sample_kernel(pt_ref, q_ref, kn_ref, vn_ref, tb_ref, lq1, lk1, lq2, lk2, sg_ref, *rest,
                        pps, n_pages, lq, lam_init):
    k_refs, v_refs = rest[:pps], rest[pps:2 * pps]
    o_ref, qm_sc, m_sc, l_sc, acc_sc, kst, vst = rest[2 * pps:]
    s = pl.program_id(1)
    n_steps = n_pages // pps
    past = n_pages * PAGE_SIZE
    rows = lq * DEC_ROWS
    width = DIFF_HEADS * DIFF_DV

    @pl.when(s == 0)
    def _():
        pair = lax.broadcasted_iota(jnp.int32, (DEC_ROWS, width), 0)
        lane_pair = lax.broadcasted_iota(jnp.int32, (DEC_ROWS, width), 1) // DIFF_DH
        for t in range(lq):
            qt = q_ref[0, t:t + 1, :] * (DIFF_DH ** -0.5)
            qm_sc[t * DEC_ROWS:(t + 1) * DEC_ROWS, :] = jnp.where(pair == lane_pair, qt, 0.0).astype(BF16)
        m_sc[...] = jnp.full(m_sc.shape, -jnp.inf, F32)
        l_sc[...] = jnp.zeros(l_sc.shape, F32)
        acc_sc[...] = jnp.zeros(acc_sc.shape, F32)

    row_tok = lax.broadcasted_iota(jnp.int32, (rows, 1), 0) // DEC_ROWS

    def update(kt, vt, kpos0):
        n = kt.shape[0]
        sc = _dot_nt(qm_sc[...], kt)
        dist = (past + row_tok) - (kpos0 + lax.broadcasted_iota(jnp.int32, (1, n), 1))
        bias = _t5_bias(dist, lambda bkt: tb_ref[:, bkt:bkt + 1])
        _softmax_step(jnp.where(dist >= 0, sc + bias, -jnp.inf), vt, m_sc, l_sc, acc_sc)

    @pl.when(s < n_steps)
    def _():
        for i in range(pps):
            update(k_refs[i][0].astype(BF16), v_refs[i][0].astype(BF16), (s * pps + i) * PAGE_SIZE)

    @pl.when(s == n_steps)
    def _():
        kst[...] = jnp.zeros(kst.shape, F32)
        vst[...] = jnp.zeros(vst.shape, F32)
        kst[0:lq, :] = kn_ref[0]
        vst[0:lq, :] = vn_ref[0]
        update(kst[...].astype(BF16), vst[...].astype(BF16), past)

        lam = _lambda(lq1, lk1, lq2, lk2, lam_init)
        rid = lax.broadcasted_iota(jnp.int32, (rows, 1), 0) % DEC_ROWS
        coef = jnp.where(rid % 2 == 0, 1.0, -lam)
        lane_head = lax.broadcasted_iota(jnp.int32, (1, width), 1) // DIFF_DV
        a = jnp.where(rid // 2 == lane_head, acc_sc[...] * (coef / l_sc[...]), 0.0)
        for t in range(lq):
            ot = jnp.sum(a[t * DEC_ROWS:(t + 1) * DEC_ROWS], axis=0, keepdims=True)
            for hh in range(DIFF_HEADS):
                sl = slice(hh * DIFF_DV, (hh + 1) * DIFF_DV)
                o_ref[0, t:t + 1, sl] = _rms(ot[:, sl], sg_ref[...]) * (1.0 - lam_init)


def diff_attention_sample(q, k_new, v_new, cache_k, cache_v, page_table, rel_table, lams, subln_g, lam_init, pps=4):
    b, lq, width = q.shape
    n_pages = page_table.shape[1]
    assert n_pages % pps == 0 and lq <= SUBLANES
    n_steps = n_pages // pps
    rows = lq * DEC_ROWS
    head_of_row = (np.arange(rows) % DEC_ROWS) // 2
    tb = jnp.pad(rel_table.astype(F32).T[head_of_row], ((0, 0), (0, LANES - REL_BUCKETS)))
    vec = lambda a: a.astype(F32).reshape(1, -1)

    def page_map(i):
        return lambda bi, s, pt: (pt[bi * n_pages + jnp.minimum(s, n_steps - 1) * pps + i], 0, 0)

    tok = pl.BlockSpec((1, lq, width), lambda bi, s, pt: (bi, 0, 0))
    const = lambda shape: pl.BlockSpec(shape, lambda bi, s, pt: (0,) * len(shape))
    page = [pl.BlockSpec((1, PAGE_SIZE, width), page_map(i)) for i in range(pps)]
    grid_spec = pltpu.PrefetchScalarGridSpec(
        num_scalar_prefetch=1,
        grid=(b, n_steps + 1),
        in_specs=[tok, tok, tok, const((rows, LANES))] + [const((1, DIFF_DH))] * 4 + [const((1, DIFF_DV))] + page + page,
        out_specs=tok,
        scratch_shapes=[pltpu.VMEM((rows, width), BF16), pltpu.VMEM((rows, 1), F32), pltpu.VMEM((rows, 1), F32),
                        pltpu.VMEM((rows, width), F32), pltpu.VMEM((PAGE_SIZE, width), F32),
                        pltpu.VMEM((PAGE_SIZE, width), F32)],
    )
    return pl.pallas_call(
        functools.partial(_attn_sample_kernel, pps=pps, n_pages=n_pages, lq=lq, lam_init=lam_init),
        grid_spec=grid_spec,
        out_shape=jax.ShapeDtypeStruct((b, lq, width), F32),
        compiler_params=_cparams(("parallel", "arbitrary")),
        name="diff_attention_sample",
    )(page_table.reshape(-1).astype(jnp.int32), q, k_new, v_new, tb, *[vec(a) for a in lams], vec(subln_g),
      *([cache_k] * pps), *([cache_v] * pps))


def kernel(x_prompt, x_sample, state_ret, state_ssm, state_conv, cache_k_diff, cache_v_diff, page_table, norm_mix_g, norm_ffn_g, norm_final_g, ret_w_in, ret_w_out, ssd_w_in, ssd_conv_w, ssd_conv_b, ssd_dt_bias, ssd_A_log, ssd_D, ssd_norm_g, ssd_w_out, diff_w_in, diff_lam_q1, diff_lam_k1, diff_lam_q2, diff_lam_k2, diff_subln_g, diff_w_out, rel_bias_table, ffn_w_in, ffn_w_out):
    kinds = tuple(i % 3 for i in range(DEPTH))
    n_pages = page_table.shape[1]
    bf = lambda a: a.astype(BF16)
    ret_wi, ret_wo = bf(ret_w_in), bf(ret_w_out)
    ssd_wmain = bf(ssd_w_in[:, :, :SSD_INNER + SSD_CONV_CH])
    ssd_wdt = bf(jnp.pad(ssd_w_in[:, :, SSD_INNER + SSD_CONV_CH:], ((0, 0), (0, 0), (0, LANES - SSD_HEADS))))
    ssd_wo = bf(ssd_w_out)
    diff_wi, diff_wo = bf(diff_w_in), bf(diff_w_out)
    ffn_wi, ffn_wo = bf(ffn_w_in), bf(ffn_w_out)

    def run_group(x, sample):
        b, l, d = x.shape
        t = b * l
        tm = 512 if t % 512 == 0 else t
        c_real = CHUNK if l % CHUNK == 0 else l
        act = F32 if sample else BF16
        offset = n_pages * PAGE_SIZE if sample else 0
        rope_rows = l if l % CHUNK == 0 else CHUNK
        cos, sin = rope_tables(offset, rope_rows)
        ret_new, ssm_new, conv_new, k_new, v_new = [], [], [], [], []
        h = x.reshape(t, d)
        for i in range(DEPTH):
            kind = kinds[i]
            j = kinds[:i].count(kind)
            g = norm_mix_g[i]
            if kind == 0:
                n_in = ret_wi.shape[2]
                (qkvg,) = norm_linear(h, g, ret_wi[j], [(0, n_in, act)], tm)
                s0 = state_ret[j] if sample else jnp.zeros((b, RET_HEADS, RET_DK, RET_DV), F32)
                y, s1 = retention_core(qkvg.reshape(b, l, n_in), cos, sin, s0, c_real, act)
                ret_new.append(s1)
                h = linear_residual(y.reshape(t, RET_VDIM), ret_wo[j], h, tm)
            elif kind == 1:
                n_in = SSD_INNER + SSD_CONV_CH
                (zxbc,) = norm_linear(h, g, ssd_wmain[j], [(0, n_in, act)], tm)
                (dt_raw,) = norm_linear(h, g, ssd_wdt[j], [(0, LANES, F32)], tm)
                zxbc = zxbc.reshape(b, l, n_in)
                if sample:
                    buf, s0 = state_conv[j], state_ssm[j]
                    conv_new.append(zxbc[:, l - (SSD_CONV - 1):, SSD_INNER:])
                else:
                    buf = jnp.zeros((b, SSD_CONV - 1, SSD_CONV_CH), F32)
                    s0 = jnp.zeros((b, SSD_HEADS, SSD_HEADDIM, SSD_STATE), F32)
                    tail = h.reshape(b, l, d)[:, l - (SSD_CONV - 1):].reshape(b * (SSD_CONV - 1), d)
                    rows = -(-tail.shape[0] // 16) * 16
                    tail = jnp.pad(tail, ((0, rows - tail.shape[0]), (0, 0)), constant_values=1.0)
                    (xbc_tail,) = norm_linear(tail, g, ssd_wmain[j][:, SSD_INNER:], [(0, SSD_CONV_CH, F32)], rows)
                    conv_new.append(xbc_tail[:b * (SSD_CONV - 1)].reshape(b, SSD_CONV - 1, SSD_CONV_CH))
                y, s1 = ssd_core(zxbc, dt_raw.reshape(b, l, LANES), buf, s0, ssd_conv_w[j], ssd_conv_b[j],
                                 ssd_dt_bias[j], ssd_A_log[j], ssd_D[j], ssd_norm_g[j], c_real, act)
                ssm_new.append(s1)
                h = linear_residual(y.reshape(t, SSD_INNER), ssd_wo[j], h, tm)
            else:
                lam_init = 0.8 - 0.6 * math.exp(-0.3 * i)
                lams = (diff_lam_q1[j], diff_lam_k1[j], diff_lam_q2[j], diff_lam_k2[j])
                if sample:
                    q, kn, vn = norm_linear(h, g, diff_wi[j], [(0, d, F32), (d, 2 * d, F32), (2 * d, 3 * d, F32)], tm)
                    y = diff_attention_sample(
                        q.reshape(b, l, d), kn.reshape(b, l, d), vn.reshape(b, l, d),
                        cache_k_diff[j].reshape(-1, PAGE_SIZE, d), cache_v_diff[j].reshape(-1, PAGE_SIZE, d),
                        page_table, rel_bias_table, lams, diff_subln_g[j], lam_init)
                else:
                    q, kv, kn, vn = norm_linear(
                        h, g, diff_wi[j], [(0, d, BF16), (d, 3 * d, BF16), (d, 2 * d, F32), (2 * d, 3 * d, F32)], tm)
                    y = diff_attention_prompt(q.reshape(b, l, d), kv.reshape(b, l, 2 * d), rel_bias_table, lams,
                                              diff_subln_g[j], lam_init)
                k_new.append(kn.reshape(b, l, DIFF_HEADS, 2, DIFF_DH))
                v_new.append(vn.reshape(b, l, DIFF_HEADS, DIFF_DV))
                h = linear_residual(y.reshape(t, d), diff_wo[j], h, tm)
            h = ffn(h, norm_ffn_g[i], ffn_wi[i], ffn_wo[i], min(tm, 256))
        y = final_norm(h, norm_final_g, tm).reshape(b, l, d)
        return y, jnp.stack(ret_new), jnp.stack(ssm_new), jnp.stack(conv_new), jnp.stack(k_new), jnp.stack(v_new)

    y_p, ret_p, ssm_p, conv_p, k_p, v_p = run_group(x_prompt, False)
    y_s, ret_s, ssm_s, conv_s, k_s, v_s = run_group(x_sample, True)
    return (y_p, y_s, ret_p, ssm_p, conv_p, k_p, v_p, ret_s, ssm_s, conv_s, k_s, v_s)
```

```python
import functools
import math

import jax
import jax.numpy as jnp
import numpy as np
from jax import lax
from jax.experimental import pallas as pl
from jax.experimental.pallas import tpu as pltpu

F32 = jnp.float32
BF16 = jnp.bfloat16

D_MODEL = 1024
DEPTH = 4
PAGE_SIZE = 128
EPS = 1e-6
CHUNK = 128
ATTN_TILE = 256

RET_HEADS = 4
RET_DK = D_MODEL // RET_HEADS
RET_DV = 2 * RET_DK
RET_VDIM = RET_HEADS * RET_DV
ROPE_BASE = 10000.0
ROPE_HALF = RET_DK // 2

SSD_INNER = 2 * D_MODEL
SSD_HEADDIM = 64
SSD_HEADS = SSD_INNER // SSD_HEADDIM
SSD_GROUPS = 8
SSD_HPG = SSD_HEADS // SSD_GROUPS
SSD_STATE = 128
SSD_CONV = 4
SSD_BC = SSD_GROUPS * SSD_STATE
SSD_CONV_CH = SSD_INNER + 2 * SSD_BC
SSD_GW = SSD_HPG * SSD_HEADDIM

DIFF_HEADS = 8
DIFF_DH = D_MODEL // DIFF_HEADS // 2
DIFF_DV = 2 * DIFF_DH
REL_BUCKETS = 32
REL_MAX_DIST = 128

FFN_HIDDEN = -(-8 * D_MODEL // (3 * 256)) * 256

V7X_VMEM_BYTES = 64 * 1024 * 1024
LANES = 128
SUBLANES = 8
VMEM_LIMIT = 56 * 1024 * 1024


def _cparams(sem):
    return pltpu.CompilerParams(dimension_semantics=sem, vmem_limit_bytes=VMEM_LIMIT)


def _dot(a, b):
    return jnp.dot(a, b, preferred_element_type=F32)


def _dot_nt(a, b):
    return lax.dot_general(a, b, (((1,), (1,)), ((), ())), preferred_element_type=F32)


def _dot_tn(a, b):
    return lax.dot_general(a, b, (((0,), (0,)), ((), ())), preferred_element_type=F32)


def _rms(x, g):
    return x * lax.rsqrt(jnp.mean(x * x, axis=-1, keepdims=True) + EPS) * g


def _silu(x):
    return x * jax.nn.sigmoid(x)


def _resident(shape):
    return pl.BlockSpec(shape, lambda *_: (0,) * len(shape), pipeline_mode=pl.Buffered(1))


def _norm_linear_kernel(x_ref, g_ref, w_ref, *o_refs, outs, tn):
    xn = _rms(x_ref[...], g_ref[...]).astype(BF16)
    n = w_ref.shape[1]
    for c0 in range(0, n, tn):
        acc = _dot(xn, w_ref[:, c0:c0 + tn])
        for o_ref, (lo, hi, _) in zip(o_refs, outs):
            if lo <= c0 and c0 + tn <= hi:
                o_ref[:, c0 - lo:c0 - lo + tn] = acc.astype(o_ref.dtype)


def norm_linear(x, g, w, outs, tm, tn=512):
    t, d = x.shape
    n = w.shape[1]
    tn = min(tn, n)
    assert t % tm == 0 and n % tn == 0 and all(lo % tn == 0 and hi % tn == 0 for lo, hi, _ in outs)
    return pl.pallas_call(
        functools.partial(_norm_linear_kernel, outs=tuple(outs), tn=tn),
        grid=(t // tm,),
        in_specs=[pl.BlockSpec((tm, d), lambda i: (i, 0)), _resident((1, d)), _resident((d, n))],
        out_specs=[pl.BlockSpec((tm, hi - lo), lambda i: (i, 0)) for lo, hi, _ in outs],
        out_shape=[jax.ShapeDtypeStruct((t, hi - lo), dt) for lo, hi, dt in outs],
        compiler_params=_cparams(("parallel",)),
        name="norm_linear",
    )(x, g.reshape(1, d), w)


def _linear_residual_kernel(y_ref, w_ref, h_ref, o_ref):
    o_ref[...] = h_ref[...] + _dot(y_ref[...].astype(BF16), w_ref[...])


def linear_residual(y, w, h, tm):
    t, k = y.shape
    d = w.shape[1]
    return pl.pallas_call(
        _linear_residual_kernel,
        grid=(t // tm,),
        in_specs=[pl.BlockSpec((tm, k), lambda i: (i, 0)), _resident((k, d)), pl.BlockSpec((tm, d), lambda i: (i, 0))],
        out_specs=pl.BlockSpec((tm, d), lambda i: (i, 0)),
        out_shape=jax.ShapeDtypeStruct((t, d), F32),
        compiler_params=_cparams(("parallel",)),
        name="linear_residual",
    )(y, w, h)


def _ffn_kernel(h_ref, g_ref, wgu_ref, wo_ref, o_ref, *, th):
    h = h_ref[...]
    xn = _rms(h, g_ref[...]).astype(BF16)
    hidden = wo_ref.shape[0]
    o_ref[...] = h
    for c0 in range(0, hidden, th):
        gate = _dot(xn, wgu_ref[:, c0:c0 + th])
        up = _dot(xn, wgu_ref[:, hidden + c0:hidden + c0 + th])
        act = (_silu(gate) * up).astype(BF16)
        o_ref[...] += _dot(act, wo_ref[c0:c0 + th, :])


def ffn(h, g, wgu, wo, tm, th=256):
    t, d = h.shape
    hidden = wo.shape[0]
    assert hidden % th == 0 and t % tm == 0
    return pl.pallas_call(
        functools.partial(_ffn_kernel, th=th),
        grid=(t // tm,),
        in_specs=[pl.BlockSpec((tm, d), lambda i: (i, 0)), _resident((1, d)), _resident((d, 2 * hidden)),
                  _resident((hidden, d))],
        out_specs=pl.BlockSpec((tm, d), lambda i: (i, 0)),
        out_shape=jax.ShapeDtypeStruct((t, d), F32),
        compiler_params=_cparams(("parallel",)),
        name="ffn",
    )(h, g.reshape(1, d), wgu, wo)


def _final_norm_kernel(h_ref, g_ref, o_ref):
    o_ref[...] = _rms(h_ref[...], g_ref[...])


def final_norm(h, g, tm):
    t, d = h.shape
    return pl.pallas_call(
        _final_norm_kernel,
        grid=(t // tm,),
        in_specs=[pl.BlockSpec((tm, d), lambda i: (i, 0)), _resident((1, d))],
        out_specs=pl.BlockSpec((tm, d), lambda i: (i, 0)),
        out_shape=jax.ShapeDtypeStruct((t, d), F32),
        compiler_params=_cparams(("parallel",)),
        name="final_norm",
    )(h, g.reshape(1, d))


def _rope_kernel(inv_ref, cos_ref, sin_ref, *, offset):
    rows = cos_ref.shape[0]
    pos = (offset + pl.program_id(0) * rows + lax.broadcasted_iota(jnp.int32, (rows, 1), 0)).astype(F32)
    ang = pos * inv_ref[...]
    cos_ref[...] = jnp.cos(ang)
    sin_ref[...] = jnp.sin(ang)


def rope_tables(offset, rows):
    inv = (1.0 / (ROPE_BASE ** jnp.linspace(0.0, 1.0, ROPE_HALF, dtype=F32))).reshape(1, ROPE_HALF)
    tr = min(rows, 512)
    return pl.pallas_call(
        functools.partial(_rope_kernel, offset=offset),
        grid=(rows // tr,),
        in_specs=[_resident((1, ROPE_HALF))],
        out_specs=[pl.BlockSpec((tr, ROPE_HALF), lambda i: (i, 0))] * 2,
        out_shape=[jax.ShapeDtypeStruct((rows, ROPE_HALF), F32)] * 2,
        compiler_params=_cparams(("parallel",)),
        name="rope_tables",
    )(inv)


def _ret_kernel(lg_ref, q_ref, k_ref, v_ref, g_ref, cos_ref, sin_ref, s0_ref, y_ref, s1_ref, *stage, C, c_real):
    hd = pl.program_id(1)
    c = pl.program_id(2)
    lg = lg_ref[hd]

    @pl.when(c == 0)
    def _():
        s1_ref[...] = s0_ref[...]

    def load(ref, st):
        if c_real == C:
            return ref[0].astype(F32)
        st[...] = jnp.zeros(st.shape, F32)
        st[0:c_real, :] = ref[0].astype(F32)
        return st[...]

    st = stage if stage else (None,) * 3
    q = load(q_ref, st[0])
    k = load(k_ref, st[1])
    v = load(v_ref, st[2]).astype(BF16)
    cos = cos_ref[...]
    sin = sin_ref[...]

    def rot(x):
        x1, x2 = x[:, :ROPE_HALF], x[:, ROPE_HALF:]
        return jnp.concatenate([x1 * cos - x2 * sin, x2 * cos + x1 * sin], axis=-1)

    qr = rot(q).astype(BF16)
    kr = rot(k) * (RET_DK ** -0.5)
    i = lax.broadcasted_iota(jnp.int32, (C, 1), 0).astype(F32)
    j = lax.broadcasted_iota(jnp.int32, (1, C), 1).astype(F32)
    rel = i - j
    decay = jnp.where(rel >= 0, jnp.exp(jnp.maximum(rel, 0.0) * lg), 0.0)
    scores = _dot_nt(qr, kr.astype(BF16)) * decay
    s_prev = s1_ref[0, 0]
    o = _dot(scores.astype(BF16), v) + _dot(qr, s_prev.astype(BF16)) * jnp.exp((i + 1.0) * lg)
    kw = (kr * jnp.exp((c_real - 1.0 - i) * lg)).astype(BF16)
    chunk_decay = jnp.exp(jnp.zeros((1, 1), F32) + c_real * lg)
    s1_ref[0, 0] = chunk_decay * s_prev + _dot_tn(kw, v)
    o = o * lax.rsqrt(jnp.mean(o * o, axis=-1, keepdims=True) + EPS)
    o = o[0:c_real]
    y_ref[0] = (_silu(g_ref[0].astype(F32)) * o).astype(y_ref.dtype)


def retention_core(qkvg, cos, sin, s0, c_real, out_dtype):
    b, l, _ = qkvg.shape
    C = CHUNK
    nc = l // c_real
    kb, vb = RET_DK, RET_DV
    k_off = RET_HEADS
    v_off = 2 * RET_HEADS * RET_DK // RET_DV
    g_off = v_off + RET_HEADS
    lg = jnp.asarray(np.log(1.0 - 2.0 ** (-5.0 - np.arange(RET_HEADS))), F32)
    stage = [] if c_real == C else [pltpu.VMEM((C, kb), F32), pltpu.VMEM((C, kb), F32), pltpu.VMEM((C, vb), F32)]
    return pl.pallas_call(
        functools.partial(_ret_kernel, C=C, c_real=c_real),
        grid=(b, RET_HEADS, nc),
        in_specs=[
            pl.BlockSpec(memory_space=pltpu.SMEM),
            pl.BlockSpec((1, c_real, kb), lambda bi, h, c: (bi, c, h)),
            pl.BlockSpec((1, c_real, kb), lambda bi, h, c: (bi, c, k_off + h)),
            pl.BlockSpec((1, c_real, vb), lambda bi, h, c: (bi, c, v_off + h)),
            pl.BlockSpec((1, c_real, vb), lambda bi, h, c: (bi, c, g_off + h)),
            pl.BlockSpec((C, ROPE_HALF), lambda bi, h, c: (c, 0)),
            pl.BlockSpec((C, ROPE_HALF), lambda bi, h, c: (c, 0)),
            pl.BlockSpec((1, 1, RET_DK, RET_DV), lambda bi, h, c: (bi, h, 0, 0)),
        ],
        out_specs=[
            pl.BlockSpec((1, c_real, vb), lambda bi, h, c: (bi, c, h)),
            pl.BlockSpec((1, 1, RET_DK, RET_DV), lambda bi, h, c: (bi, h, 0, 0)),
        ],
        out_shape=[jax.ShapeDtypeStruct((b, l, RET_VDIM), out_dtype),
                   jax.ShapeDtypeStruct((b, RET_HEADS, RET_DK, RET_DV), F32)],
        scratch_shapes=stage,
        compiler_params=_cparams(("parallel", "parallel", "arbitrary")),
        name="retention_core",
    )(lg, qkvg, qkvg, qkvg, qkvg, cos, sin, s0)


def _softplus(x):
    return jnp.maximum(x, 0.0) + jnp.log1p(jnp.exp(-jnp.abs(x)))


def _cumsum(x, axis):
    n = x.shape[axis]
    idx = lax.broadcasted_iota(jnp.int32, x.shape, axis)
    s = 1
    while s < n:
        x = x + jnp.where(idx >= s, pltpu.roll(x, s, axis), 0.0)
        s *= 2
    return x


def _ssd_kernel(z_ref, x_ref, bc_ref, dt_ref, dtt_ref, cw_ref, cb_ref, dtb_ref, dtbt_ref, al_ref, alt_ref,
                dsk_ref, ng_ref, buf_ref, s0_ref, y_ref, s1_ref, extx, extbc, xs_ref, bcs_ref, *, C, c_real):
    c = pl.program_id(1)
    halo = SUBLANES

    @pl.when(c == 0)
    def _():
        s1_ref[...] = s0_ref[...]
        extx[...] = jnp.zeros(extx.shape, F32)
        extbc[...] = jnp.zeros(extbc.shape, F32)
        extx[0:halo, :] = buf_ref[0, :, 0:SSD_INNER]
        extbc[0:halo, :] = buf_ref[0, :, SSD_INNER:SSD_CONV_CH]

    extx[halo:halo + c_real, :] = x_ref[0].astype(F32)
    extbc[halo:halo + c_real, :] = bc_ref[0].astype(F32)

    cw = 512
    first = halo - (SSD_CONV - 1)
    for ext, dst, ch0 in ((extx, xs_ref, 0), (extbc, bcs_ref, SSD_INNER)):
        for col in range(0, SSD_INNER, cw):
            acc = cb_ref[:, ch0 + col:ch0 + col + cw]
            for tap in range(SSD_CONV):
                acc = acc + ext[first + tap:first + tap + C, col:col + cw] * cw_ref[tap:tap + 1, ch0 + col:ch0 + col + cw]
            dst[:, col:col + cw] = _silu(acc).astype(dst.dtype)
    extx[0:halo, :] = extx[C:C + halo, :]
    extbc[0:halo, :] = extbc[C:C + halo, :]

    row = lax.broadcasted_iota(jnp.int32, (C, 1), 0)
    col_i = lax.broadcasted_iota(jnp.int32, (1, C), 1)
    dt = jnp.where(row < c_real, _softplus(dt_ref[0, 0] + dtb_ref[...]), 0.0)
    dtt = jnp.where(col_i < c_real, _softplus(dtt_ref[0, 0] + dtbt_ref[...]), 0.0)
    cs = _cumsum(dt * (-jnp.exp(al_ref[...])), 0)
    cst = _cumsum(dtt * (-jnp.exp(alt_ref[...])), 1)
    causal = row >= col_i
    lane_r = lax.broadcasted_iota(jnp.int32, (1, SSD_GW), 1) // SSD_HEADDIM
    sub_r = lax.broadcasted_iota(jnp.int32, (SSD_GW, 1), 0) // SSD_HEADDIM

    def by_head(vals, sel):
        out = vals[SSD_HPG - 1]
        for r in range(SSD_HPG - 2, -1, -1):
            out = jnp.where(sel == r, vals[r], out)
        return out

    for g in range(SSD_GROUPS):
        bm = bcs_ref[:, g * SSD_STATE:(g + 1) * SSD_STATE]
        cm = bcs_ref[:, SSD_BC + g * SSD_STATE:SSD_BC + (g + 1) * SSD_STATE]
        cb = _dot_nt(cm, bm)
        h0 = s1_ref[0, g * SSD_HPG:(g + 1) * SSD_HPG].reshape(SSD_GW, SSD_STATE)
        xg = xs_ref[:, g * SSD_GW:(g + 1) * SSD_GW]
        heads = range(g * SSD_HPG, (g + 1) * SSD_HPG)
        cs_cols = [cs[:, hh:hh + 1] for hh in heads]
        cs_last = [cs[C - 1:C, hh:hh + 1] for hh in heads]
        yg = _dot_nt(cm, h0.astype(BF16)) * by_head([jnp.exp(cc) for cc in cs_cols], lane_r)
        for r, hh in enumerate(heads):
            seg = cs_cols[r] - cst[hh:hh + 1, :]
            w = cb * jnp.exp(jnp.where(causal, seg, -jnp.inf)) * dtt[hh:hh + 1, :]
            yg = yg + _dot(w.astype(BF16), jnp.where(lane_r == r, xg, 0.0).astype(BF16))
        w_end = by_head([jnp.exp(cl - cc) * dt[:, hh:hh + 1] for cl, cc, hh in zip(cs_last, cs_cols, heads)], lane_r)
        h1 = by_head([jnp.exp(cl) for cl in cs_last], sub_r) * h0 + _dot_tn((xg * w_end).astype(BF16), bm)
        s1_ref[0, g * SSD_HPG:(g + 1) * SSD_HPG] = h1.reshape(SSD_HPG, SSD_HEADDIM, SSD_STATE)
        sl = slice(g * SSD_GW, (g + 1) * SSD_GW)
        yo = (yg[0:c_real] + dsk_ref[:, sl] * xg[0:c_real]) * _silu(z_ref[0, :, sl].astype(F32))
        yo = yo * lax.rsqrt(jnp.mean(yo * yo, axis=-1, keepdims=True) + EPS) * ng_ref[:, sl]
        y_ref[0, :, sl] = yo.astype(y_ref.dtype)


def ssd_core(zxbc, dt_raw, conv_buf, s0, conv_w, conv_b, dt_bias, a_log, d_skip, norm_g, c_real, out_dtype):
    b, l, _ = zxbc.shape
    C = CHUNK
    nc = l // c_real
    halo = SUBLANES
    dt4 = dt_raw.reshape(b, nc, c_real, LANES)
    dtt = jnp.swapaxes(dt4[..., :SSD_HEADS], -1, -2)
    if c_real < C:
        dt4 = jnp.pad(dt4, ((0, 0), (0, 0), (0, C - c_real), (0, 0)))
        dtt = jnp.pad(dtt, ((0, 0), (0, 0), (0, 0), (0, C - c_real)))
    lane_pad = LANES - SSD_HEADS
    buf8 = jnp.pad(conv_buf.astype(F32), ((0, 0), (halo - (SSD_CONV - 1), 0), (0, 0)))
    small = [
        conv_w.astype(F32), conv_b.reshape(1, -1).astype(F32),
        jnp.pad(dt_bias.astype(F32), (0, lane_pad)).reshape(1, LANES), dt_bias.astype(F32).reshape(SSD_HEADS, 1),
        jnp.pad(a_log.astype(F32), (0, lane_pad)).reshape(1, LANES), a_log.astype(F32).reshape(SSD_HEADS, 1),
        jnp.repeat(d_skip.astype(F32), SSD_HEADDIM).reshape(1, SSD_INNER), norm_g.astype(F32).reshape(1, SSD_INNER),
    ]
    w = SSD_INNER
    return pl.pallas_call(
        functools.partial(_ssd_kernel, C=C, c_real=c_real),
        grid=(b, nc),
        in_specs=[
            pl.BlockSpec((1, c_real, w), lambda bi, c: (bi, c, 0)),
            pl.BlockSpec((1, c_real, w), lambda bi, c: (bi, c, 1)),
            pl.BlockSpec((1, c_real, w), lambda bi, c: (bi, c, 2)),
            pl.BlockSpec((1, 1, C, LANES), lambda bi, c: (bi, c, 0, 0)),
            pl.BlockSpec((1, 1, SSD_HEADS, C), lambda bi, c: (bi, c, 0, 0)),
        ] + [_resident(a.shape) for a in small] + [
            pl.BlockSpec((1, halo, SSD_CONV_CH), lambda bi, c: (bi, 0, 0)),
            pl.BlockSpec((1, SSD_HEADS, SSD_HEADDIM, SSD_STATE), lambda bi, c: (bi, 0, 0, 0)),
        ],
        out_specs=[
            pl.BlockSpec((1, c_real, w), lambda bi, c: (bi, c, 0)),
            pl.BlockSpec((1, SSD_HEADS, SSD_HEADDIM, SSD_STATE), lambda bi, c: (bi, 0, 0, 0)),
        ],
        out_shape=[jax.ShapeDtypeStruct((b, l, SSD_INNER), out_dtype),
                   jax.ShapeDtypeStruct((b, SSD_HEADS, SSD_HEADDIM, SSD_STATE), F32)],
        scratch_shapes=[pltpu.VMEM((C + halo, w), F32), pltpu.VMEM((C + halo, w), F32),
                        pltpu.VMEM((C, w), F32), pltpu.VMEM((C, w), BF16)],
        compiler_params=_cparams(("parallel", "arbitrary")),
        name="ssd_core",
    )(zxbc, zxbc, zxbc, dt4, dtt, *small, buf8, s0)


def _t5_bias(dist, table_at):
    n = jnp.maximum(dist, 0)
    exact = REL_BUCKETS // 2
    nf = jnp.maximum(n, 1).astype(F32)
    large = exact + (jnp.log(nf / exact) / math.log(REL_MAX_DIST / exact) * (REL_BUCKETS - exact)).astype(jnp.int32)
    bucket = jnp.where(n < exact, n, jnp.minimum(large, REL_BUCKETS - 1))
    bias = jnp.zeros(dist.shape, F32)
    for bkt in range(REL_BUCKETS):
        bias = jnp.where(bucket == bkt, table_at(bkt), bias)
    return bias


def _lambda(lq1, lk1, lq2, lk2, lam_init):
    s1 = jnp.sum(lq1[...] * lk1[...], axis=-1, keepdims=True)
    s2 = jnp.sum(lq2[...] * lk2[...], axis=-1, keepdims=True)
    return jnp.exp(s1) - jnp.exp(s2) + lam_init


def _diff_in_proj_kernel(x_ref, g_ref, wq_ref, wkt_ref, wv_ref, q_ref, kt32_ref, kt16_ref, v32_ref, v16_ref, *, tn, kb):
    xn = _rms(x_ref[0], g_ref[...]).astype(BF16)
    d = wq_ref.shape[1]
    tm = xn.shape[0]
    for c0 in range(0, d, tn):
        q_ref[0, :, c0:c0 + tn] = (_dot(xn, wq_ref[:, c0:c0 + tn]) * (DIFF_DH ** -0.5)).astype(BF16)
        v = _dot(xn, wv_ref[:, c0:c0 + tn])
        v32_ref[0, :, c0:c0 + tn] = v
        v16_ref[0, :, c0:c0 + tn] = v.astype(BF16)
        kt = _dot_nt(wkt_ref[c0:c0 + tn, :], xn)
        kt32_ref[0, c0:c0 + tn, :] = kt
        for s in range(tm // kb):
            kt16_ref[0, s, c0:c0 + tn, :] = kt[:, s * kb:(s + 1) * kb].astype(BF16)


def diff_in_proj(x, g, wq, wkt, wv, tm, kb):
    b, l, d = x.shape
    assert l % tm == 0 and tm % kb == 0
    tok = lambda: pl.BlockSpec((1, tm, d), lambda bi, i: (bi, i, 0))
    return pl.pallas_call(
        functools.partial(_diff_in_proj_kernel, tn=512, kb=kb),
        grid=(b, l // tm),
        in_specs=[tok(), _resident((1, d)), _resident((d, d)), _resident((d, d)), _resident((d, d))],
        out_specs=[tok(), pl.BlockSpec((1, d, tm), lambda bi, i: (bi, 0, i)),
                   pl.BlockSpec((1, tm // kb, d, kb), lambda bi, i: (bi, i, 0, 0)), tok(), tok()],
        out_shape=[jax.ShapeDtypeStruct((b, l, d), BF16), jax.ShapeDtypeStruct((b, d, l), F32),
                   jax.ShapeDtypeStruct((b, l // kb, d, kb), BF16), jax.ShapeDtypeStruct((b, l, d), F32),
                   jax.ShapeDtypeStruct((b, l, d), BF16)],
        compiler_params=_cparams(("parallel", "parallel")),
        name="diff_in_proj",
    )(x, g.reshape(1, d), wq, wkt, wv)


def _attn_prompt_kernel(tbl_ref, q_ref, kt_ref, v_ref, lq1, lk1, lq2, lk2, sg_ref, o_ref,
                        bias_sc, m1, a1, m2, a2, *, T, lam_init):
    hd = pl.program_id(1)
    qi = pl.program_id(2)

    @pl.when(qi == 0)
    def _():
        i = lax.broadcasted_iota(jnp.int32, (T, 2 * T), 0)
        j = lax.broadcasted_iota(jnp.int32, (T, 2 * T), 1)
        dist = i - j + T
        bias_sc[...] = jnp.where(dist >= 0, _t5_bias(dist, lambda bkt: tbl_ref[bkt, hd]), -jnp.inf)

    lane = lax.broadcasted_iota(jnp.int32, (1, DIFF_DV), 1)
    q = q_ref[0]
    zero = jnp.zeros((), q.dtype)
    qa = jnp.where(lane < DIFF_DH, q, zero)
    qb = jnp.where(lane >= DIFF_DH, q, zero)
    for m_ref, a_ref in ((m1, a1), (m2, a2)):
        m_ref[...] = jnp.full(m_ref.shape, -jnp.inf, F32)
        a_ref[...] = jnp.zeros(a_ref.shape, F32)

    def update(kb0, nblk, bias):
        w = nblk * T
        kt = jnp.concatenate([kt_ref[0, kb0 + n] for n in range(nblk)], axis=1) if nblk > 1 else kt_ref[0, kb0]
        vt = v_ref[0, pl.ds(pl.multiple_of(kb0 * T, T), w), :]
        vx = jnp.concatenate([vt, jnp.ones((w, DIFF_DV), BF16)], axis=1)
        for qq, m_ref, a_ref in ((qa, m1, a1), (qb, m2, a2)):
            s = _dot(qq, kt) + bias
            m_prev = m_ref[...]
            m_new = jnp.maximum(m_prev, jnp.max(s, axis=-1, keepdims=True))
            alpha = jnp.exp(m_prev - m_new)
            p = jnp.concatenate([jnp.exp(s[:, c:c + LANES] - m_new) for c in range(0, w, LANES)], axis=1)
            a_ref[...] = jnp.concatenate([alpha, alpha], axis=1) * a_ref[...] + _dot(p.astype(BF16), vx)
            m_ref[...] = m_new

    far_bias = tbl_ref[REL_BUCKETS - 1, hd]
    n_far = jnp.maximum(qi - 1, 0)

    def far_body(i2, carry):
        update(2 * i2, 2, far_bias)
        return carry

    lax.fori_loop(0, n_far // 2, far_body, 0)

    @pl.when(n_far % 2 == 1)
    def _():
        update(n_far - 1, 1, far_bias)

    @pl.when(qi >= 1)
    def _():
        update(qi - 1, 2, bias_sc[...])

    @pl.when(qi == 0)
    def _():
        update(0, 1, bias_sc[:, T:2 * T])

    lam = _lambda(lq1, lk1, lq2, lk2, lam_init)
    o = a1[:, :DIFF_DV] / a1[:, DIFF_DV:] - lam * (a2[:, :DIFF_DV] / a2[:, DIFF_DV:])
    o_ref[0] = (_rms(o, sg_ref[...]) * (1.0 - lam_init)).astype(o_ref.dtype)


def diff_attention_prompt(q, kt, v, rel_table, lams, subln_g, lam_init, T):
    b, l, _ = q.shape
    assert T >= REL_MAX_DIST and l % T == 0 and DIFF_DV == 2 * DIFF_DH
    vec = lambda a: a.astype(F32).reshape(1, -1)
    return pl.pallas_call(
        functools.partial(_attn_prompt_kernel, T=T, lam_init=lam_init),
        grid=(b, DIFF_HEADS, l // T),
        in_specs=[
            pl.BlockSpec(memory_space=pltpu.SMEM),
            pl.BlockSpec((1, T, DIFF_DV), lambda bi, h, qi: (bi, qi, h)),
            pl.BlockSpec((1, l // T, DIFF_DV, T), lambda bi, h, qi: (bi, 0, h, 0)),
            pl.BlockSpec((1, l, DIFF_DV), lambda bi, h, qi: (bi, 0, h)),
        ] + [_resident((1, DIFF_DH))] * 4 + [_resident((1, DIFF_DV))],
        out_specs=pl.BlockSpec((1, T, DIFF_DV), lambda bi, h, qi: (bi, qi, h)),
        out_shape=jax.ShapeDtypeStruct((b, l, DIFF_HEADS * DIFF_DV), BF16),
        scratch_shapes=[pltpu.VMEM((T, 2 * T), F32)] + [pltpu.VMEM((T, LANES), F32), pltpu.VMEM((T, 2 * DIFF_DV), F32)] * 2,
        compiler_params=_cparams(("parallel", "parallel", "arbitrary")),
        name="diff_attention_prompt",
    )(rel_table.astype(F32), q, kt, v, *[vec(a) for a in lams], vec(subln_g))


DEC_RPH = SUBLANES


def _attn_sample_kernel(pt_ref, q_ref, kn_ref, vn_ref, tb_ref, lq1, lk1, lq2, lk2, sg_ref, *rest,
                        pps, n_pages, lq, lam_init):
    kt_refs, v_refs = rest[:pps], rest[pps:2 * pps]
    o_ref, qm_sc, m_sc, l_sc, acc_sc = rest[2 * pps:]
    s = pl.program_id(1)
    n_steps = n_pages // pps
    past = n_pages * PAGE_SIZE
    rows = DIFF_HEADS * DEC_RPH
    width = DIFF_HEADS * DIFF_DV
    rid = lax.broadcasted_iota(jnp.int32, (rows, 1), 0)
    row_tok = (rid % DEC_RPH) // 2

    def q_rows():
        r8 = lax.broadcasted_iota(jnp.int32, (DEC_RPH, width), 0)
        lane_pair = lax.broadcasted_iota(jnp.int32, (DEC_RPH, width), 1) // DIFF_DH
        qrep = jnp.zeros((DEC_RPH, width), F32)
        for t in range(lq):
            qrep = jnp.where(r8 // 2 == t, q_ref[0, t:t + 1, :], qrep)
        qrep = qrep * (DIFF_DH ** -0.5)
        return jnp.concatenate([jnp.where(lane_pair == 2 * hh + r8 % 2, qrep, 0.0) for hh in range(DIFF_HEADS)], axis=0)

    @pl.when(s == 0)
    def _():
        qm_sc[...] = q_rows().astype(BF16)
        m_sc[...] = jnp.full(m_sc.shape, -jnp.inf, F32)
        l_sc[...] = jnp.zeros(l_sc.shape, F32)
        acc_sc[...] = jnp.zeros(acc_sc.shape, F32)

    far_bias = tb_ref[:, REL_BUCKETS - 1:REL_BUCKETS]

    def page_update(i, near):
        sc = _dot(qm_sc[...], kt_refs[i][0].astype(BF16))
        if near:
            kpos = (s * pps + i) * PAGE_SIZE + lax.broadcasted_iota(jnp.int32, (1, PAGE_SIZE), 1)
            sc = sc + _t5_bias((past + row_tok) - kpos, lambda bkt: tb_ref[:, bkt:bkt + 1])
        else:
            sc = sc + far_bias
        m_prev = m_sc[...]
        m_new = jnp.maximum(m_prev, jnp.max(sc, axis=-1, keepdims=True))
        alpha = jnp.exp(m_prev - m_new)
        p = jnp.exp(sc - m_new)
        l_sc[...] = alpha * l_sc[...] + jnp.sum(p, axis=-1, keepdims=True)
        m_sc[...] = m_new
        for hh in range(DIFF_HEADS):
            sl = slice(hh * DEC_RPH, (hh + 1) * DEC_RPH)
            vh = v_refs[i][0, pl.ds(hh, PAGE_SIZE, stride=DIFF_HEADS), :].astype(BF16)
            acc_sc[sl, :] = alpha[sl] * acc_sc[sl, :] + _dot(p[sl].astype(BF16), vh)

    @pl.when(s < n_steps)
    def _():
        for i in range(pps - 1):
            page_update(i, False)

    @pl.when(s < n_steps - 1)
    def _():
        page_update(pps - 1, False)

    @pl.when(s == n_steps - 1)
    def _():
        page_update(pps - 1, True)

    @pl.when(s == n_steps)
    def _():
        qf = q_rows()
        sj = []
        for jn in range(lq):
            dist = row_tok - jn
            sc = jnp.sum(qf * kn_ref[0, jn:jn + 1, :], axis=-1, keepdims=True)
            sc = sc + _t5_bias(dist, lambda bkt: tb_ref[:, bkt:bkt + 1])
            sj.append(jnp.where(dist >= 0, sc, -jnp.inf))
        m_prev = m_sc[...]
        m_new = m_prev
        for sc in sj:
            m_new = jnp.maximum(m_new, sc)
        alpha = jnp.exp(m_prev - m_new)
        l_new = alpha * l_sc[...]
        acc = alpha * acc_sc[...]
        for jn, sc in enumerate(sj):
            p = jnp.exp(sc - m_new)
            l_new = l_new + p
            vrow = jnp.concatenate(
                [jnp.broadcast_to(vn_ref[0, jn:jn + 1, hh * DIFF_DV:(hh + 1) * DIFF_DV], (DEC_RPH, DIFF_DV))
                 for hh in range(DIFF_HEADS)], axis=0)
            acc = acc + p * vrow

        lam = _lambda(lq1, lk1, lq2, lk2, lam_init)
        coef = jnp.where(rid % 2 == 0, 1.0, -lam) / l_new
        a = acc * coef
        a = a + pltpu.roll(a, rows - 1, 0)
        res = _rms(a, sg_ref[...]) * (1.0 - lam_init)
        for hh in range(DIFF_HEADS):
            for t in range(lq):
                r = hh * DEC_RPH + 2 * t
                o_ref[0, t:t + 1, hh * DIFF_DV:(hh + 1) * DIFF_DV] = res[r:r + 1, :]


def diff_attention_sample(q, k_new, v_new, cache_kt, cache_v, page_table, rel_table, lams, subln_g, lam_init, pps=4):
    b, lq, width = q.shape
    n_pages = page_table.shape[1]
    assert n_pages % pps == 0 and 2 * lq <= DEC_RPH and n_pages // pps >= 1
    n_steps = n_pages // pps
    rows = DIFF_HEADS * DEC_RPH
    head_of_row = np.arange(rows) // DEC_RPH
    tb = jnp.pad(rel_table.astype(F32).T[head_of_row], ((0, 0), (0, LANES - REL_BUCKETS)))
    vec = lambda a: a.astype(F32).reshape(1, -1)

    def page_map(i):
        return lambda bi, s, pt: (pt[bi * n_pages + jnp.minimum(s, n_steps - 1) * pps + i], 0, 0)

    tok = pl.BlockSpec((1, lq, width), lambda bi, s, pt: (bi, 0, 0))
    const = lambda shape: pl.BlockSpec(shape, lambda bi, s, pt: (0,) * len(shape))
    kpage = [pl.BlockSpec((1, width, PAGE_SIZE), page_map(i)) for i in range(pps)]
    vpage = [pl.BlockSpec((1, PAGE_SIZE * DIFF_HEADS, DIFF_DV), page_map(i)) for i in range(pps)]
    grid_spec = pltpu.PrefetchScalarGridSpec(
        num_scalar_prefetch=1,
        grid=(b, n_steps + 1),
        in_specs=[tok, tok, tok, const((rows, LANES))] + [const((1, DIFF_DH))] * 4 + [const((1, DIFF_DV))] + kpage + vpage,
        out_specs=tok,
        scratch_shapes=[pltpu.VMEM((rows, width), BF16), pltpu.VMEM((rows, 1), F32), pltpu.VMEM((rows, 1), F32),
                        pltpu.VMEM((rows, DIFF_DV), F32)],
    )
    return pl.pallas_call(
        functools.partial(_attn_sample_kernel, pps=pps, n_pages=n_pages, lq=lq, lam_init=lam_init),
        grid_spec=grid_spec,
        out_shape=jax.ShapeDtypeStruct((b, lq, width), F32),
        compiler_params=_cparams(("parallel", "arbitrary")),
        name="diff_attention_sample",
    )(page_table.reshape(-1).astype(jnp.int32), q, k_new, v_new, tb, *[vec(a) for a in lams], vec(subln_g),
      *([cache_kt] * pps), *([cache_v] * pps))


def kernel(x_prompt, x_sample, state_ret, state_ssm, state_conv, cache_k_diff, cache_v_diff, page_table, norm_mix_g, norm_ffn_g, norm_final_g, ret_w_in, ret_w_out, ssd_w_in, ssd_conv_w, ssd_conv_b, ssd_dt_bias, ssd_A_log, ssd_D, ssd_norm_g, ssd_w_out, diff_w_in, diff_lam_q1, diff_lam_k1, diff_lam_q2, diff_lam_k2, diff_subln_g, diff_w_out, rel_bias_table, ffn_w_in, ffn_w_out):
    kinds = tuple(i % 3 for i in range(DEPTH))
    n_pages = page_table.shape[1]
    bf = lambda a: a.astype(BF16)
    ret_wi, ret_wo = bf(ret_w_in), bf(ret_w_out)
    ssd_wmain = bf(ssd_w_in[:, :, :SSD_INNER + SSD_CONV_CH])
    ssd_wdt = bf(jnp.pad(ssd_w_in[:, :, SSD_INNER + SSD_CONV_CH:], ((0, 0), (0, 0), (0, LANES - SSD_HEADS))))
    ssd_wo = bf(ssd_w_out)
    diff_wi, diff_wo = bf(diff_w_in), bf(diff_w_out)
    ffn_wi, ffn_wo = bf(ffn_w_in), bf(ffn_w_out)

    def run_group(x, sample):
        b, l, d = x.shape
        t = b * l
        tm = 512 if t % 512 == 0 else t
        c_real = CHUNK if l % CHUNK == 0 else l
        act = F32 if sample else BF16
        offset = n_pages * PAGE_SIZE if sample else 0
        rope_rows = l if l % CHUNK == 0 else CHUNK
        cos, sin = rope_tables(offset, rope_rows)
        ret_new, ssm_new, conv_new, k_new, v_new = [], [], [], [], []
        h = x.reshape(t, d)
        for i in range(DEPTH):
            kind = kinds[i]
            j = kinds[:i].count(kind)
            g = norm_mix_g[i]
            if kind == 0:
                n_in = ret_wi.shape[2]
                (qkvg,) = norm_linear(h, g, ret_wi[j], [(0, n_in, act)], tm)
                s0 = state_ret[j] if sample else jnp.zeros((b, RET_HEADS, RET_DK, RET_DV), F32)
                y, s1 = retention_core(qkvg.reshape(b, l, n_in), cos, sin, s0, c_real, act)
                ret_new.append(s1)
                h = linear_residual(y.reshape(t, RET_VDIM), ret_wo[j], h, tm)
            elif kind == 1:
                n_in = SSD_INNER + SSD_CONV_CH
                (zxbc,) = norm_linear(h, g, ssd_wmain[j], [(0, n_in, act)], tm)
                (dt_raw,) = norm_linear(h, g, ssd_wdt[j], [(0, LANES, F32)], tm)
                zxbc = zxbc.reshape(b, l, n_in)
                if sample:
                    buf, s0 = state_conv[j], state_ssm[j]
                    conv_new.append(zxbc[:, l - (SSD_CONV - 1):, SSD_INNER:])
                else:
                    buf = jnp.zeros((b, SSD_CONV - 1, SSD_CONV_CH), F32)
                    s0 = jnp.zeros((b, SSD_HEADS, SSD_HEADDIM, SSD_STATE), F32)
                    tail = h.reshape(b, l, d)[:, l - (SSD_CONV - 1):].reshape(b * (SSD_CONV - 1), d)
                    rows = -(-tail.shape[0] // 16) * 16
                    tail = jnp.pad(tail, ((0, rows - tail.shape[0]), (0, 0)))
                    (xbc_tail,) = norm_linear(tail, g, ssd_wmain[j][:, SSD_INNER:], [(0, SSD_CONV_CH, F32)], rows)
                    conv_new.append(xbc_tail[:b * (SSD_CONV - 1)].reshape(b, SSD_CONV - 1, SSD_CONV_CH))
                y, s1 = ssd_core(zxbc, dt_raw.reshape(b, l, LANES), buf, s0, ssd_conv_w[j], ssd_conv_b[j],
                                 ssd_dt_bias[j], ssd_A_log[j], ssd_D[j], ssd_norm_g[j], c_real, act)
                ssm_new.append(s1)
                h = linear_residual(y.reshape(t, SSD_INNER), ssd_wo[j], h, tm)
            else:
                lam_init = 0.8 - 0.6 * math.exp(-0.3 * i)
                lams = (diff_lam_q1[j], diff_lam_k1[j], diff_lam_q2[j], diff_lam_k2[j])
                if sample:
                    q, kn, vn = norm_linear(h, g, diff_wi[j], [(0, d, F32), (d, 2 * d, F32), (2 * d, 3 * d, F32)], tm)
                    cache_kt = jnp.transpose(cache_k_diff[j], (0, 2, 3, 4, 1)).reshape(-1, d, PAGE_SIZE)
                    cache_v = cache_v_diff[j].reshape(-1, PAGE_SIZE * DIFF_HEADS, DIFF_DV)
                    y = diff_attention_sample(q.reshape(b, l, d), kn.reshape(b, l, d), vn.reshape(b, l, d), cache_kt,
                                              cache_v, page_table, rel_bias_table, lams, diff_subln_g[j], lam_init)
                    k_new.append(kn.reshape(b, l, DIFF_HEADS, 2, DIFF_DH))
                    v_new.append(vn.reshape(b, l, DIFF_HEADS, DIFF_DV))
                else:
                    w = diff_wi[j]
                    q, kt32, kt16, v32, v16 = diff_in_proj(h.reshape(b, l, d), g, w[:, :d], w[:, d:2 * d].T, w[:, 2 * d:],
                                                           tm, ATTN_TILE)
                    y = diff_attention_prompt(q, kt16, v16, rel_bias_table, lams, diff_subln_g[j], lam_init, ATTN_TILE)
                    k_new.append(jnp.transpose(kt32.reshape(b, DIFF_HEADS, 2, DIFF_DH, l), (0, 4, 1, 2, 3)))
                    v_new.append(v32.reshape(b, l, DIFF_HEADS, DIFF_DV))
                h = linear_residual(y.reshape(t, d), diff_wo[j], h, tm)
            h = ffn(h, norm_ffn_g[i], ffn_wi[i], ffn_wo[i], min(tm, 256))
        y = final_norm(h, norm_final_g, tm).reshape(b, l, d)
        return y, jnp.stack(ret_new), jnp.stack(ssm_new), jnp.stack(conv_new), jnp.stack(k_new), jnp.stack(v_new)

    y_p, ret_p, ssm_p, conv_p, k_p, v_p = run_group(x_prompt, False)
    y_s, ret_s, ssm_s, conv_s, k_s, v_s = run_group(x_sample, True)
    return (y_p, y_s, ret_p, ssm_p, conv_p, k_p, v_p, ret_s, ssm_s, conv_s, k_s, v_s)
```

```python
import functools
import math

import jax
import jax.numpy as jnp
import numpy as np
from jax import lax
from jax.experimental import pallas as pl
from jax.experimental.pallas import tpu as pltpu

F32 = jnp.float32
BF16 = jnp.bfloat16

D_MODEL = 1024
DEPTH = 4
PAGE_SIZE = 128
EPS = 1e-6
RET_CHUNK = 256
SSD_CHUNK = 128
ATTN_TILE = 256

RET_HEADS = 4
RET_DK = D_MODEL // RET_HEADS
RET_DV = 2 * RET_DK
RET_VDIM = RET_HEADS * RET_DV
ROPE_BASE = 10000.0
ROPE_HALF = RET_DK // 2

SSD_INNER = 2 * D_MODEL
SSD_HEADDIM = 64
SSD_HEADS = SSD_INNER // SSD_HEADDIM
SSD_GROUPS = 8
SSD_HPG = SSD_HEADS // SSD_GROUPS
SSD_STATE = 128
SSD_CONV = 4
SSD_BC = SSD_GROUPS * SSD_STATE
SSD_CONV_CH = SSD_INNER + 2 * SSD_BC
SSD_GW = SSD_HPG * SSD_HEADDIM

DIFF_HEADS = 8
DIFF_DH = D_MODEL // DIFF_HEADS // 2
DIFF_DV = 2 * DIFF_DH
REL_BUCKETS = 32
REL_MAX_DIST = 128

FFN_HIDDEN = -(-8 * D_MODEL // (3 * 256)) * 256

V7X_VMEM_BYTES = 64 * 1024 * 1024
LANES = 128
SUBLANES = 8
VMEM_LIMIT = 56 * 1024 * 1024


def _cparams(sem):
    return pltpu.CompilerParams(dimension_semantics=sem, vmem_limit_bytes=VMEM_LIMIT)


def _dot(a, b):
    return jnp.dot(a, b, preferred_element_type=F32)


def _dot_nt(a, b):
    return lax.dot_general(a, b, (((1,), (1,)), ((), ())), preferred_element_type=F32)


def _dot_tn(a, b):
    return lax.dot_general(a, b, (((0,), (0,)), ((), ())), preferred_element_type=F32)


def _rms(x, g):
    return x * lax.rsqrt(jnp.mean(x * x, axis=-1, keepdims=True) + EPS) * g


def _silu(x):
    return x * jax.nn.sigmoid(x)


def _resident(shape):
    return pl.BlockSpec(shape, lambda *_: (0,) * len(shape), pipeline_mode=pl.Buffered(1))


def _norm_linear_kernel(x_ref, g_ref, w_ref, *o_refs, outs, tn):
    xn = _rms(x_ref[...], g_ref[...]).astype(BF16)
    n = w_ref.shape[1]
    for c0 in range(0, n, tn):
        acc = _dot(xn, w_ref[:, c0:c0 + tn])
        for o_ref, (lo, hi, _) in zip(o_refs, outs):
            if lo <= c0 and c0 + tn <= hi:
                o_ref[:, c0 - lo:c0 - lo + tn] = acc.astype(o_ref.dtype)


def norm_linear(x, g, w, outs, tm, tn=512):
    t, d = x.shape
    n = w.shape[1]
    tn = min(tn, n)
    assert t % tm == 0 and n % tn == 0 and all(lo % tn == 0 and hi % tn == 0 for lo, hi, _ in outs)
    return pl.pallas_call(
        functools.partial(_norm_linear_kernel, outs=tuple(outs), tn=tn),
        grid=(t // tm,),
        in_specs=[pl.BlockSpec((tm, d), lambda i: (i, 0)), _resident((1, d)), _resident((d, n))],
        out_specs=[pl.BlockSpec((tm, hi - lo), lambda i: (i, 0)) for lo, hi, _ in outs],
        out_shape=[jax.ShapeDtypeStruct((t, hi - lo), dt) for lo, hi, dt in outs],
        compiler_params=_cparams(("parallel",)),
        name="norm_linear",
    )(x, g.reshape(1, d), w)


def _linear_residual_kernel(y_ref, w_ref, h_ref, o_ref):
    o_ref[...] = h_ref[...] + _dot(y_ref[...].astype(BF16), w_ref[...])


def linear_residual(y, w, h, tm):
    t, k = y.shape
    d = w.shape[1]
    return pl.pallas_call(
        _linear_residual_kernel,
        grid=(t // tm,),
        in_specs=[pl.BlockSpec((tm, k), lambda i: (i, 0)), _resident((k, d)), pl.BlockSpec((tm, d), lambda i: (i, 0))],
        out_specs=pl.BlockSpec((tm, d), lambda i: (i, 0)),
        out_shape=jax.ShapeDtypeStruct((t, d), F32),
        compiler_params=_cparams(("parallel",)),
        name="linear_residual",
    )(y, w, h)


def _ffn_kernel(h_ref, g_ref, wgu_ref, wo_ref, o_ref, *, th):
    h = h_ref[...]
    xn = _rms(h, g_ref[...]).astype(BF16)
    hidden = wo_ref.shape[0]
    o_ref[...] = h
    for c0 in range(0, hidden, th):
        gate = _dot(xn, wgu_ref[:, c0:c0 + th])
        up = _dot(xn, wgu_ref[:, hidden + c0:hidden + c0 + th])
        act = (_silu(gate) * up).astype(BF16)
        o_ref[...] += _dot(act, wo_ref[c0:c0 + th, :])


def ffn(h, g, wgu, wo, tm, th=256):
    t, d = h.shape
    hidden = wo.shape[0]
    assert hidden % th == 0 and t % tm == 0
    return pl.pallas_call(
        functools.partial(_ffn_kernel, th=th),
        grid=(t // tm,),
        in_specs=[pl.BlockSpec((tm, d), lambda i: (i, 0)), _resident((1, d)), _resident((d, 2 * hidden)),
                  _resident((hidden, d))],
        out_specs=pl.BlockSpec((tm, d), lambda i: (i, 0)),
        out_shape=jax.ShapeDtypeStruct((t, d), F32),
        compiler_params=_cparams(("parallel",)),
        name="ffn",
    )(h, g.reshape(1, d), wgu, wo)


def _final_norm_kernel(h_ref, g_ref, o_ref):
    o_ref[...] = _rms(h_ref[...], g_ref[...])


def final_norm(h, g, tm):
    t, d = h.shape
    return pl.pallas_call(
        _final_norm_kernel,
        grid=(t // tm,),
        in_specs=[pl.BlockSpec((tm, d), lambda i: (i, 0)), _resident((1, d))],
        out_specs=pl.BlockSpec((tm, d), lambda i: (i, 0)),
        out_shape=jax.ShapeDtypeStruct((t, d), F32),
        compiler_params=_cparams(("parallel",)),
        name="final_norm",
    )(h, g.reshape(1, d))


def _rope_kernel(inv_ref, cos_ref, sin_ref, *, offset):
    rows = cos_ref.shape[0]
    pos = (offset + pl.program_id(0) * rows + lax.broadcasted_iota(jnp.int32, (rows, 1), 0)).astype(F32)
    ang = pos * inv_ref[...]
    cos_ref[...] = jnp.cos(ang)
    sin_ref[...] = jnp.sin(ang)


def rope_tables(offset, rows):
    inv = (1.0 / (ROPE_BASE ** jnp.linspace(0.0, 1.0, ROPE_HALF, dtype=F32))).reshape(1, ROPE_HALF)
    tr = min(rows, 512)
    return pl.pallas_call(
        functools.partial(_rope_kernel, offset=offset),
        grid=(rows // tr,),
        in_specs=[_resident((1, ROPE_HALF))],
        out_specs=[pl.BlockSpec((tr, ROPE_HALF), lambda i: (i, 0))] * 2,
        out_shape=[jax.ShapeDtypeStruct((rows, ROPE_HALF), F32)] * 2,
        compiler_params=_cparams(("parallel",)),
        name="rope_tables",
    )(inv)


def _ret_kernel(lg_ref, q_ref, k_ref, v_ref, g_ref, cos_ref, sin_ref, s0_ref, y_ref, s1_ref, *stage, C, c_real):
    hd = pl.program_id(1)
    c = pl.program_id(2)
    lg = lg_ref[hd]

    @pl.when(c == 0)
    def _():
        s1_ref[...] = s0_ref[...]

    def load(ref, st):
        if c_real == C:
            return ref[0].astype(F32)
        st[...] = jnp.zeros(st.shape, F32)
        st[0:c_real, :] = ref[0].astype(F32)
        return st[...]

    st = stage if stage else (None,) * 3
    q = load(q_ref, st[0])
    k = load(k_ref, st[1])
    v = load(v_ref, st[2]).astype(BF16)
    cos = cos_ref[...]
    sin = sin_ref[...]

    def rot(x):
        x1, x2 = x[:, :ROPE_HALF], x[:, ROPE_HALF:]
        return jnp.concatenate([x1 * cos - x2 * sin, x2 * cos + x1 * sin], axis=-1)

    qr = rot(q).astype(BF16)
    kr = rot(k) * (RET_DK ** -0.5)
    i = lax.broadcasted_iota(jnp.int32, (C, 1), 0).astype(F32)
    j = lax.broadcasted_iota(jnp.int32, (1, C), 1).astype(F32)
    rel = i - j
    decay = jnp.where(rel >= 0, jnp.exp(jnp.maximum(rel, 0.0) * lg), 0.0)
    scores = _dot_nt(qr, kr.astype(BF16)) * decay
    s_prev = s1_ref[0, 0]
    o = _dot(scores.astype(BF16), v) + _dot(qr, s_prev.astype(BF16)) * jnp.exp((i + 1.0) * lg)
    kw = (kr * jnp.exp((c_real - 1.0 - i) * lg)).astype(BF16)
    chunk_decay = jnp.exp(jnp.zeros((1, 1), F32) + c_real * lg)
    s1_ref[0, 0] = chunk_decay * s_prev + _dot_tn(kw, v)
    o = o * lax.rsqrt(jnp.mean(o * o, axis=-1, keepdims=True) + EPS)
    o = o[0:c_real]
    y_ref[0] = (_silu(g_ref[0].astype(F32)) * o).astype(y_ref.dtype)


def retention_core(qkvg, cos, sin, s0, c_real, out_dtype):
    b, l, _ = qkvg.shape
    C = RET_CHUNK
    nc = l // c_real
    kb, vb = RET_DK, RET_DV
    k_off = RET_HEADS
    v_off = 2 * RET_HEADS * RET_DK // RET_DV
    g_off = v_off + RET_HEADS
    lg = jnp.asarray(np.log(1.0 - 2.0 ** (-5.0 - np.arange(RET_HEADS))), F32)
    stage = [] if c_real == C else [pltpu.VMEM((C, kb), F32), pltpu.VMEM((C, kb), F32), pltpu.VMEM((C, vb), F32)]
    return pl.pallas_call(
        functools.partial(_ret_kernel, C=C, c_real=c_real),
        grid=(b, RET_HEADS, nc),
        in_specs=[
            pl.BlockSpec(memory_space=pltpu.SMEM),
            pl.BlockSpec((1, c_real, kb), lambda bi, h, c: (bi, c, h)),
            pl.BlockSpec((1, c_real, kb), lambda bi, h, c: (bi, c, k_off + h)),
            pl.BlockSpec((1, c_real, vb), lambda bi, h, c: (bi, c, v_off + h)),
            pl.BlockSpec((1, c_real, vb), lambda bi, h, c: (bi, c, g_off + h)),
            pl.BlockSpec((C, ROPE_HALF), lambda bi, h, c: (c, 0)),
            pl.BlockSpec((C, ROPE_HALF), lambda bi, h, c: (c, 0)),
            pl.BlockSpec((1, 1, RET_DK, RET_DV), lambda bi, h, c: (bi, h, 0, 0)),
        ],
        out_specs=[
            pl.BlockSpec((1, c_real, vb), lambda bi, h, c: (bi, c, h)),
            pl.BlockSpec((1, 1, RET_DK, RET_DV), lambda bi, h, c: (bi, h, 0, 0)),
        ],
        out_shape=[jax.ShapeDtypeStruct((b, l, RET_VDIM), out_dtype),
                   jax.ShapeDtypeStruct((b, RET_HEADS, RET_DK, RET_DV), F32)],
        scratch_shapes=stage,
        compiler_params=_cparams(("parallel", "parallel", "arbitrary")),
        name="retention_core",
    )(lg, qkvg, qkvg, qkvg, qkvg, cos, sin, s0)


def _softplus(x):
    return jnp.maximum(x, 0.0) + jnp.log1p(jnp.exp(-jnp.abs(x)))


def _cumsum(x, axis):
    n = x.shape[axis]
    idx = lax.broadcasted_iota(jnp.int32, x.shape, axis)
    s = 1
    while s < n:
        x = x + jnp.where(idx >= s, pltpu.roll(x, s, axis), 0.0)
        s *= 2
    return x


def _ssd_kernel(z_ref, x_ref, bc_ref, dt_ref, dtt_ref, cw_ref, cb_ref, dtb_ref, dtbt_ref, al_ref, alt_ref,
                dsk_ref, ng_ref, buf_ref, s0_ref, y_ref, s1_ref, extx, extbc, xs_ref, bcs_ref, *, C, c_real):
    c = pl.program_id(1)
    halo = SUBLANES

    @pl.when(c == 0)
    def _():
        s1_ref[...] = s0_ref[...]
        extx[...] = jnp.zeros(extx.shape, F32)
        extbc[...] = jnp.zeros(extbc.shape, F32)
        extx[0:halo, :] = buf_ref[0, :, 0:SSD_INNER]
        extbc[0:halo, :] = buf_ref[0, :, SSD_INNER:SSD_CONV_CH]

    extx[halo:halo + c_real, :] = x_ref[0].astype(F32)
    extbc[halo:halo + c_real, :] = bc_ref[0].astype(F32)

    cw = 512
    first = halo - (SSD_CONV - 1)
    for ext, dst, ch0 in ((extx, xs_ref, 0), (extbc, bcs_ref, SSD_INNER)):
        for col in range(0, SSD_INNER, cw):
            acc = cb_ref[:, ch0 + col:ch0 + col + cw]
            for tap in range(SSD_CONV):
                acc = acc + ext[first + tap:first + tap + C, col:col + cw] * cw_ref[tap:tap + 1, ch0 + col:ch0 + col + cw]
            dst[:, col:col + cw] = _silu(acc).astype(dst.dtype)
    extx[0:halo, :] = extx[C:C + halo, :]
    extbc[0:halo, :] = extbc[C:C + halo, :]

    row = lax.broadcasted_iota(jnp.int32, (C, 1), 0)
    col_i = lax.broadcasted_iota(jnp.int32, (1, C), 1)
    dt = jnp.where(row < c_real, _softplus(dt_ref[0, 0] + dtb_ref[...]), 0.0)
    dtt = jnp.where(col_i < c_real, _softplus(dtt_ref[0, 0] + dtbt_ref[...]), 0.0)
    cs = _cumsum(dt * (-jnp.exp(al_ref[...])), 0)
    cst = _cumsum(dtt * (-jnp.exp(alt_ref[...])), 1)
    causal = row >= col_i
    lane_r = lax.broadcasted_iota(jnp.int32, (1, SSD_GW), 1) // SSD_HEADDIM
    sub_r = lax.broadcasted_iota(jnp.int32, (SSD_GW, 1), 0) // SSD_HEADDIM

    def by_head(vals, sel):
        out = vals[SSD_HPG - 1]
        for r in range(SSD_HPG - 2, -1, -1):
            out = jnp.where(sel == r, vals[r], out)
        return out

    for g in range(SSD_GROUPS):
        bm = bcs_ref[:, g * SSD_STATE:(g + 1) * SSD_STATE]
        cm = bcs_ref[:, SSD_BC + g * SSD_STATE:SSD_BC + (g + 1) * SSD_STATE]
        cb = _dot_nt(cm, bm)
        h0 = s1_ref[0, g * SSD_HPG:(g + 1) * SSD_HPG].reshape(SSD_GW, SSD_STATE)
        xg = xs_ref[:, g * SSD_GW:(g + 1) * SSD_GW]
        heads = range(g * SSD_HPG, (g + 1) * SSD_HPG)
        cs_cols = [cs[:, hh:hh + 1] for hh in heads]
        cs_last = [cs[C - 1:C, hh:hh + 1] for hh in heads]
        yg = _dot_nt(cm, h0.astype(BF16)) * by_head([jnp.exp(cc) for cc in cs_cols], lane_r)
        for r, hh in enumerate(heads):
            seg = cs_cols[r] - cst[hh:hh + 1, :]
            w = cb * jnp.exp(jnp.where(causal, seg, -jnp.inf)) * dtt[hh:hh + 1, :]
            yg = yg + _dot(w.astype(BF16), jnp.where(lane_r == r, xg, 0.0).astype(BF16))
        w_end = by_head([jnp.exp(cl - cc) * dt[:, hh:hh + 1] for cl, cc, hh in zip(cs_last, cs_cols, heads)], lane_r)
        h1 = by_head([jnp.exp(cl) for cl in cs_last], sub_r) * h0 + _dot_tn((xg * w_end).astype(BF16), bm)
        s1_ref[0, g * SSD_HPG:(g + 1) * SSD_HPG] = h1.reshape(SSD_HPG, SSD_HEADDIM, SSD_STATE)
        sl = slice(g * SSD_GW, (g + 1) * SSD_GW)
        yo = (yg[0:c_real] + dsk_ref[:, sl] * xg[0:c_real]) * _silu(z_ref[0, :, sl].astype(F32))
        yo = yo * lax.rsqrt(jnp.mean(yo * yo, axis=-1, keepdims=True) + EPS) * ng_ref[:, sl]
        y_ref[0, :, sl] = yo.astype(y_ref.dtype)


def ssd_core(zxbc, dt_raw, conv_buf, s0, conv_w, conv_b, dt_bias, a_log, d_skip, norm_g, c_real, out_dtype):
    b, l, _ = zxbc.shape
    C = SSD_CHUNK
    nc = l // c_real
    halo = SUBLANES
    dt4 = dt_raw.reshape(b, nc, c_real, LANES)
    dtt = jnp.swapaxes(dt4[..., :SSD_HEADS], -1, -2)
    if c_real < C:
        dt4 = jnp.pad(dt4, ((0, 0), (0, 0), (0, C - c_real), (0, 0)))
        dtt = jnp.pad(dtt, ((0, 0), (0, 0), (0, 0), (0, C - c_real)))
    lane_pad = LANES - SSD_HEADS
    buf8 = jnp.pad(conv_buf.astype(F32), ((0, 0), (halo - (SSD_CONV - 1), 0), (0, 0)))
    small = [
        conv_w.astype(F32), conv_b.reshape(1, -1).astype(F32),
        jnp.pad(dt_bias.astype(F32), (0, lane_pad)).reshape(1, LANES), dt_bias.astype(F32).reshape(SSD_HEADS, 1),
        jnp.pad(a_log.astype(F32), (0, lane_pad)).reshape(1, LANES), a_log.astype(F32).reshape(SSD_HEADS, 1),
        jnp.repeat(d_skip.astype(F32), SSD_HEADDIM).reshape(1, SSD_INNER), norm_g.astype(F32).reshape(1, SSD_INNER),
    ]
    w = SSD_INNER
    return pl.pallas_call(
        functools.partial(_ssd_kernel, C=C, c_real=c_real),
        grid=(b, nc),
        in_specs=[
            pl.BlockSpec((1, c_real, w), lambda bi, c: (bi, c, 0)),
            pl.BlockSpec((1, c_real, w), lambda bi, c: (bi, c, 1)),
            pl.BlockSpec((1, c_real, w), lambda bi, c: (bi, c, 2)),
            pl.BlockSpec((1, 1, C, LANES), lambda bi, c: (bi, c, 0, 0)),
            pl.BlockSpec((1, 1, SSD_HEADS, C), lambda bi, c: (bi, c, 0, 0)),
        ] + [_resident(a.shape) for a in small] + [
            pl.BlockSpec((1, halo, SSD_CONV_CH), lambda bi, c: (bi, 0, 0)),
            pl.BlockSpec((1, SSD_HEADS, SSD_HEADDIM, SSD_STATE), lambda bi, c: (bi, 0, 0, 0)),
        ],
        out_specs=[
            pl.BlockSpec((1, c_real, w), lambda bi, c: (bi, c, 0)),
            pl.BlockSpec((1, SSD_HEADS, SSD_HEADDIM, SSD_STATE), lambda bi, c: (bi, 0, 0, 0)),
        ],
        out_shape=[jax.ShapeDtypeStruct((b, l, SSD_INNER), out_dtype),
                   jax.ShapeDtypeStruct((b, SSD_HEADS, SSD_HEADDIM, SSD_STATE), F32)],
        scratch_shapes=[pltpu.VMEM((C + halo, w), F32), pltpu.VMEM((C + halo, w), F32),
                        pltpu.VMEM((C, w), F32), pltpu.VMEM((C, w), BF16)],
        compiler_params=_cparams(("parallel", "arbitrary")),
        name="ssd_core",
    )(zxbc, zxbc, zxbc, dt4, dtt, *small, buf8, s0)


def _t5_bias(dist, table_at):
    n = jnp.maximum(dist, 0)
    exact = REL_BUCKETS // 2
    nf = jnp.maximum(n, 1).astype(F32)
    large = exact + (jnp.log(nf / exact) / math.log(REL_MAX_DIST / exact) * (REL_BUCKETS - exact)).astype(jnp.int32)
    bucket = jnp.where(n < exact, n, jnp.minimum(large, REL_BUCKETS - 1))
    bias = jnp.zeros(dist.shape, F32)
    for bkt in range(REL_BUCKETS):
        bias = jnp.where(bucket == bkt, table_at(bkt), bias)
    return bias


def _lambda(lq1, lk1, lq2, lk2, lam_init):
    s1 = jnp.sum(lq1[...] * lk1[...], axis=-1, keepdims=True)
    s2 = jnp.sum(lq2[...] * lk2[...], axis=-1, keepdims=True)
    return jnp.exp(s1) - jnp.exp(s2) + lam_init


def _diff_in_proj_kernel(x_ref, g_ref, wq_ref, wkt_ref, wv_ref, q_ref, kt32_ref, kt16_ref, v32_ref, v16_ref, *, tn, kb):
    xn = _rms(x_ref[0], g_ref[...]).astype(BF16)
    d = wq_ref.shape[1]
    tm = xn.shape[0]
    for c0 in range(0, d, tn):
        q_ref[0, :, c0:c0 + tn] = (_dot(xn, wq_ref[:, c0:c0 + tn]) * (DIFF_DH ** -0.5)).astype(BF16)
        v = _dot(xn, wv_ref[:, c0:c0 + tn])
        v32_ref[0, :, c0:c0 + tn] = v
        v16_ref[0, :, c0:c0 + tn] = v.astype(BF16)
        kt = _dot_nt(wkt_ref[c0:c0 + tn, :], xn)
        kt32_ref[0, c0:c0 + tn, :] = kt
        for s in range(tm // kb):
            kt16_ref[0, s, c0:c0 + tn, :] = kt[:, s * kb:(s + 1) * kb].astype(BF16)


def diff_in_proj(x, g, wq, wkt, wv, tm, kb):
    b, l, d = x.shape
    assert l % tm == 0 and tm % kb == 0
    tok = lambda: pl.BlockSpec((1, tm, d), lambda bi, i: (bi, i, 0))
    return pl.pallas_call(
        functools.partial(_diff_in_proj_kernel, tn=512, kb=kb),
        grid=(b, l // tm),
        in_specs=[tok(), _resident((1, d)), _resident((d, d)), _resident((d, d)), _resident((d, d))],
        out_specs=[tok(), pl.BlockSpec((1, d, tm), lambda bi, i: (bi, 0, i)),
                   pl.BlockSpec((1, tm // kb, d, kb), lambda bi, i: (bi, i, 0, 0)), tok(), tok()],
        out_shape=[jax.ShapeDtypeStruct((b, l, d), BF16), jax.ShapeDtypeStruct((b, d, l), F32),
                   jax.ShapeDtypeStruct((b, l // kb, d, kb), BF16), jax.ShapeDtypeStruct((b, l, d), F32),
                   jax.ShapeDtypeStruct((b, l, d), BF16)],
        compiler_params=_cparams(("parallel", "parallel")),
        name="diff_in_proj",
    )(x, g.reshape(1, d), wq, wkt, wv)


def _attn_prompt_kernel(tbl_ref, q_ref, kt_ref, v_ref, lq1, lk1, lq2, lk2, sg_ref, o_ref,
                        bias_sc, m_sc, a_sc, *, T, lam_init):
    hd = pl.program_id(1)
    qi = pl.program_id(2)

    @pl.when(qi == 0)
    def _():
        i = lax.broadcasted_iota(jnp.int32, (T, 2 * T), 0)
        j = lax.broadcasted_iota(jnp.int32, (T, 2 * T), 1)
        dist = i - j + T
        bias_sc[...] = jnp.where(dist >= 0, _t5_bias(dist, lambda bkt: tbl_ref[bkt, hd]), -jnp.inf)

    lane = lax.broadcasted_iota(jnp.int32, (1, DIFF_DV), 1)
    q = q_ref[0]
    zero = jnp.zeros((), q.dtype)
    q2 = jnp.concatenate([jnp.where(lane < DIFF_DH, q, zero), jnp.where(lane >= DIFF_DH, q, zero)], axis=0)
    m_sc[...] = jnp.full(m_sc.shape, -jnp.inf, F32)
    a_sc[...] = jnp.zeros(a_sc.shape, F32)

    def update(kb0, nblk, bias):
        w = nblk * T
        kt = jnp.concatenate([kt_ref[0, kb0 + n] for n in range(nblk)], axis=1) if nblk > 1 else kt_ref[0, kb0]
        vt = v_ref[0, pl.ds(pl.multiple_of(kb0 * T, T), w), :]
        vx = jnp.concatenate([vt, jnp.ones((w, DIFF_DV), BF16)], axis=1)
        s = _dot(q2, kt)
        s = s + (jnp.concatenate([bias, bias], axis=0) if getattr(bias, "ndim", 0) == 2 else bias)
        m_prev = m_sc[...]
        m_new = jnp.maximum(m_prev, jnp.max(s, axis=-1, keepdims=True))
        alpha = jnp.exp(m_prev - m_new)
        p = jnp.concatenate([jnp.exp(s[:, c:c + LANES] - m_new) for c in range(0, w, LANES)], axis=1)
        a_sc[...] = jnp.concatenate([alpha, alpha], axis=1) * a_sc[...] + _dot(p.astype(BF16), vx)
        m_sc[...] = m_new

    far_bias = tbl_ref[REL_BUCKETS - 1, hd]
    n_far = jnp.maximum(qi - 1, 0)

    def far_body(i2, carry):
        update(2 * i2, 2, far_bias)
        return carry

    lax.fori_loop(0, n_far // 2, far_body, 0)

    @pl.when(n_far % 2 == 1)
    def _():
        update(n_far - 1, 1, far_bias)

    @pl.when(qi >= 1)
    def _():
        update(qi - 1, 2, bias_sc[...])

    @pl.when(qi == 0)
    def _():
        update(0, 1, bias_sc[:, T:2 * T])

    lam = _lambda(lq1, lk1, lq2, lk2, lam_init)
    o = a_sc[:T, :DIFF_DV] / a_sc[:T, DIFF_DV:] - lam * (a_sc[T:, :DIFF_DV] / a_sc[T:, DIFF_DV:])
    o_ref[0] = (_rms(o, sg_ref[...]) * (1.0 - lam_init)).astype(o_ref.dtype)


def diff_attention_prompt(q, kt, v, rel_table, lams, subln_g, lam_init, T):
    b, l, _ = q.shape
    assert T >= REL_MAX_DIST and l % T == 0 and DIFF_DV == 2 * DIFF_DH
    vec = lambda a: a.astype(F32).reshape(1, -1)
    return pl.pallas_call(
        functools.partial(_attn_prompt_kernel, T=T, lam_init=lam_init),
        grid=(b, DIFF_HEADS, l // T),
        in_specs=[
            pl.BlockSpec(memory_space=pltpu.SMEM),
            pl.BlockSpec((1, T, DIFF_DV), lambda bi, h, qi: (bi, qi, h)),
            pl.BlockSpec((1, l // T, DIFF_DV, T), lambda bi, h, qi: (bi, 0, h, 0)),
            pl.BlockSpec((1, l, DIFF_DV), lambda bi, h, qi: (bi, 0, h)),
        ] + [_resident((1, DIFF_DH))] * 4 + [_resident((1, DIFF_DV))],
        out_specs=pl.BlockSpec((1, T, DIFF_DV), lambda bi, h, qi: (bi, qi, h)),
        out_shape=jax.ShapeDtypeStruct((b, l, DIFF_HEADS * DIFF_DV), BF16),
        scratch_shapes=[pltpu.VMEM((T, 2 * T), F32), pltpu.VMEM((2 * T, LANES), F32), pltpu.VMEM((2 * T, 2 * DIFF_DV), F32)],
        compiler_params=_cparams(("parallel", "parallel", "arbitrary")),
        name="diff_attention_prompt",
    )(rel_table.astype(F32), q, kt, v, *[vec(a) for a in lams], vec(subln_g))


DEC_RPH = SUBLANES


def _attn_sample_kernel(pt_ref, q_ref, kn_ref, vn_ref, tb_ref, lq1, lk1, lq2, lk2, sg_ref, *rest,
                        pps, n_pages, lq, lam_init):
    kt_refs, v_refs = rest[:pps], rest[pps:2 * pps]
    o_ref, qm_sc, m_sc, l_sc, acc_sc = rest[2 * pps:]
    s = pl.program_id(1)
    n_steps = n_pages // pps
    past = n_pages * PAGE_SIZE
    rows = DIFF_HEADS * DEC_RPH
    width = DIFF_HEADS * DIFF_DV
    rid = lax.broadcasted_iota(jnp.int32, (rows, 1), 0)
    row_tok = (rid % DEC_RPH) // 2

    def q_rows():
        r8 = lax.broadcasted_iota(jnp.int32, (DEC_RPH, width), 0)
        lane_pair = lax.broadcasted_iota(jnp.int32, (DEC_RPH, width), 1) // DIFF_DH
        qrep = jnp.zeros((DEC_RPH, width), F32)
        for t in range(lq):
            qrep = jnp.where(r8 // 2 == t, q_ref[0, t:t + 1, :], qrep)
        qrep = qrep * (DIFF_DH ** -0.5)
        return jnp.concatenate([jnp.where(lane_pair == 2 * hh + r8 % 2, qrep, 0.0) for hh in range(DIFF_HEADS)], axis=0)

    @pl.when(s == 0)
    def _():
        qm_sc[...] = q_rows().astype(BF16)
        m_sc[...] = jnp.full(m_sc.shape, -jnp.inf, F32)
        l_sc[...] = jnp.zeros(l_sc.shape, F32)
        acc_sc[...] = jnp.zeros(acc_sc.shape, F32)

    far_bias = tb_ref[:, REL_BUCKETS - 1:REL_BUCKETS]

    def pages_update(last_near):
        qm = qm_sc[...]
        sc = []
        for i in range(pps):
            sci = _dot(qm, kt_refs[i][0].astype(BF16))
            if last_near and i == pps - 1:
                kpos = (n_pages - 1) * PAGE_SIZE + lax.broadcasted_iota(jnp.int32, (1, PAGE_SIZE), 1)
                sc.append(sci + _t5_bias((past + row_tok) - kpos, lambda bkt: tb_ref[:, bkt:bkt + 1]))
            else:
                sc.append(sci + far_bias)
        m_prev = m_sc[...]
        m_new = m_prev
        for sci in sc:
            m_new = jnp.maximum(m_new, jnp.max(sci, axis=-1, keepdims=True))
        alpha = jnp.exp(m_prev - m_new)
        p = [jnp.exp(sci - m_new) for sci in sc]
        l_new = alpha * l_sc[...]
        for pi in p:
            l_new = l_new + jnp.sum(pi, axis=-1, keepdims=True)
        l_sc[...] = l_new
        m_sc[...] = m_new
        for hh in range(DIFF_HEADS):
            sl = slice(hh * DEC_RPH, (hh + 1) * DEC_RPH)
            ph = jnp.concatenate([pi[sl] for pi in p], axis=1).astype(BF16)
            vh = jnp.concatenate([v_refs[i][0, pl.ds(hh, PAGE_SIZE, stride=DIFF_HEADS), :].astype(BF16)
                                  for i in range(pps)], axis=0)
            acc_sc[sl, :] = alpha[sl] * acc_sc[sl, :] + _dot(ph, vh)

    @pl.when(s < n_steps - 1)
    def _():
        pages_update(False)

    @pl.when(s == n_steps - 1)
    def _():
        pages_update(True)

    @pl.when(s == n_steps)
    def _():
        qf = q_rows()
        sj = []
        for jn in range(lq):
            dist = row_tok - jn
            sc = jnp.sum(qf * kn_ref[0, jn:jn + 1, :], axis=-1, keepdims=True)
            sc = sc + _t5_bias(dist, lambda bkt: tb_ref[:, bkt:bkt + 1])
            sj.append(jnp.where(dist >= 0, sc, -jnp.inf))
        m_prev = m_sc[...]
        m_new = m_prev
        for sc in sj:
            m_new = jnp.maximum(m_new, sc)
        alpha = jnp.exp(m_prev - m_new)
        l_new = alpha * l_sc[...]
        acc = alpha * acc_sc[...]
        for jn, sc in enumerate(sj):
            p = jnp.exp(sc - m_new)
            l_new = l_new + p
            vrow = jnp.concatenate(
                [jnp.broadcast_to(vn_ref[0, jn:jn + 1, hh * DIFF_DV:(hh + 1) * DIFF_DV], (DEC_RPH, DIFF_DV))
                 for hh in range(DIFF_HEADS)], axis=0)
            acc = acc + p * vrow

        lam = _lambda(lq1, lk1, lq2, lk2, lam_init)
        coef = jnp.where(rid % 2 == 0, 1.0, -lam) / l_new
        a = acc * coef
        a = a + pltpu.roll(a, rows - 1, 0)
        res = _rms(a, sg_ref[...]) * (1.0 - lam_init)
        for hh in range(DIFF_HEADS):
            for t in range(lq):
                r = hh * DEC_RPH + 2 * t
                o_ref[0, t:t + 1, hh * DIFF_DV:(hh + 1) * DIFF_DV] = res[r:r + 1, :]


def diff_attention_sample(q, k_new, v_new, cache_kt, cache_v, page_table, rel_table, lams, subln_g, lam_init, pps=8):
    b, lq, width = q.shape
    n_pages = page_table.shape[1]
    assert n_pages % pps == 0 and 2 * lq <= DEC_RPH and n_pages // pps >= 1
    n_steps = n_pages // pps
    rows = DIFF_HEADS * DEC_RPH
    head_of_row = np.arange(rows) // DEC_RPH
    tb = jnp.pad(rel_table.astype(F32).T[head_of_row], ((0, 0), (0, LANES - REL_BUCKETS)))
    vec = lambda a: a.astype(F32).reshape(1, -1)

    def page_map(i):
        return lambda bi, s, pt: (pt[bi * n_pages + jnp.minimum(s, n_steps - 1) * pps + i], 0, 0)

    tok = pl.BlockSpec((1, lq, width), lambda bi, s, pt: (bi, 0, 0))
    const = lambda shape: pl.BlockSpec(shape, lambda bi, s, pt: (0,) * len(shape))
    kpage = [pl.BlockSpec((1, width, PAGE_SIZE), page_map(i)) for i in range(pps)]
    vpage = [pl.BlockSpec((1, PAGE_SIZE * DIFF_HEADS, DIFF_DV), page_map(i)) for i in range(pps)]
    grid_spec = pltpu.PrefetchScalarGridSpec(
        num_scalar_prefetch=1,
        grid=(b, n_steps + 1),
        in_specs=[tok, tok, tok, const((rows, LANES))] + [const((1, DIFF_DH))] * 4 + [const((1, DIFF_DV))] + kpage + vpage,
        out_specs=tok,
        scratch_shapes=[pltpu.VMEM((rows, width), BF16), pltpu.VMEM((rows, 1), F32), pltpu.VMEM((rows, 1), F32),
                        pltpu.VMEM((rows, DIFF_DV), F32)],
    )
    return pl.pallas_call(
        functools.partial(_attn_sample_kernel, pps=pps, n_pages=n_pages, lq=lq, lam_init=lam_init),
        grid_spec=grid_spec,
        out_shape=jax.ShapeDtypeStruct((b, lq, width), F32),
        compiler_params=_cparams(("parallel", "arbitrary")),
        name="diff_attention_sample",
    )(page_table.reshape(-1).astype(jnp.int32), q, k_new, v_new, tb, *[vec(a) for a in lams], vec(subln_g),
      *([cache_kt] * pps), *([cache_v] * pps))


def kernel(x_prompt, x_sample, state_ret, state_ssm, state_conv, cache_k_diff, cache_v_diff, page_table, norm_mix_g, norm_ffn_g, norm_final_g, ret_w_in, ret_w_out, ssd_w_in, ssd_conv_w, ssd_conv_b, ssd_dt_bias, ssd_A_log, ssd_D, ssd_norm_g, ssd_w_out, diff_w_in, diff_lam_q1, diff_lam_k1, diff_lam_q2, diff_lam_k2, diff_subln_g, diff_w_out, rel_bias_table, ffn_w_in, ffn_w_out):
    kinds = tuple(i % 3 for i in range(DEPTH))
    n_pages = page_table.shape[1]
    bf = lambda a: a.astype(BF16)
    ret_wi, ret_wo = bf(ret_w_in), bf(ret_w_out)
    ssd_wmain = bf(ssd_w_in[:, :, :SSD_INNER + SSD_CONV_CH])
    ssd_wdt = bf(jnp.pad(ssd_w_in[:, :, SSD_INNER + SSD_CONV_CH:], ((0, 0), (0, 0), (0, LANES - SSD_HEADS))))
    ssd_wo = bf(ssd_w_out)
    diff_wi, diff_wo = bf(diff_w_in), bf(diff_w_out)
    ffn_wi, ffn_wo = bf(ffn_w_in), bf(ffn_w_out)

    def run_group(x, sample):
        b, l, d = x.shape
        t = b * l
        tm = 512 if t % 512 == 0 else t
        ret_c = RET_CHUNK if l % RET_CHUNK == 0 else l
        ssd_c = SSD_CHUNK if l % SSD_CHUNK == 0 else l
        act = F32 if sample else BF16
        offset = n_pages * PAGE_SIZE if sample else 0
        rope_rows = l if l % RET_CHUNK == 0 else RET_CHUNK
        cos, sin = rope_tables(offset, rope_rows)
        ret_new, ssm_new, conv_new, k_new, v_new = [], [], [], [], []
        h = x.reshape(t, d)
        for i in range(DEPTH):
            kind = kinds[i]
            j = kinds[:i].count(kind)
            g = norm_mix_g[i]
            if kind == 0:
                n_in = ret_wi.shape[2]
                (qkvg,) = norm_linear(h, g, ret_wi[j], [(0, n_in, act)], tm)
                s0 = state_ret[j] if sample else jnp.zeros((b, RET_HEADS, RET_DK, RET_DV), F32)
                y, s1 = retention_core(qkvg.reshape(b, l, n_in), cos, sin, s0, ret_c, act)
                ret_new.append(s1)
                h = linear_residual(y.reshape(t, RET_VDIM), ret_wo[j], h, tm)
            elif kind == 1:
                n_in = SSD_INNER + SSD_CONV_CH
                (zxbc,) = norm_linear(h, g, ssd_wmain[j], [(0, n_in, act)], tm)
                (dt_raw,) = norm_linear(h, g, ssd_wdt[j], [(0, LANES, F32)], tm)
                zxbc = zxbc.reshape(b, l, n_in)
                if sample:
                    buf, s0 = state_conv[j], state_ssm[j]
                    conv_new.append(zxbc[:, l - (SSD_CONV - 1):, SSD_INNER:])
                else:
                    buf = jnp.zeros((b, SSD_CONV - 1, SSD_CONV_CH), F32)
                    s0 = jnp.zeros((b, SSD_HEADS, SSD_HEADDIM, SSD_STATE), F32)
                    tail = h.reshape(b, l, d)[:, l - (SSD_CONV - 1):].reshape(b * (SSD_CONV - 1), d)
                    rows = -(-tail.shape[0] // 16) * 16
                    tail = jnp.pad(tail, ((0, rows - tail.shape[0]), (0, 0)))
                    (xbc_tail,) = norm_linear(tail, g, ssd_wmain[j][:, SSD_INNER:], [(0, SSD_CONV_CH, F32)], rows)
                    conv_new.append(xbc_tail[:b * (SSD_CONV - 1)].reshape(b, SSD_CONV - 1, SSD_CONV_CH))
                y, s1 = ssd_core(zxbc, dt_raw.reshape(b, l, LANES), buf, s0, ssd_conv_w[j], ssd_conv_b[j],
                                 ssd_dt_bias[j], ssd_A_log[j], ssd_D[j], ssd_norm_g[j], ssd_c, act)
                ssm_new.append(s1)
                h = linear_residual(y.reshape(t, SSD_INNER), ssd_wo[j], h, tm)
            else:
                lam_init = 0.8 - 0.6 * math.exp(-0.3 * i)
                lams = (diff_lam_q1[j], diff_lam_k1[j], diff_lam_q2[j], diff_lam_k2[j])
                if sample:
                    q, kn, vn = norm_linear(h, g, diff_wi[j], [(0, d, F32), (d, 2 * d, F32), (2 * d, 3 * d, F32)], tm)
                    cache_kt = jnp.transpose(cache_k_diff[j], (0, 2, 3, 4, 1)).reshape(-1, d, PAGE_SIZE)
                    cache_v = cache_v_diff[j].reshape(-1, PAGE_SIZE * DIFF_HEADS, DIFF_DV)
                    y = diff_attention_sample(q.reshape(b, l, d), kn.reshape(b, l, d), vn.reshape(b, l, d), cache_kt,
                                              cache_v, page_table, rel_bias_table, lams, diff_subln_g[j], lam_init)
                    k_new.append(kn.reshape(b, l, DIFF_HEADS, 2, DIFF_DH))
                    v_new.append(vn.reshape(b, l, DIFF_HEADS, DIFF_DV))
                else:
                    w = diff_wi[j]
                    q, kt32, kt16, v32, v16 = diff_in_proj(h.reshape(b, l, d), g, w[:, :d], w[:, d:2 * d].T, w[:, 2 * d:],
                                                           tm, ATTN_TILE)
                    y = diff_attention_prompt(q, kt16, v16, rel_bias_table, lams, diff_subln_g[j], lam_init, ATTN_TILE)
                    k_new.append(jnp.transpose(kt32.reshape(b, DIFF_HEADS, 2, DIFF_DH, l), (0, 4, 1, 2, 3)))
                    v_new.append(v32.reshape(b, l, DIFF_HEADS, DIFF_DV))
                h = linear_residual(y.reshape(t, d), diff_wo[j], h, tm)
            h = ffn(h, norm_ffn_g[i], ffn_wi[i], ffn_wo[i], tm)
        y = final_norm(h, norm_final_g, tm).reshape(b, l, d)
        return y, jnp.stack(ret_new), jnp.stack(ssm_new), jnp.stack(conv_new), jnp.stack(k_new), jnp.stack(v_new)

    y_p, ret_p, ssm_p, conv_p, k_p, v_p = run_group(x_prompt, False)
    y_s, ret_s, ssm_s, conv_s, k_s, v_s = run_group(x_sample, True)
    return (y_p, y_s, ret_p, ssm_p, conv_p, k_p, v_p, ret_s, ssm_s, conv_s, k_s, v_s)
```

```python
import functools
import math

import jax
import jax.numpy as jnp
import numpy as np
from jax import lax
from jax.experimental import pallas as pl
from jax.experimental.pallas import tpu as pltpu

F32 = jnp.float32
BF16 = jnp.bfloat16

D_MODEL = 1024
DEPTH = 4
PAGE_SIZE = 128
EPS = 1e-6
RET_CHUNK = 256
SSD_CHUNK = 128
ATTN_TILE = 256

RET_HEADS = 4
RET_DK = D_MODEL // RET_HEADS
RET_DV = 2 * RET_DK
RET_VDIM = RET_HEADS * RET_DV
ROPE_BASE = 10000.0
ROPE_HALF = RET_DK // 2

SSD_INNER = 2 * D_MODEL
SSD_HEADDIM = 64
SSD_HEADS = SSD_INNER // SSD_HEADDIM
SSD_GROUPS = 8
SSD_HPG = SSD_HEADS // SSD_GROUPS
SSD_STATE = 128
SSD_CONV = 4
SSD_BC = SSD_GROUPS * SSD_STATE
SSD_CONV_CH = SSD_INNER + 2 * SSD_BC
SSD_GW = SSD_HPG * SSD_HEADDIM

DIFF_HEADS = 8
DIFF_DH = D_MODEL // DIFF_HEADS // 2
DIFF_DV = 2 * DIFF_DH
REL_BUCKETS = 32
REL_MAX_DIST = 128

FFN_HIDDEN = -(-8 * D_MODEL // (3 * 256)) * 256

V7X_VMEM_BYTES = 64 * 1024 * 1024
LANES = 128
SUBLANES = 8
BF16_ROWS = 16
VMEM_LIMIT = 56 * 1024 * 1024


def _cparams(sem):
    return pltpu.CompilerParams(dimension_semantics=sem, vmem_limit_bytes=VMEM_LIMIT)


def _dot(a, b):
    return jnp.dot(a, b, preferred_element_type=F32)


def _dot_nt(a, b):
    return lax.dot_general(a, b, (((1,), (1,)), ((), ())), preferred_element_type=F32)


def _dot_tn(a, b):
    return lax.dot_general(a, b, (((0,), (0,)), ((), ())), preferred_element_type=F32)


def _rms(x, g):
    return x * lax.rsqrt(jnp.mean(x * x, axis=-1, keepdims=True) + EPS) * g


def _silu(x):
    half = 0.5 * x
    return half * jnp.tanh(half) + half


def _resident(shape):
    return pl.BlockSpec(shape, lambda *_: (0,) * len(shape), pipeline_mode=pl.Buffered(1))


def _resident_layer(stacked, layer):
    tail = stacked.shape[1:]
    return pl.BlockSpec((None,) + tail, lambda *_: (layer,) + (0,) * len(tail), pipeline_mode=pl.Buffered(1))


def _norm_linear_kernel(x_ref, g_ref, w_ref, *o_refs, outs, tn):
    xn = _rms(x_ref[...], g_ref[...]).astype(BF16)
    n = w_ref.shape[1]
    for c0 in range(0, n, tn):
        acc = _dot(xn, w_ref[:, c0:c0 + tn])
        for o_ref, (lo, hi, _) in zip(o_refs, outs):
            if lo <= c0 and c0 + tn <= hi:
                o_ref[:, c0 - lo:c0 - lo + tn] = acc.astype(o_ref.dtype)


def norm_linear(x, g, w_stack, layer, outs, tm, tn=512):
    t, d = x.shape
    n = w_stack.shape[2]
    tn = min(tn, n)
    assert t % tm == 0 and n % tn == 0 and all(lo % tn == 0 and hi % tn == 0 for lo, hi, _ in outs)
    return pl.pallas_call(
        functools.partial(_norm_linear_kernel, outs=tuple(outs), tn=tn),
        grid=(t // tm,),
        in_specs=[pl.BlockSpec((tm, d), lambda i: (i, 0)), _resident((1, d)), _resident_layer(w_stack, layer)],
        out_specs=[pl.BlockSpec((tm, hi - lo), lambda i: (i, 0)) for lo, hi, _ in outs],
        out_shape=[jax.ShapeDtypeStruct((t, hi - lo), dt) for lo, hi, dt in outs],
        compiler_params=_cparams(("parallel",)),
        name="norm_linear",
    )(x, g.reshape(1, d), w_stack)


def _out_ffn_kernel(y_ref, wmix_ref, h_ref, g_ref, wgu_ref, wo_ref, *rest, th, final):
    o_ref = rest[-1]
    h = h_ref[...] + _dot(y_ref[...].astype(BF16), wmix_ref[...])
    xn = _rms(h, g_ref[...]).astype(BF16)
    hidden = wo_ref.shape[0]
    o_ref[...] = h
    for c0 in range(0, hidden, th):
        gate = _dot(xn, wgu_ref[:, c0:c0 + th])
        up = _dot(xn, wgu_ref[:, hidden + c0:hidden + c0 + th])
        act = (_silu(gate) * up).astype(BF16)
        o_ref[...] += _dot(act, wo_ref[c0:c0 + th, :])
    if final:
        o_ref[...] = _rms(o_ref[...], rest[0][...])


def out_ffn(y, wmix_stack, j, h, g, wgu_stack, wo_stack, i, tm, g_final=None, th=256):
    t, k = y.shape
    d = h.shape[1]
    hidden = wo_stack.shape[1]
    assert hidden % th == 0 and t % tm == 0
    final = g_final is not None
    extra = [g_final.reshape(1, d)] if final else []
    return pl.pallas_call(
        functools.partial(_out_ffn_kernel, th=th, final=final),
        grid=(t // tm,),
        in_specs=[pl.BlockSpec((tm, k), lambda m: (m, 0)), _resident_layer(wmix_stack, j),
                  pl.BlockSpec((tm, d), lambda m: (m, 0)), _resident((1, d)), _resident_layer(wgu_stack, i),
                  _resident_layer(wo_stack, i)] + [_resident((1, d))] * len(extra),
        out_specs=pl.BlockSpec((tm, d), lambda m: (m, 0)),
        out_shape=jax.ShapeDtypeStruct((t, d), F32),
        compiler_params=_cparams(("parallel",)),
        name="out_ffn",
    )(y, wmix_stack, h, g.reshape(1, d), wgu_stack, wo_stack, *extra)


def _rope_kernel(inv_ref, cos_ref, sin_ref, *, offset):
    rows = cos_ref.shape[0]
    pos = (offset + pl.program_id(0) * rows + lax.broadcasted_iota(jnp.int32, (rows, 1), 0)).astype(F32)
    ang = pos * inv_ref[...]
    cos_ref[...] = jnp.cos(ang)
    sin_ref[...] = jnp.sin(ang)


def rope_tables(offset, rows):
    inv = (1.0 / (ROPE_BASE ** jnp.linspace(0.0, 1.0, ROPE_HALF, dtype=F32))).reshape(1, ROPE_HALF)
    tr = min(rows, 512)
    return pl.pallas_call(
        functools.partial(_rope_kernel, offset=offset),
        grid=(rows // tr,),
        in_specs=[_resident((1, ROPE_HALF))],
        out_specs=[pl.BlockSpec((tr, ROPE_HALF), lambda i: (i, 0))] * 2,
        out_shape=[jax.ShapeDtypeStruct((rows, ROPE_HALF), F32)] * 2,
        compiler_params=_cparams(("parallel",)),
        name="rope_tables",
    )(inv)


def _ret_kernel(lg_ref, q_ref, k_ref, v_ref, g_ref, cos_ref, sin_ref, s0_ref, *rest, C, c_real, carry):
    y_ref, s1_ref = rest[carry:carry + 2]
    stage = rest[carry + 2:]
    hd = pl.program_id(1)
    c = pl.program_id(2)
    lg = lg_ref[hd]

    @pl.when(c == 0)
    def _():
        s1_ref[...] = s0_ref[...]

    def load(ref, st):
        if c_real == C:
            return ref[0].astype(F32)
        st[...] = jnp.zeros(st.shape, F32)
        st[0:c_real, :] = ref[0].astype(F32)
        return st[...]

    st = stage if stage else (None,) * 3
    q = load(q_ref, st[0])
    k = load(k_ref, st[1])
    v = load(v_ref, st[2]).astype(BF16)
    cos = cos_ref[...]
    sin = sin_ref[...]

    def rot(x):
        x1, x2 = x[:, :ROPE_HALF], x[:, ROPE_HALF:]
        return jnp.concatenate([x1 * cos - x2 * sin, x2 * cos + x1 * sin], axis=-1)

    qr = rot(q).astype(BF16)
    kr = rot(k) * (RET_DK ** -0.5)
    i = lax.broadcasted_iota(jnp.int32, (C, 1), 0).astype(F32)
    j = lax.broadcasted_iota(jnp.int32, (1, C), 1).astype(F32)
    rel = i - j
    decay = jnp.where(rel >= 0, jnp.exp(jnp.maximum(rel, 0.0) * lg), 0.0)
    scores = _dot_nt(qr, kr.astype(BF16)) * decay
    s_prev = s1_ref[0, 0]
    o = _dot(scores.astype(BF16), v) + _dot(qr, s_prev.astype(BF16)) * jnp.exp((i + 1.0) * lg)
    kw = (kr * jnp.exp((c_real - 1.0 - i) * lg)).astype(BF16)
    chunk_decay = jnp.exp(jnp.zeros((1, 1), F32) + c_real * lg)
    s1_ref[0, 0] = chunk_decay * s_prev + _dot_tn(kw, v)
    o = o * lax.rsqrt(jnp.mean(o * o, axis=-1, keepdims=True) + EPS)
    o = o[0:c_real]
    y_ref[0] = (_silu(g_ref[0].astype(F32)) * o).astype(y_ref.dtype)


def retention_core(qkvg, cos, sin, s0_stack, j0, s1_stack, j1, n_layers, c_real, out_dtype):
    b, l, _ = qkvg.shape
    C = cos.shape[0] if c_real < RET_CHUNK else RET_CHUNK
    nc = l // c_real
    kb, vb = RET_DK, RET_DV
    k_off = RET_HEADS
    v_off = 2 * RET_HEADS * RET_DK // RET_DV
    g_off = v_off + RET_HEADS
    lg = jnp.asarray(np.log(1.0 - 2.0 ** (-5.0 - np.arange(RET_HEADS))), F32)
    stage = [] if c_real == C else [pltpu.VMEM((C, kb), F32), pltpu.VMEM((C, kb), F32), pltpu.VMEM((C, vb), F32)]
    carry = s1_stack is not None
    return pl.pallas_call(
        functools.partial(_ret_kernel, C=C, c_real=c_real, carry=int(carry)),
        grid=(b, RET_HEADS, nc),
        in_specs=[
            pl.BlockSpec(memory_space=pltpu.SMEM),
            pl.BlockSpec((1, c_real, kb), lambda bi, h, c: (bi, c, h)),
            pl.BlockSpec((1, c_real, kb), lambda bi, h, c: (bi, c, k_off + h)),
            pl.BlockSpec((1, c_real, vb), lambda bi, h, c: (bi, c, v_off + h)),
            pl.BlockSpec((1, c_real, vb), lambda bi, h, c: (bi, c, g_off + h)),
            pl.BlockSpec((C, ROPE_HALF), lambda bi, h, c: (c, 0)),
            pl.BlockSpec((C, ROPE_HALF), lambda bi, h, c: (c, 0)),
            pl.BlockSpec((None, 1, 1, RET_DK, RET_DV), lambda bi, h, c: (j0, bi, h, 0, 0)),
        ] + ([pl.BlockSpec(memory_space=pl.ANY)] if carry else []),
        out_specs=[
            pl.BlockSpec((1, c_real, vb), lambda bi, h, c: (bi, c, h)),
            pl.BlockSpec((None, 1, 1, RET_DK, RET_DV), lambda bi, h, c: (j1, bi, h, 0, 0)),
        ],
        out_shape=[jax.ShapeDtypeStruct((b, l, RET_VDIM), out_dtype),
                   jax.ShapeDtypeStruct((n_layers, b, RET_HEADS, RET_DK, RET_DV), F32)],
        scratch_shapes=stage,
        input_output_aliases={8: 1} if carry else {},
        compiler_params=_cparams(("parallel", "parallel", "arbitrary")),
        name="retention_core",
    )(lg, qkvg, qkvg, qkvg, qkvg, cos, sin, s0_stack, *([s1_stack] if carry else []))


def _softplus(x):
    return jnp.maximum(x, 0.0) + jnp.log1p(jnp.exp(-jnp.abs(x)))


def _cumsum(x, axis):
    n = x.shape[axis]
    idx = lax.broadcasted_iota(jnp.int32, x.shape, axis)
    s = 1
    while s < n:
        x = x + jnp.where(idx >= s, pltpu.roll(x, s, axis), 0.0)
        s *= 2
    return x


def _ssd_kernel(z_ref, x_ref, bc_ref, dt_ref, dtt_ref, cw_ref, cb_ref, dtb_ref, dtbt_ref, al_ref, alt_ref,
                dsk_ref, ng_ref, buf_ref, s0_ref, y_ref, s1_ref, extx, extbc, xs_ref, bcs_ref, *, C, c_real):
    c = pl.program_id(1)
    halo = SUBLANES

    @pl.when(c == 0)
    def _():
        s1_ref[...] = s0_ref[...]
        extx[...] = jnp.zeros(extx.shape, F32)
        extbc[...] = jnp.zeros(extbc.shape, F32)
        extx[0:halo, :] = buf_ref[0, :, 0:SSD_INNER]
        extbc[0:halo, :] = buf_ref[0, :, SSD_INNER:SSD_CONV_CH]

    extx[halo:halo + c_real, :] = x_ref[0].astype(F32)
    extbc[halo:halo + c_real, :] = bc_ref[0].astype(F32)

    cw = 512
    first = halo - (SSD_CONV - 1)
    for ext, dst, ch0 in ((extx, xs_ref, 0), (extbc, bcs_ref, SSD_INNER)):
        for col in range(0, SSD_INNER, cw):
            acc = cb_ref[:, ch0 + col:ch0 + col + cw]
            for tap in range(SSD_CONV):
                acc = acc + ext[first + tap:first + tap + C, col:col + cw] * cw_ref[tap:tap + 1, ch0 + col:ch0 + col + cw]
            dst[:, col:col + cw] = _silu(acc).astype(dst.dtype)
    extx[0:halo, :] = extx[C:C + halo, :]
    extbc[0:halo, :] = extbc[C:C + halo, :]

    row = lax.broadcasted_iota(jnp.int32, (C, 1), 0)
    col_i = lax.broadcasted_iota(jnp.int32, (1, C), 1)
    dt = jnp.where(row < c_real, _softplus(dt_ref[0, 0] + dtb_ref[...]), 0.0)
    dtt = jnp.where(col_i < c_real, _softplus(dtt_ref[0, 0] + dtbt_ref[...]), 0.0)
    cs = _cumsum(dt * (-jnp.exp(al_ref[...])), 0)
    cst = _cumsum(dtt * (-jnp.exp(alt_ref[...])), 1)
    causal = row >= col_i
    lane_r = lax.broadcasted_iota(jnp.int32, (1, SSD_GW), 1) // SSD_HEADDIM
    sub_r = lax.broadcasted_iota(jnp.int32, (SSD_GW, 1), 0) // SSD_HEADDIM

    def by_head(vals, sel):
        out = vals[SSD_HPG - 1]
        for r in range(SSD_HPG - 2, -1, -1):
            out = jnp.where(sel == r, vals[r], out)
        return out

    for g in range(SSD_GROUPS):
        bm = bcs_ref[:, g * SSD_STATE:(g + 1) * SSD_STATE]
        cm = bcs_ref[:, SSD_BC + g * SSD_STATE:SSD_BC + (g + 1) * SSD_STATE]
        cb = _dot_nt(cm, bm)
        h0 = s1_ref[0, g * SSD_HPG:(g + 1) * SSD_HPG].reshape(SSD_GW, SSD_STATE)
        xg = xs_ref[:, g * SSD_GW:(g + 1) * SSD_GW]
        heads = range(g * SSD_HPG, (g + 1) * SSD_HPG)
        cs_cols = [cs[:, hh:hh + 1] for hh in heads]
        cs_last = [cs[C - 1:C, hh:hh + 1] for hh in heads]
        yg = _dot_nt(cm, h0.astype(BF16)) * by_head([jnp.exp(cc) for cc in cs_cols], lane_r)
        for r, hh in enumerate(heads):
            seg = cs_cols[r] - cst[hh:hh + 1, :]
            w = cb * jnp.exp(jnp.where(causal, seg, -jnp.inf)) * dtt[hh:hh + 1, :]
            yg = yg + _dot(w.astype(BF16), jnp.where(lane_r == r, xg, 0.0).astype(BF16))
        w_end = by_head([jnp.exp(cl - cc) * dt[:, hh:hh + 1] for cl, cc, hh in zip(cs_last, cs_cols, heads)], lane_r)
        h1 = by_head([jnp.exp(cl) for cl in cs_last], sub_r) * h0 + _dot_tn((xg * w_end).astype(BF16), bm)
        s1_ref[0, g * SSD_HPG:(g + 1) * SSD_HPG] = h1.reshape(SSD_HPG, SSD_HEADDIM, SSD_STATE)
        sl = slice(g * SSD_GW, (g + 1) * SSD_GW)
        yo = (yg[0:c_real] + dsk_ref[:, sl] * xg[0:c_real]) * _silu(z_ref[0, :, sl].astype(F32))
        yo = yo * lax.rsqrt(jnp.mean(yo * yo, axis=-1, keepdims=True) + EPS) * ng_ref[:, sl]
        y_ref[0, :, sl] = yo.astype(y_ref.dtype)


def ssd_core(zxbc, dt_raw, conv_buf, s0, conv_w, conv_b, dt_bias, a_log, d_skip, norm_g, c_real, out_dtype):
    b, l, _ = zxbc.shape
    C = SSD_CHUNK
    nc = l // c_real
    halo = SUBLANES
    dt4 = dt_raw.reshape(b, nc, c_real, LANES)
    dtt = jnp.swapaxes(dt4[..., :SSD_HEADS], -1, -2)
    if c_real < C:
        dt4 = jnp.pad(dt4, ((0, 0), (0, 0), (0, C - c_real), (0, 0)))
        dtt = jnp.pad(dtt, ((0, 0), (0, 0), (0, 0), (0, C - c_real)))
    lane_pad = LANES - SSD_HEADS
    buf8 = jnp.pad(conv_buf.astype(F32), ((0, 0), (halo - (SSD_CONV - 1), 0), (0, 0)))
    small = [
        conv_w.astype(F32), conv_b.reshape(1, -1).astype(F32),
        jnp.pad(dt_bias.astype(F32), (0, lane_pad)).reshape(1, LANES), dt_bias.astype(F32).reshape(SSD_HEADS, 1),
        jnp.pad(a_log.astype(F32), (0, lane_pad)).reshape(1, LANES), a_log.astype(F32).reshape(SSD_HEADS, 1),
        jnp.repeat(d_skip.astype(F32), SSD_HEADDIM).reshape(1, SSD_INNER), norm_g.astype(F32).reshape(1, SSD_INNER),
    ]
    w = SSD_INNER
    return pl.pallas_call(
        functools.partial(_ssd_kernel, C=C, c_real=c_real),
        grid=(b, nc),
        in_specs=[
            pl.BlockSpec((1, c_real, w), lambda bi, c: (bi, c, 0)),
            pl.BlockSpec((1, c_real, w), lambda bi, c: (bi, c, 1)),
            pl.BlockSpec((1, c_real, w), lambda bi, c: (bi, c, 2)),
            pl.BlockSpec((1, 1, C, LANES), lambda bi, c: (bi, c, 0, 0)),
            pl.BlockSpec((1, 1, SSD_HEADS, C), lambda bi, c: (bi, c, 0, 0)),
        ] + [_resident(a.shape) for a in small] + [
            pl.BlockSpec((1, halo, SSD_CONV_CH), lambda bi, c: (bi, 0, 0)),
            pl.BlockSpec((1, SSD_HEADS, SSD_HEADDIM, SSD_STATE), lambda bi, c: (bi, 0, 0, 0)),
        ],
        out_specs=[
            pl.BlockSpec((1, c_real, w), lambda bi, c: (bi, c, 0)),
            pl.BlockSpec((1, SSD_HEADS, SSD_HEADDIM, SSD_STATE), lambda bi, c: (bi, 0, 0, 0)),
        ],
        out_shape=[jax.ShapeDtypeStruct((b, l, SSD_INNER), out_dtype),
                   jax.ShapeDtypeStruct((b, SSD_HEADS, SSD_HEADDIM, SSD_STATE), F32)],
        scratch_shapes=[pltpu.VMEM((C + halo, w), F32), pltpu.VMEM((C + halo, w), F32),
                        pltpu.VMEM((C, w), F32), pltpu.VMEM((C, w), BF16)],
        compiler_params=_cparams(("parallel", "arbitrary")),
        name="ssd_core",
    )(zxbc, zxbc, zxbc, dt4, dtt, *small, buf8, s0)


def _t5_bias(dist, table_at):
    n = jnp.maximum(dist, 0)
    exact = REL_BUCKETS // 2
    nf = jnp.maximum(n, 1).astype(F32)
    large = exact + (jnp.log(nf / exact) / math.log(REL_MAX_DIST / exact) * (REL_BUCKETS - exact)).astype(jnp.int32)
    bucket = jnp.where(n < exact, n, jnp.minimum(large, REL_BUCKETS - 1))
    bias = jnp.zeros(dist.shape, F32)
    for bkt in range(REL_BUCKETS):
        bias = jnp.where(bucket == bkt, table_at(bkt), bias)
    return bias


def _lambda(lq1, lk1, lq2, lk2, lam_init):
    s1 = jnp.sum(lq1[...] * lk1[...], axis=-1, keepdims=True)
    s2 = jnp.sum(lq2[...] * lk2[...], axis=-1, keepdims=True)
    return jnp.exp(s1) - jnp.exp(s2) + lam_init


def _diff_in_proj_kernel(x_ref, g_ref, wq_ref, wkt_ref, wv_ref, q_ref, kt32_ref, kt16_ref, v32_ref, v16_ref, *, tn, kb):
    xn = _rms(x_ref[0], g_ref[...]).astype(BF16)
    d = wq_ref.shape[1]
    tm = xn.shape[0]
    for c0 in range(0, d, tn):
        q_ref[0, :, c0:c0 + tn] = (_dot(xn, wq_ref[:, c0:c0 + tn]) * (DIFF_DH ** -0.5)).astype(BF16)
        v = _dot(xn, wv_ref[:, c0:c0 + tn])
        v32_ref[0, :, c0:c0 + tn] = v
        v16_ref[0, :, c0:c0 + tn] = v.astype(BF16)
        kt = _dot_nt(wkt_ref[c0:c0 + tn, :], xn)
        kt32_ref[0, c0:c0 + tn, :] = kt
        for s in range(tm // kb):
            kt16_ref[0, s, c0:c0 + tn, :] = kt[:, s * kb:(s + 1) * kb].astype(BF16)


def diff_in_proj(x, g, wq, wkt, wv, tm, kb):
    b, l, d = x.shape
    assert l % tm == 0 and tm % kb == 0
    tok = lambda: pl.BlockSpec((1, tm, d), lambda bi, i: (bi, i, 0))
    return pl.pallas_call(
        functools.partial(_diff_in_proj_kernel, tn=512, kb=kb),
        grid=(b, l // tm),
        in_specs=[tok(), _resident((1, d)), _resident((d, d)), _resident((d, d)), _resident((d, d))],
        out_specs=[tok(), pl.BlockSpec((1, d, tm), lambda bi, i: (bi, 0, i)),
                   pl.BlockSpec((1, tm // kb, d, kb), lambda bi, i: (bi, i, 0, 0)), tok(), tok()],
        out_shape=[jax.ShapeDtypeStruct((b, l, d), BF16), jax.ShapeDtypeStruct((b, d, l), F32),
                   jax.ShapeDtypeStruct((b, l // kb, d, kb), BF16), jax.ShapeDtypeStruct((b, l, d), F32),
                   jax.ShapeDtypeStruct((b, l, d), BF16)],
        compiler_params=_cparams(("parallel", "parallel")),
        name="diff_in_proj",
    )(x, g.reshape(1, d), wq, wkt, wv)


def _attn_prompt_kernel(tbl_ref, q_ref, kt_ref, v_ref, lq1, lk1, lq2, lk2, sg_ref, o_ref,
                        bias_sc, m_sc, a_sc, *, T, lam_init):
    hd = pl.program_id(1)
    qi = pl.program_id(2)

    @pl.when(qi == 0)
    def _():
        i = lax.broadcasted_iota(jnp.int32, (T, 2 * T), 0)
        j = lax.broadcasted_iota(jnp.int32, (T, 2 * T), 1)
        dist = i - j + T
        bias_sc[...] = jnp.where(dist >= 0, _t5_bias(dist, lambda bkt: tbl_ref[bkt, hd]), -jnp.inf)

    lane = lax.broadcasted_iota(jnp.int32, (1, DIFF_DV), 1)
    q = q_ref[0]
    zero = jnp.zeros((), q.dtype)
    q2 = jnp.concatenate([jnp.where(lane < DIFF_DH, q, zero), jnp.where(lane >= DIFF_DH, q, zero)], axis=0)
    m_sc[...] = jnp.full(m_sc.shape, -jnp.inf, F32)
    a_sc[...] = jnp.zeros(a_sc.shape, F32)

    def update(kb0, nblk, bias):
        w = nblk * T
        kt = jnp.concatenate([kt_ref[0, kb0 + n] for n in range(nblk)], axis=1) if nblk > 1 else kt_ref[0, kb0]
        vt = v_ref[0, pl.ds(pl.multiple_of(kb0 * T, T), w), :]
        vx = jnp.concatenate([vt, jnp.ones((w, DIFF_DV), BF16)], axis=1)
        s = _dot(q2, kt)
        s = s + (jnp.concatenate([bias, bias], axis=0) if getattr(bias, "ndim", 0) == 2 else bias)
        m_prev = m_sc[...]
        m_new = jnp.maximum(m_prev, jnp.max(s, axis=-1, keepdims=True))
        alpha = jnp.exp(m_prev - m_new)
        p = jnp.concatenate([jnp.exp(s[:, c:c + LANES] - m_new) for c in range(0, w, LANES)], axis=1)
        a_sc[...] = jnp.concatenate([alpha, alpha], axis=1) * a_sc[...] + _dot(p.astype(BF16), vx)
        m_sc[...] = m_new

    far_bias = tbl_ref[REL_BUCKETS - 1, hd]
    n_far = jnp.maximum(qi - 1, 0)

    def far_body(i2, carry):
        update(2 * i2, 2, far_bias)
        return carry

    lax.fori_loop(0, n_far // 2, far_body, 0)

    @pl.when(n_far % 2 == 1)
    def _():
        update(n_far - 1, 1, far_bias)

    @pl.when(qi >= 1)
    def _():
        update(qi - 1, 2, bias_sc[...])

    @pl.when(qi == 0)
    def _():
        update(0, 1, bias_sc[:, T:2 * T])

    lam = _lambda(lq1, lk1, lq2, lk2, lam_init)
    o = a_sc[:T, :DIFF_DV] / a_sc[:T, DIFF_DV:] - lam * (a_sc[T:, :DIFF_DV] / a_sc[T:, DIFF_DV:])
    o_ref[0] = (_rms(o, sg_ref[...]) * (1.0 - lam_init)).astype(o_ref.dtype)


def diff_attention_prompt(q, kt, v, rel_table, lams, subln_g, lam_init, T):
    b, l, _ = q.shape
    assert T >= REL_MAX_DIST and l % T == 0 and DIFF_DV == 2 * DIFF_DH
    vec = lambda a: a.astype(F32).reshape(1, -1)
    return pl.pallas_call(
        functools.partial(_attn_prompt_kernel, T=T, lam_init=lam_init),
        grid=(b, DIFF_HEADS, l // T),
        in_specs=[
            pl.BlockSpec(memory_space=pltpu.SMEM),
            pl.BlockSpec((1, T, DIFF_DV), lambda bi, h, qi: (bi, qi, h)),
            pl.BlockSpec((1, l // T, DIFF_DV, T), lambda bi, h, qi: (bi, 0, h, 0)),
            pl.BlockSpec((1, l, DIFF_DV), lambda bi, h, qi: (bi, 0, h)),
        ] + [_resident((1, DIFF_DH))] * 4 + [_resident((1, DIFF_DV))],
        out_specs=pl.BlockSpec((1, T, DIFF_DV), lambda bi, h, qi: (bi, qi, h)),
        out_shape=jax.ShapeDtypeStruct((b, l, DIFF_HEADS * DIFF_DV), BF16),
        scratch_shapes=[pltpu.VMEM((T, 2 * T), F32), pltpu.VMEM((2 * T, LANES), F32), pltpu.VMEM((2 * T, 2 * DIFF_DV), F32)],
        compiler_params=_cparams(("parallel", "parallel", "arbitrary")),
        name="diff_attention_prompt",
    )(rel_table.astype(F32), q, kt, v, *[vec(a) for a in lams], vec(subln_g))


DEC_RPH = SUBLANES


def _attn_sample_kernel(pt_ref, q_ref, kn_ref, vn_ref, tb_ref, lq1, lk1, lq2, lk2, sg_ref, *rest,
                        pps, n_pages, lq, lam_init):
    kt_refs, v_refs = rest[:pps], rest[pps:2 * pps]
    o_ref, qm_sc, m_sc, l_sc, acc_sc = rest[2 * pps:]
    s = pl.program_id(1)
    n_steps = n_pages // pps
    past = n_pages * PAGE_SIZE
    rows = DIFF_HEADS * DEC_RPH
    width = DIFF_HEADS * DIFF_DV
    rid = lax.broadcasted_iota(jnp.int32, (rows, 1), 0)
    row_tok = (rid % DEC_RPH) // 2

    def q_rows():
        r8 = lax.broadcasted_iota(jnp.int32, (DEC_RPH, width), 0)
        lane_pair = lax.broadcasted_iota(jnp.int32, (DEC_RPH, width), 1) // DIFF_DH
        qrep = jnp.zeros((DEC_RPH, width), F32)
        for t in range(lq):
            qrep = jnp.where(r8 // 2 == t, q_ref[0, t:t + 1, :], qrep)
        qrep = qrep * (DIFF_DH ** -0.5)
        return jnp.concatenate([jnp.where(lane_pair == 2 * hh + r8 % 2, qrep, 0.0) for hh in range(DIFF_HEADS)], axis=0)

    @pl.when(s == 0)
    def _():
        qm_sc[...] = q_rows().astype(BF16)
        m_sc[...] = jnp.full(m_sc.shape, -jnp.inf, F32)
        l_sc[...] = jnp.zeros(l_sc.shape, F32)
        acc_sc[...] = jnp.zeros(acc_sc.shape, F32)

    far_bias = tb_ref[:, REL_BUCKETS - 1:REL_BUCKETS]

    def pages_update(last_near):
        qm = qm_sc[...]
        sc = []
        for i in range(pps):
            sci = _dot(qm, kt_refs[i][0].astype(BF16))
            if last_near and i == pps - 1:
                kpos = (n_pages - 1) * PAGE_SIZE + lax.broadcasted_iota(jnp.int32, (1, PAGE_SIZE), 1)
                sc.append(sci + _t5_bias((past + row_tok) - kpos, lambda bkt: tb_ref[:, bkt:bkt + 1]))
            else:
                sc.append(sci + far_bias)
        m_prev = m_sc[...]
        m_new = m_prev
        for sci in sc:
            m_new = jnp.maximum(m_new, jnp.max(sci, axis=-1, keepdims=True))
        alpha = jnp.exp(m_prev - m_new)
        p = [jnp.exp(sci - m_new) for sci in sc]
        l_new = alpha * l_sc[...]
        for pi in p:
            l_new = l_new + jnp.sum(pi, axis=-1, keepdims=True)
        l_sc[...] = l_new
        m_sc[...] = m_new
        for hh in range(DIFF_HEADS):
            sl = slice(hh * DEC_RPH, (hh + 1) * DEC_RPH)
            ph = jnp.concatenate([pi[sl] for pi in p], axis=1).astype(BF16)
            vh = jnp.concatenate([v_refs[i][0, pl.ds(hh, PAGE_SIZE, stride=DIFF_HEADS), :].astype(BF16)
                                  for i in range(pps)], axis=0)
            acc_sc[sl, :] = alpha[sl] * acc_sc[sl, :] + _dot(ph, vh)

    @pl.when(s < n_steps - 1)
    def _():
        pages_update(False)

    @pl.when(s == n_steps - 1)
    def _():
        pages_update(True)

    @pl.when(s == n_steps)
    def _():
        qf = q_rows()
        sj = []
        for jn in range(lq):
            dist = row_tok - jn
            sc = jnp.sum(qf * kn_ref[0, jn:jn + 1, :], axis=-1, keepdims=True)
            sc = sc + _t5_bias(dist, lambda bkt: tb_ref[:, bkt:bkt + 1])
            sj.append(jnp.where(dist >= 0, sc, -jnp.inf))
        m_prev = m_sc[...]
        m_new = m_prev
        for sc in sj:
            m_new = jnp.maximum(m_new, sc)
        alpha = jnp.exp(m_prev - m_new)
        l_new = alpha * l_sc[...]
        acc = alpha * acc_sc[...]
        for jn, sc in enumerate(sj):
            p = jnp.exp(sc - m_new)
            l_new = l_new + p
            vrow = jnp.concatenate(
                [jnp.broadcast_to(vn_ref[0, jn:jn + 1, hh * DIFF_DV:(hh + 1) * DIFF_DV], (DEC_RPH, DIFF_DV))
                 for hh in range(DIFF_HEADS)], axis=0)
            acc = acc + p * vrow

        lam = _lambda(lq1, lk1, lq2, lk2, lam_init)
        coef = jnp.where(rid % 2 == 0, 1.0, -lam) / l_new
        a = acc * coef
        a = a + pltpu.roll(a, rows - 1, 0)
        res = _rms(a, sg_ref[...]) * (1.0 - lam_init)
        for hh in range(DIFF_HEADS):
            for t in range(lq):
                r = hh * DEC_RPH + 2 * t
                o_ref[0, t:t + 1, hh * DIFF_DV:(hh + 1) * DIFF_DV] = res[r:r + 1, :]


def diff_attention_sample(q, k_new, v_new, cache_kt, cache_v, page_table, rel_table, lams, subln_g, lam_init, pps=8):
    b, lq, width = q.shape
    n_pages = page_table.shape[1]
    assert n_pages % pps == 0 and 2 * lq <= DEC_RPH and n_pages // pps >= 1
    n_steps = n_pages // pps
    rows = DIFF_HEADS * DEC_RPH
    head_of_row = np.arange(rows) // DEC_RPH
    tb = jnp.pad(rel_table.astype(F32).T[head_of_row], ((0, 0), (0, LANES - REL_BUCKETS)))
    vec = lambda a: a.astype(F32).reshape(1, -1)

    def page_map(i):
        return lambda bi, s, pt: (pt[bi * n_pages + jnp.minimum(s, n_steps - 1) * pps + i], 0, 0)

    tok = pl.BlockSpec((1, lq, width), lambda bi, s, pt: (bi, 0, 0))
    const = lambda shape: pl.BlockSpec(shape, lambda bi, s, pt: (0,) * len(shape))
    kpage = [pl.BlockSpec((1, width, PAGE_SIZE), page_map(i)) for i in range(pps)]
    vpage = [pl.BlockSpec((1, PAGE_SIZE * DIFF_HEADS, DIFF_DV), page_map(i)) for i in range(pps)]
    grid_spec = pltpu.PrefetchScalarGridSpec(
        num_scalar_prefetch=1,
        grid=(b, n_steps + 1),
        in_specs=[tok, tok, tok, const((rows, LANES))] + [const((1, DIFF_DH))] * 4 + [const((1, DIFF_DV))] + kpage + vpage,
        out_specs=tok,
        scratch_shapes=[pltpu.VMEM((rows, width), BF16), pltpu.VMEM((rows, 1), F32), pltpu.VMEM((rows, 1), F32),
                        pltpu.VMEM((rows, DIFF_DV), F32)],
    )
    return pl.pallas_call(
        functools.partial(_attn_sample_kernel, pps=pps, n_pages=n_pages, lq=lq, lam_init=lam_init),
        grid_spec=grid_spec,
        out_shape=jax.ShapeDtypeStruct((b, lq, width), F32),
        compiler_params=_cparams(("parallel", "arbitrary")),
        name="diff_attention_sample",
    )(page_table.reshape(-1).astype(jnp.int32), q, k_new, v_new, tb, *[vec(a) for a in lams], vec(subln_g),
      *([cache_kt] * pps), *([cache_v] * pps))


def kernel(x_prompt, x_sample, state_ret, state_ssm, state_conv, cache_k_diff, cache_v_diff, page_table, norm_mix_g, norm_ffn_g, norm_final_g, ret_w_in, ret_w_out, ssd_w_in, ssd_conv_w, ssd_conv_b, ssd_dt_bias, ssd_A_log, ssd_D, ssd_norm_g, ssd_w_out, diff_w_in, diff_lam_q1, diff_lam_k1, diff_lam_q2, diff_lam_k2, diff_subln_g, diff_w_out, rel_bias_table, ffn_w_in, ffn_w_out):
    kinds = tuple(i % 3 for i in range(DEPTH))
    n_pages = page_table.shape[1]
    bf = lambda a: a.astype(BF16)
    ret_wi, ret_wo = bf(ret_w_in), bf(ret_w_out)
    ssd_wmain = bf(ssd_w_in[:, :, :SSD_INNER + SSD_CONV_CH])
    ssd_wxbc = ssd_wmain[:, :, SSD_INNER:]
    ssd_wdt = bf(jnp.pad(ssd_w_in[:, :, SSD_INNER + SSD_CONV_CH:], ((0, 0), (0, 0), (0, LANES - SSD_HEADS))))
    ssd_wo = bf(ssd_w_out)
    diff_wi, diff_wo = bf(diff_w_in), bf(diff_w_out)
    ffn_wi, ffn_wo = bf(ffn_w_in), bf(ffn_w_out)

    n_ret = kinds.count(0)

    def run_group(x, sample):
        b, l, d = x.shape
        t = b * l
        tm = 512 if t % 512 == 0 else t
        ret_c = RET_CHUNK if l % RET_CHUNK == 0 else l
        ssd_c = SSD_CHUNK if l % SSD_CHUNK == 0 else l
        act = F32 if sample else BF16
        offset = n_pages * PAGE_SIZE if sample else 0
        rope_rows = l if l % RET_CHUNK == 0 else -(-l // BF16_ROWS) * BF16_ROWS
        cos, sin = rope_tables(offset, rope_rows)
        ret_new, ssm_new, conv_new, k_new, v_new = None, [], [], [], []
        h = x.reshape(t, d)
        for i in range(DEPTH):
            kind = kinds[i]
            j = kinds[:i].count(kind)
            g = norm_mix_g[i]
            if kind == 0:
                n_in = ret_wi.shape[2]
                (qkvg,) = norm_linear(h, g, ret_wi, j, [(0, n_in, act)], tm)
                s0, j0 = (state_ret, j) if sample else (jnp.zeros((1, b, RET_HEADS, RET_DK, RET_DV), F32), 0)
                y, ret_new = retention_core(qkvg.reshape(b, l, n_in), cos, sin, s0, j0, ret_new, j, n_ret, ret_c, act)
                y, wmix = y.reshape(t, RET_VDIM), ret_wo
            elif kind == 1:
                n_in = SSD_INNER + SSD_CONV_CH
                (zxbc,) = norm_linear(h, g, ssd_wmain, j, [(0, n_in, act)], tm)
                (dt_raw,) = norm_linear(h, g, ssd_wdt, j, [(0, LANES, F32)], tm)
                zxbc = zxbc.reshape(b, l, n_in)
                if sample:
                    buf, s0 = state_conv[j], state_ssm[j]
                    conv_new.append(zxbc[:, l - (SSD_CONV - 1):, SSD_INNER:])
                else:
                    buf = jnp.zeros((b, SSD_CONV - 1, SSD_CONV_CH), F32)
                    s0 = jnp.zeros((b, SSD_HEADS, SSD_HEADDIM, SSD_STATE), F32)
                    tail = h.reshape(b, l, d)[:, l - (SSD_CONV - 1):].reshape(b * (SSD_CONV - 1), d)
                    rows = -(-tail.shape[0] // BF16_ROWS) * BF16_ROWS
                    tail = jnp.pad(tail, ((0, rows - tail.shape[0]), (0, 0)))
                    (xbc_tail,) = norm_linear(tail, g, ssd_wxbc, j, [(0, SSD_CONV_CH, F32)], rows)
                    conv_new.append(xbc_tail[:b * (SSD_CONV - 1)].reshape(b, SSD_CONV - 1, SSD_CONV_CH))
                y, s1 = ssd_core(zxbc, dt_raw.reshape(b, l, LANES), buf, s0, ssd_conv_w[j], ssd_conv_b[j],
                                 ssd_dt_bias[j], ssd_A_log[j], ssd_D[j], ssd_norm_g[j], ssd_c, act)
                ssm_new.append(s1)
                y, wmix = y.reshape(t, SSD_INNER), ssd_wo
            else:
                lam_init = 0.8 - 0.6 * math.exp(-0.3 * i)
                lams = (diff_lam_q1[j], diff_lam_k1[j], diff_lam_q2[j], diff_lam_k2[j])
                if sample:
                    q, kn, vn = norm_linear(h, g, diff_wi, j, [(0, d, F32), (d, 2 * d, F32), (2 * d, 3 * d, F32)], tm)
                    cache_kt = jnp.transpose(cache_k_diff[j], (0, 2, 3, 4, 1)).reshape(-1, d, PAGE_SIZE)
                    cache_v = cache_v_diff[j].reshape(-1, PAGE_SIZE * DIFF_HEADS, DIFF_DV)
                    y = diff_attention_sample(q.reshape(b, l, d), kn.reshape(b, l, d), vn.reshape(b, l, d), cache_kt,
                                              cache_v, page_table, rel_bias_table, lams, diff_subln_g[j], lam_init)
                    k_new.append(kn.reshape(b, l, DIFF_HEADS, 2, DIFF_DH))
                    v_new.append(vn.reshape(b, l, DIFF_HEADS, DIFF_DV))
                else:
                    w = diff_wi[j]
                    q, kt32, kt16, v32, v16 = diff_in_proj(h.reshape(b, l, d), g, w[:, :d], w[:, d:2 * d].T, w[:, 2 * d:],
                                                           tm, ATTN_TILE)
                    y = diff_attention_prompt(q, kt16, v16, rel_bias_table, lams, diff_subln_g[j], lam_init, ATTN_TILE)
                    k_new.append(jnp.transpose(kt32.reshape(b, DIFF_HEADS, 2, DIFF_DH, l), (0, 4, 1, 2, 3)))
                    v_new.append(v32.reshape(b, l, DIFF_HEADS, DIFF_DV))
                y, wmix = y.reshape(t, d), diff_wo
            h = out_ffn(y, wmix, j, h, norm_ffn_g[i], ffn_wi, ffn_wo, i, tm,
                        g_final=norm_final_g if i == DEPTH - 1 else None)
        return h.reshape(b, l, d), ret_new, jnp.stack(ssm_new), jnp.stack(conv_new), jnp.stack(k_new), jnp.stack(v_new)

    y_p, ret_p, ssm_p, conv_p, k_p, v_p = run_group(x_prompt, False)
    y_s, ret_s, ssm_s, conv_s, k_s, v_s = run_group(x_sample, True)
    return (y_p, y_s, ret_p, ssm_p, conv_p, k_p, v_p, ret_s, ssm_s, conv_s, k_s, v_s)
```

```python
import functools
import math

import jax
import jax.numpy as jnp
import numpy as np
from jax import lax
from jax.experimental import pallas as pl
from jax.experimental.pallas import tpu as pltpu

F32 = jnp.float32
BF16 = jnp.bfloat16

D_MODEL = 1024
DEPTH = 4
PAGE_SIZE = 128
EPS = 1e-6
RET_CHUNK = 256
SSD_CHUNK = 128
ATTN_TILE = 256

RET_HEADS = 4
RET_DK = D_MODEL // RET_HEADS
RET_DV = 2 * RET_DK
RET_VDIM = RET_HEADS * RET_DV
ROPE_BASE = 10000.0
ROPE_HALF = RET_DK // 2

SSD_INNER = 2 * D_MODEL
SSD_HEADDIM = 64
SSD_HEADS = SSD_INNER // SSD_HEADDIM
SSD_GROUPS = 8
SSD_HPG = SSD_HEADS // SSD_GROUPS
SSD_STATE = 128
SSD_CONV = 4
SSD_BC = SSD_GROUPS * SSD_STATE
SSD_CONV_CH = SSD_INNER + 2 * SSD_BC
SSD_GW = SSD_HPG * SSD_HEADDIM

DIFF_HEADS = 8
DIFF_DH = D_MODEL // DIFF_HEADS // 2
DIFF_DV = 2 * DIFF_DH
REL_BUCKETS = 32
REL_MAX_DIST = 128

FFN_HIDDEN = -(-8 * D_MODEL // (3 * 256)) * 256

V7X_VMEM_BYTES = 64 * 1024 * 1024
LANES = 128
SUBLANES = 8
BF16_ROWS = 16
VMEM_LIMIT = 56 * 1024 * 1024


def _cparams(sem):
    return pltpu.CompilerParams(dimension_semantics=sem, vmem_limit_bytes=VMEM_LIMIT)


def _dot(a, b):
    return jnp.dot(a, b, preferred_element_type=F32)


def _dot_nt(a, b):
    return lax.dot_general(a, b, (((1,), (1,)), ((), ())), preferred_element_type=F32)


def _dot_tn(a, b):
    return lax.dot_general(a, b, (((0,), (0,)), ((), ())), preferred_element_type=F32)


def _rms(x, g):
    return x * lax.rsqrt(jnp.mean(x * x, axis=-1, keepdims=True) + EPS) * g


def _silu(x):
    half = 0.5 * x
    return half * jnp.tanh(half) + half


def _resident(shape):
    return pl.BlockSpec(shape, lambda *_: (0,) * len(shape), pipeline_mode=pl.Buffered(1))


def _resident_layer(stacked, layer):
    tail = stacked.shape[1:]
    return pl.BlockSpec((None,) + tail, lambda *_: (layer,) + (0,) * len(tail), pipeline_mode=pl.Buffered(1))


def _norm_linear_kernel(x_ref, g_ref, w_ref, *o_refs, outs, tn):
    xn = _rms(x_ref[...], g_ref[...]).astype(BF16)
    n = w_ref.shape[1]
    for c0 in range(0, n, tn):
        acc = _dot(xn, w_ref[:, c0:c0 + tn])
        for o_ref, (lo, hi, _) in zip(o_refs, outs):
            if lo <= c0 and c0 + tn <= hi:
                o_ref[:, c0 - lo:c0 - lo + tn] = acc.astype(o_ref.dtype)


def norm_linear(x, g, w_stack, layer, outs, tm, tn=512):
    t, d = x.shape
    n = w_stack.shape[2]
    tn = min(tn, n)
    assert t % tm == 0 and n % tn == 0 and all(lo % tn == 0 and hi % tn == 0 for lo, hi, _ in outs)
    return pl.pallas_call(
        functools.partial(_norm_linear_kernel, outs=tuple(outs), tn=tn),
        grid=(t // tm,),
        in_specs=[pl.BlockSpec((tm, d), lambda i: (i, 0)), _resident((1, d)), _resident_layer(w_stack, layer)],
        out_specs=[pl.BlockSpec((tm, hi - lo), lambda i: (i, 0)) for lo, hi, _ in outs],
        out_shape=[jax.ShapeDtypeStruct((t, hi - lo), dt) for lo, hi, dt in outs],
        compiler_params=_cparams(("parallel",)),
        name="norm_linear",
    )(x, g.reshape(1, d), w_stack)


def _out_ffn_kernel(y_ref, wmix_ref, h_ref, g_ref, wgu_ref, wo_ref, *rest, th, final):
    o_ref = rest[-1]
    h = h_ref[...] + _dot(y_ref[...].astype(BF16), wmix_ref[...])
    xn = _rms(h, g_ref[...]).astype(BF16)
    hidden = wo_ref.shape[0]
    o_ref[...] = h
    for c0 in range(0, hidden, th):
        gate = _dot(xn, wgu_ref[:, c0:c0 + th])
        up = _dot(xn, wgu_ref[:, hidden + c0:hidden + c0 + th])
        act = (_silu(gate) * up).astype(BF16)
        o_ref[...] += _dot(act, wo_ref[c0:c0 + th, :])
    if final:
        o_ref[...] = _rms(o_ref[...], rest[0][...])


def out_ffn(y, wmix_stack, j, h, g, wgu_stack, wo_stack, i, tm, g_final=None, th=256):
    t, k = y.shape
    d = h.shape[1]
    hidden = wo_stack.shape[1]
    assert hidden % th == 0 and t % tm == 0
    final = g_final is not None
    extra = [g_final.reshape(1, d)] if final else []
    return pl.pallas_call(
        functools.partial(_out_ffn_kernel, th=th, final=final),
        grid=(t // tm,),
        in_specs=[pl.BlockSpec((tm, k), lambda m: (m, 0)), _resident_layer(wmix_stack, j),
                  pl.BlockSpec((tm, d), lambda m: (m, 0)), _resident((1, d)), _resident_layer(wgu_stack, i),
                  _resident_layer(wo_stack, i)] + [_resident((1, d))] * len(extra),
        out_specs=pl.BlockSpec((tm, d), lambda m: (m, 0)),
        out_shape=jax.ShapeDtypeStruct((t, d), F32),
        compiler_params=_cparams(("parallel",)),
        name="out_ffn",
    )(y, wmix_stack, h, g.reshape(1, d), wgu_stack, wo_stack, *extra)


def _rope_kernel(inv_ref, cos_ref, sin_ref, *, offset):
    rows = cos_ref.shape[0]
    pos = (offset + pl.program_id(0) * rows + lax.broadcasted_iota(jnp.int32, (rows, 1), 0)).astype(F32)
    ang = pos * inv_ref[...]
    cos_ref[...] = jnp.cos(ang)
    sin_ref[...] = jnp.sin(ang)


def rope_tables(offset, rows):
    inv = (1.0 / (ROPE_BASE ** jnp.linspace(0.0, 1.0, ROPE_HALF, dtype=F32))).reshape(1, ROPE_HALF)
    tr = min(rows, 512)
    return pl.pallas_call(
        functools.partial(_rope_kernel, offset=offset),
        grid=(rows // tr,),
        in_specs=[_resident((1, ROPE_HALF))],
        out_specs=[pl.BlockSpec((tr, ROPE_HALF), lambda i: (i, 0))] * 2,
        out_shape=[jax.ShapeDtypeStruct((rows, ROPE_HALF), F32)] * 2,
        compiler_params=_cparams(("parallel",)),
        name="rope_tables",
    )(inv)


def _ret_kernel(q_ref, k_ref, v_ref, g_ref, cos_ref, sin_ref, s0_ref, *rest, C, c_real, carry, slot, n_chunks):
    y_ref, st_ref = rest[carry:carry + 2]
    decay_sc = rest[carry + 2]
    stage = rest[carry + 3:]
    c = pl.program_id(1)
    log_gamma = [math.log(1.0 - 2.0 ** (-5.0 - hd)) for hd in range(RET_HEADS)]
    i = lax.broadcasted_iota(jnp.int32, (C, 1), 0).astype(F32)
    j = lax.broadcasted_iota(jnp.int32, (1, C), 1).astype(F32)

    @pl.when(c == 0)
    def _():
        st_ref[slot] = s0_ref[...]
        rel = i - j
        for hd in range(RET_HEADS):
            decay_sc[hd] = jnp.where(rel >= 0, jnp.exp(jnp.maximum(rel, 0.0) * log_gamma[hd]), 0.0)

    if c_real == C:
        q_src, k_src, v_src = q_ref.at[0], k_ref.at[0], v_ref.at[0]
    else:
        for ref, st in zip((q_ref, k_ref, v_ref), stage):
            st[...] = jnp.zeros(st.shape, F32)
            st[0:c_real, :] = ref[0].astype(F32)
        q_src, k_src, v_src = stage
    cos = cos_ref[...]
    sin = sin_ref[...]

    def rot(x):
        x1, x2 = x[:, :ROPE_HALF], x[:, ROPE_HALF:]
        return jnp.concatenate([x1 * cos - x2 * sin, x2 * cos + x1 * sin], axis=-1)

    for hd in range(RET_HEADS):
        lg = log_gamma[hd]
        ks = slice(hd * RET_DK, (hd + 1) * RET_DK)
        vs = slice(hd * RET_DV, (hd + 1) * RET_DV)
        qr = rot(q_src[:, ks].astype(F32)).astype(BF16)
        kr = rot(k_src[:, ks].astype(F32)) * (RET_DK ** -0.5)
        v = v_src[:, vs].astype(BF16)
        scores = _dot_nt(qr, kr.astype(BF16)) * decay_sc[hd]
        s_prev = st_ref[slot, 0, hd]
        o = _dot(scores.astype(BF16), v) + _dot(qr, s_prev.astype(BF16)) * jnp.exp((i + 1.0) * lg)
        kw = (kr * jnp.exp((c_real - 1.0 - i) * lg)).astype(BF16)
        st_ref[slot, 0, hd] = math.exp(c_real * lg) * s_prev + _dot_tn(kw, v)
        o = o * lax.rsqrt(jnp.mean(o * o, axis=-1, keepdims=True) + EPS)
        y_ref[0, :, vs] = (_silu(g_ref[0, :, vs].astype(F32)) * o[0:c_real]).astype(y_ref.dtype)

    if st_ref.shape[0] > 1:
        @pl.when(c == n_chunks - 1)
        def _():
            for other in range(st_ref.shape[0]):
                if other != slot:
                    st_ref[other] = st_ref[slot]


def retention_core(qkvg, cos, sin, s0_stack, j0, s1_stack, j1, n_layers, c_real, out_dtype):
    b, l, _ = qkvg.shape
    C = cos.shape[0] if c_real < RET_CHUNK else RET_CHUNK
    nc = l // c_real
    qk_w, v_w = RET_HEADS * RET_DK, RET_VDIM
    k_blk, v_blk, g_blk = 1, 2 * qk_w // v_w, 2 * qk_w // v_w + 1
    stage = [] if c_real == C else [pltpu.VMEM((C, qk_w), F32), pltpu.VMEM((C, qk_w), F32), pltpu.VMEM((C, v_w), F32)]
    carry = s1_stack is not None
    st_block, st_first, slot = (1, j1, 0) if carry else (n_layers, 0, j1)
    return pl.pallas_call(
        functools.partial(_ret_kernel, C=C, c_real=c_real, carry=int(carry), slot=slot, n_chunks=nc),
        grid=(b, nc),
        in_specs=[
            pl.BlockSpec((1, c_real, qk_w), lambda bi, c: (bi, c, 0)),
            pl.BlockSpec((1, c_real, qk_w), lambda bi, c: (bi, c, k_blk)),
            pl.BlockSpec((1, c_real, v_w), lambda bi, c: (bi, c, v_blk)),
            pl.BlockSpec((1, c_real, v_w), lambda bi, c: (bi, c, g_blk)),
            pl.BlockSpec((C, ROPE_HALF), lambda bi, c: (c, 0)),
            pl.BlockSpec((C, ROPE_HALF), lambda bi, c: (c, 0)),
            pl.BlockSpec((None, 1, RET_HEADS, RET_DK, RET_DV), lambda bi, c: (j0, bi, 0, 0, 0)),
        ] + ([pl.BlockSpec(memory_space=pl.ANY)] if carry else []),
        out_specs=[
            pl.BlockSpec((1, c_real, v_w), lambda bi, c: (bi, c, 0)),
            pl.BlockSpec((st_block, 1, RET_HEADS, RET_DK, RET_DV), lambda bi, c: (st_first, bi, 0, 0, 0)),
        ],
        out_shape=[jax.ShapeDtypeStruct((b, l, RET_VDIM), out_dtype),
                   jax.ShapeDtypeStruct((n_layers, b, RET_HEADS, RET_DK, RET_DV), F32)],
        scratch_shapes=[pltpu.VMEM((RET_HEADS, C, C), F32)] + stage,
        input_output_aliases={7: 1} if carry else {},
        compiler_params=_cparams(("parallel", "arbitrary")),
        name="retention_core",
    )(qkvg, qkvg, qkvg, qkvg, cos, sin, s0_stack, *([s1_stack] if carry else []))


def _softplus(x):
    return jnp.maximum(x, 0.0) + jnp.log1p(jnp.exp(-jnp.abs(x)))


def _cumsum(x, axis):
    n = x.shape[axis]
    idx = lax.broadcasted_iota(jnp.int32, x.shape, axis)
    s = 1
    while s < n:
        x = x + jnp.where(idx >= s, pltpu.roll(x, s, axis), 0.0)
        s *= 2
    return x


def _ssd_kernel(z_ref, x_ref, bc_ref, dt_ref, dtt_ref, cw_ref, cb_ref, dtb_ref, dtbt_ref, al_ref, alt_ref,
                dsk_ref, ng_ref, buf_ref, s0_ref, y_ref, s1_ref, extx, extbc, xs_ref, bcs_ref, *, C, c_real):
    c = pl.program_id(1)
    halo = SUBLANES

    @pl.when(c == 0)
    def _():
        s1_ref[...] = s0_ref[...]
        extx[...] = jnp.zeros(extx.shape, F32)
        extbc[...] = jnp.zeros(extbc.shape, F32)
        extx[0:halo, :] = buf_ref[0, :, 0:SSD_INNER]
        extbc[0:halo, :] = buf_ref[0, :, SSD_INNER:SSD_CONV_CH]

    extx[halo:halo + c_real, :] = x_ref[0].astype(F32)
    extbc[halo:halo + c_real, :] = bc_ref[0].astype(F32)

    cw = 512
    first = halo - (SSD_CONV - 1)
    for ext, dst, ch0 in ((extx, xs_ref, 0), (extbc, bcs_ref, SSD_INNER)):
        for col in range(0, SSD_INNER, cw):
            acc = cb_ref[:, ch0 + col:ch0 + col + cw]
            for tap in range(SSD_CONV):
                acc = acc + ext[first + tap:first + tap + C, col:col + cw] * cw_ref[tap:tap + 1, ch0 + col:ch0 + col + cw]
            dst[:, col:col + cw] = _silu(acc).astype(dst.dtype)
    extx[0:halo, :] = extx[C:C + halo, :]
    extbc[0:halo, :] = extbc[C:C + halo, :]

    row = lax.broadcasted_iota(jnp.int32, (C, 1), 0)
    col_i = lax.broadcasted_iota(jnp.int32, (1, C), 1)
    dt = jnp.where(row < c_real, _softplus(dt_ref[0, 0] + dtb_ref[...]), 0.0)
    dtt = jnp.where(col_i < c_real, _softplus(dtt_ref[0, 0] + dtbt_ref[...]), 0.0)
    cs = _cumsum(dt * (-jnp.exp(al_ref[...])), 0)
    cst = _cumsum(dtt * (-jnp.exp(alt_ref[...])), 1)
    causal = row >= col_i
    lane_r = lax.broadcasted_iota(jnp.int32, (1, SSD_GW), 1) // SSD_HEADDIM
    sub_r = lax.broadcasted_iota(jnp.int32, (SSD_GW, 1), 0) // SSD_HEADDIM

    def by_head(vals, sel):
        out = vals[SSD_HPG - 1]
        for r in range(SSD_HPG - 2, -1, -1):
            out = jnp.where(sel == r, vals[r], out)
        return out

    for g in range(SSD_GROUPS):
        bm = bcs_ref[:, g * SSD_STATE:(g + 1) * SSD_STATE]
        cm = bcs_ref[:, SSD_BC + g * SSD_STATE:SSD_BC + (g + 1) * SSD_STATE]
        cb = _dot_nt(cm, bm)
        h0 = s1_ref[0, g * SSD_HPG:(g + 1) * SSD_HPG].reshape(SSD_GW, SSD_STATE)
        xg = xs_ref[:, g * SSD_GW:(g + 1) * SSD_GW]
        heads = range(g * SSD_HPG, (g + 1) * SSD_HPG)
        cs_cols = [cs[:, hh:hh + 1] for hh in heads]
        cs_last = [cs[C - 1:C, hh:hh + 1] for hh in heads]
        yg = _dot_nt(cm, h0.astype(BF16)) * by_head([jnp.exp(cc) for cc in cs_cols], lane_r)
        for r, hh in enumerate(heads):
            seg = cs_cols[r] - cst[hh:hh + 1, :]
            w = cb * jnp.exp(jnp.where(causal, seg, -jnp.inf)) * dtt[hh:hh + 1, :]
            yg = yg + _dot(w.astype(BF16), jnp.where(lane_r == r, xg, 0.0).astype(BF16))
        w_end = by_head([jnp.exp(cl - cc) * dt[:, hh:hh + 1] for cl, cc, hh in zip(cs_last, cs_cols, heads)], lane_r)
        h1 = by_head([jnp.exp(cl) for cl in cs_last], sub_r) * h0 + _dot_tn((xg * w_end).astype(BF16), bm)
        s1_ref[0, g * SSD_HPG:(g + 1) * SSD_HPG] = h1.reshape(SSD_HPG, SSD_HEADDIM, SSD_STATE)
        sl = slice(g * SSD_GW, (g + 1) * SSD_GW)
        yo = (yg[0:c_real] + dsk_ref[:, sl] * xg[0:c_real]) * _silu(z_ref[0, :, sl].astype(F32))
        yo = yo * lax.rsqrt(jnp.mean(yo * yo, axis=-1, keepdims=True) + EPS) * ng_ref[:, sl]
        y_ref[0, :, sl] = yo.astype(y_ref.dtype)


def ssd_core(zxbc, dt_raw, conv_buf, s0, conv_w, conv_b, dt_bias, a_log, d_skip, norm_g, c_real, out_dtype):
    b, l, _ = zxbc.shape
    C = SSD_CHUNK
    nc = l // c_real
    halo = SUBLANES
    dt4 = dt_raw.reshape(b, nc, c_real, LANES)
    dtt = jnp.swapaxes(dt4[..., :SSD_HEADS], -1, -2)
    if c_real < C:
        dt4 = jnp.pad(dt4, ((0, 0), (0, 0), (0, C - c_real), (0, 0)))
        dtt = jnp.pad(dtt, ((0, 0), (0, 0), (0, 0), (0, C - c_real)))
    lane_pad = LANES - SSD_HEADS
    buf8 = jnp.pad(conv_buf.astype(F32), ((0, 0), (halo - (SSD_CONV - 1), 0), (0, 0)))
    small = [
        conv_w.astype(F32), conv_b.reshape(1, -1).astype(F32),
        jnp.pad(dt_bias.astype(F32), (0, lane_pad)).reshape(1, LANES), dt_bias.astype(F32).reshape(SSD_HEADS, 1),
        jnp.pad(a_log.astype(F32), (0, lane_pad)).reshape(1, LANES), a_log.astype(F32).reshape(SSD_HEADS, 1),
        jnp.repeat(d_skip.astype(F32), SSD_HEADDIM).reshape(1, SSD_INNER), norm_g.astype(F32).reshape(1, SSD_INNER),
    ]
    w = SSD_INNER
    return pl.pallas_call(
        functools.partial(_ssd_kernel, C=C, c_real=c_real),
        grid=(b, nc),
        in_specs=[
            pl.BlockSpec((1, c_real, w), lambda bi, c: (bi, c, 0)),
            pl.BlockSpec((1, c_real, w), lambda bi, c: (bi, c, 1)),
            pl.BlockSpec((1, c_real, w), lambda bi, c: (bi, c, 2)),
            pl.BlockSpec((1, 1, C, LANES), lambda bi, c: (bi, c, 0, 0)),
            pl.BlockSpec((1, 1, SSD_HEADS, C), lambda bi, c: (bi, c, 0, 0)),
        ] + [_resident(a.shape) for a in small] + [
            pl.BlockSpec((1, halo, SSD_CONV_CH), lambda bi, c: (bi, 0, 0)),
            pl.BlockSpec((1, SSD_HEADS, SSD_HEADDIM, SSD_STATE), lambda bi, c: (bi, 0, 0, 0)),
        ],
        out_specs=[
            pl.BlockSpec((1, c_real, w), lambda bi, c: (bi, c, 0)),
            pl.BlockSpec((1, SSD_HEADS, SSD_HEADDIM, SSD_STATE), lambda bi, c: (bi, 0, 0, 0)),
        ],
        out_shape=[jax.ShapeDtypeStruct((b, l, SSD_INNER), out_dtype),
                   jax.ShapeDtypeStruct((b, SSD_HEADS, SSD_HEADDIM, SSD_STATE), F32)],
        scratch_shapes=[pltpu.VMEM((C + halo, w), F32), pltpu.VMEM((C + halo, w), F32),
                        pltpu.VMEM((C, w), F32), pltpu.VMEM((C, w), BF16)],
        compiler_params=_cparams(("parallel", "arbitrary")),
        name="ssd_core",
    )(zxbc, zxbc, zxbc, dt4, dtt, *small, buf8, s0)


def _t5_bias(dist, table_at):
    n = jnp.maximum(dist, 0)
    exact = REL_BUCKETS // 2
    nf = jnp.maximum(n, 1).astype(F32)
    large = exact + (jnp.log(nf / exact) / math.log(REL_MAX_DIST / exact) * (REL_BUCKETS - exact)).astype(jnp.int32)
    bucket = jnp.where(n < exact, n, jnp.minimum(large, REL_BUCKETS - 1))
    bias = jnp.zeros(dist.shape, F32)
    for bkt in range(REL_BUCKETS):
        bias = jnp.where(bucket == bkt, table_at(bkt), bias)
    return bias


def _lambda(lq1, lk1, lq2, lk2, lam_init):
    s1 = jnp.sum(lq1[...] * lk1[...], axis=-1, keepdims=True)
    s2 = jnp.sum(lq2[...] * lk2[...], axis=-1, keepdims=True)
    return jnp.exp(s1) - jnp.exp(s2) + lam_init


def _diff_in_proj_kernel(x_ref, g_ref, wq_ref, wkt_ref, wv_ref, q_ref, kt32_ref, kt16_ref, v32_ref, v16_ref, *, tn, kb):
    xn = _rms(x_ref[0], g_ref[...]).astype(BF16)
    d = wq_ref.shape[1]
    tm = xn.shape[0]
    for c0 in range(0, d, tn):
        q_ref[0, :, c0:c0 + tn] = (_dot(xn, wq_ref[:, c0:c0 + tn]) * (DIFF_DH ** -0.5)).astype(BF16)
        v = _dot(xn, wv_ref[:, c0:c0 + tn])
        v32_ref[0, :, c0:c0 + tn] = v
        v16_ref[0, :, c0:c0 + tn] = v.astype(BF16)
        kt = _dot_nt(wkt_ref[c0:c0 + tn, :], xn)
        kt32_ref[0, c0:c0 + tn, :] = kt
        for s in range(tm // kb):
            kt16_ref[0, s, c0:c0 + tn, :] = kt[:, s * kb:(s + 1) * kb].astype(BF16)


def diff_in_proj(x, g, wq, wkt, wv, tm, kb):
    b, l, d = x.shape
    assert l % tm == 0 and tm % kb == 0
    tok = lambda: pl.BlockSpec((1, tm, d), lambda bi, i: (bi, i, 0))
    return pl.pallas_call(
        functools.partial(_diff_in_proj_kernel, tn=512, kb=kb),
        grid=(b, l // tm),
        in_specs=[tok(), _resident((1, d)), _resident((d, d)), _resident((d, d)), _resident((d, d))],
        out_specs=[tok(), pl.BlockSpec((1, d, tm), lambda bi, i: (bi, 0, i)),
                   pl.BlockSpec((1, tm // kb, d, kb), lambda bi, i: (bi, i, 0, 0)), tok(), tok()],
        out_shape=[jax.ShapeDtypeStruct((b, l, d), BF16), jax.ShapeDtypeStruct((b, d, l), F32),
                   jax.ShapeDtypeStruct((b, l // kb, d, kb), BF16), jax.ShapeDtypeStruct((b, l, d), F32),
                   jax.ShapeDtypeStruct((b, l, d), BF16)],
        compiler_params=_cparams(("parallel", "parallel")),
        name="diff_in_proj",
    )(x, g.reshape(1, d), wq, wkt, wv)


def _attn_prompt_kernel(tbl_ref, q_ref, kt_ref, v_ref, lq1, lk1, lq2, lk2, sg_ref, o_ref,
                        bias_sc, m_sc, a_sc, *, T, lam_init):
    hd = pl.program_id(1)
    qi = pl.program_id(2)

    @pl.when(qi == 0)
    def _():
        i = lax.broadcasted_iota(jnp.int32, (T, 2 * T), 0)
        j = lax.broadcasted_iota(jnp.int32, (T, 2 * T), 1)
        dist = i - j + T
        bias_sc[...] = jnp.where(dist >= 0, _t5_bias(dist, lambda bkt: tbl_ref[bkt, hd]), -jnp.inf)

    lane = lax.broadcasted_iota(jnp.int32, (1, DIFF_DV), 1)
    q = q_ref[0]
    zero = jnp.zeros((), q.dtype)
    q2 = jnp.concatenate([jnp.where(lane < DIFF_DH, q, zero), jnp.where(lane >= DIFF_DH, q, zero)], axis=0)
    m_sc[...] = jnp.full(m_sc.shape, -jnp.inf, F32)
    a_sc[...] = jnp.zeros(a_sc.shape, F32)

    def update(kb0, nblk, bias):
        w = nblk * T
        kt = jnp.concatenate([kt_ref[0, kb0 + n] for n in range(nblk)], axis=1) if nblk > 1 else kt_ref[0, kb0]
        vt = v_ref[0, pl.ds(pl.multiple_of(kb0 * T, T), w), :]
        vx = jnp.concatenate([vt, jnp.ones((w, DIFF_DV), BF16)], axis=1)
        s = _dot(q2, kt)
        s = s + (jnp.concatenate([bias, bias], axis=0) if getattr(bias, "ndim", 0) == 2 else bias)
        m_prev = m_sc[...]
        m_new = jnp.maximum(m_prev, jnp.max(s, axis=-1, keepdims=True))
        alpha = jnp.exp(m_prev - m_new)
        p = jnp.concatenate([jnp.exp(s[:, c:c + LANES] - m_new) for c in range(0, w, LANES)], axis=1)
        a_sc[...] = jnp.concatenate([alpha, alpha], axis=1) * a_sc[...] + _dot(p.astype(BF16), vx)
        m_sc[...] = m_new

    far_bias = tbl_ref[REL_BUCKETS - 1, hd]
    n_far = jnp.maximum(qi - 1, 0)

    def far_body(i2, carry):
        update(2 * i2, 2, far_bias)
        return carry

    lax.fori_loop(0, n_far // 2, far_body, 0)

    @pl.when(n_far % 2 == 1)
    def _():
        update(n_far - 1, 1, far_bias)

    @pl.when(qi >= 1)
    def _():
        update(qi - 1, 2, bias_sc[...])

    @pl.when(qi == 0)
    def _():
        update(0, 1, bias_sc[:, T:2 * T])

    lam = _lambda(lq1, lk1, lq2, lk2, lam_init)
    o = a_sc[:T, :DIFF_DV] / a_sc[:T, DIFF_DV:] - lam * (a_sc[T:, :DIFF_DV] / a_sc[T:, DIFF_DV:])
    o_ref[0] = (_rms(o, sg_ref[...]) * (1.0 - lam_init)).astype(o_ref.dtype)


def diff_attention_prompt(q, kt, v, rel_table, lams, subln_g, lam_init, T):
    b, l, _ = q.shape
    assert T >= REL_MAX_DIST and l % T == 0 and DIFF_DV == 2 * DIFF_DH
    vec = lambda a: a.astype(F32).reshape(1, -1)
    return pl.pallas_call(
        functools.partial(_attn_prompt_kernel, T=T, lam_init=lam_init),
        grid=(b, DIFF_HEADS, l // T),
        in_specs=[
            pl.BlockSpec(memory_space=pltpu.SMEM),
            pl.BlockSpec((1, T, DIFF_DV), lambda bi, h, qi: (bi, qi, h)),
            pl.BlockSpec((1, l // T, DIFF_DV, T), lambda bi, h, qi: (bi, 0, h, 0)),
            pl.BlockSpec((1, l, DIFF_DV), lambda bi, h, qi: (bi, 0, h)),
        ] + [_resident((1, DIFF_DH))] * 4 + [_resident((1, DIFF_DV))],
        out_specs=pl.BlockSpec((1, T, DIFF_DV), lambda bi, h, qi: (bi, qi, h)),
        out_shape=jax.ShapeDtypeStruct((b, l, DIFF_HEADS * DIFF_DV), BF16),
        scratch_shapes=[pltpu.VMEM((T, 2 * T), F32), pltpu.VMEM((2 * T, LANES), F32), pltpu.VMEM((2 * T, 2 * DIFF_DV), F32)],
        compiler_params=_cparams(("parallel", "parallel", "arbitrary")),
        name="diff_attention_prompt",
    )(rel_table.astype(F32), q, kt, v, *[vec(a) for a in lams], vec(subln_g))


DEC_RPH = SUBLANES


def _attn_sample_kernel(pt_ref, q_ref, kn_ref, vn_ref, tb_ref, lq1, lk1, lq2, lk2, sg_ref, *rest,
                        pps, n_pages, lq, lam_init):
    kt_refs, v_refs = rest[:pps], rest[pps:2 * pps]
    o_ref, qm_sc, m_sc, l_sc, acc_sc = rest[2 * pps:]
    s = pl.program_id(1)
    n_steps = n_pages // pps
    past = n_pages * PAGE_SIZE
    rows = DIFF_HEADS * DEC_RPH
    width = DIFF_HEADS * DIFF_DV
    rid = lax.broadcasted_iota(jnp.int32, (rows, 1), 0)
    row_tok = (rid % DEC_RPH) // 2

    def q_rows():
        r8 = lax.broadcasted_iota(jnp.int32, (DEC_RPH, width), 0)
        lane_pair = lax.broadcasted_iota(jnp.int32, (DEC_RPH, width), 1) // DIFF_DH
        qrep = jnp.zeros((DEC_RPH, width), F32)
        for t in range(lq):
            qrep = jnp.where(r8 // 2 == t, q_ref[0, t:t + 1, :], qrep)
        qrep = qrep * (DIFF_DH ** -0.5)
        return jnp.concatenate([jnp.where(lane_pair == 2 * hh + r8 % 2, qrep, 0.0) for hh in range(DIFF_HEADS)], axis=0)

    @pl.when(s == 0)
    def _():
        qm_sc[...] = q_rows().astype(BF16)
        m_sc[...] = jnp.full(m_sc.shape, -jnp.inf, F32)
        l_sc[...] = jnp.zeros(l_sc.shape, F32)
        acc_sc[...] = jnp.zeros(acc_sc.shape, F32)

    far_bias = tb_ref[:, REL_BUCKETS - 1:REL_BUCKETS]

    def pages_update(last_near):
        qm = qm_sc[...]
        sc = []
        for i in range(pps):
            sci = _dot(qm, kt_refs[i][0].astype(BF16))
            if last_near and i == pps - 1:
                kpos = (n_pages - 1) * PAGE_SIZE + lax.broadcasted_iota(jnp.int32, (1, PAGE_SIZE), 1)
                sc.append(sci + _t5_bias((past + row_tok) - kpos, lambda bkt: tb_ref[:, bkt:bkt + 1]))
            else:
                sc.append(sci + far_bias)
        m_prev = m_sc[...]
        m_new = m_prev
        for sci in sc:
            m_new = jnp.maximum(m_new, jnp.max(sci, axis=-1, keepdims=True))
        alpha = jnp.exp(m_prev - m_new)
        p = [jnp.exp(sci - m_new) for sci in sc]
        l_new = alpha * l_sc[...]
        for pi in p:
            l_new = l_new + jnp.sum(pi, axis=-1, keepdims=True)
        l_sc[...] = l_new
        m_sc[...] = m_new
        for hh in range(DIFF_HEADS):
            sl = slice(hh * DEC_RPH, (hh + 1) * DEC_RPH)
            ph = jnp.concatenate([pi[sl] for pi in p], axis=1).astype(BF16)
            vh = jnp.concatenate([v_refs[i][0, pl.ds(hh, PAGE_SIZE, stride=DIFF_HEADS), :].astype(BF16)
                                  for i in range(pps)], axis=0)
            acc_sc[sl, :] = alpha[sl] * acc_sc[sl, :] + _dot(ph, vh)

    @pl.when(s < n_steps - 1)
    def _():
        pages_update(False)

    @pl.when(s == n_steps - 1)
    def _():
        pages_update(True)

    @pl.when(s == n_steps)
    def _():
        qf = q_rows()
        sj = []
        for jn in range(lq):
            dist = row_tok - jn
            sc = jnp.sum(qf * kn_ref[0, jn:jn + 1, :], axis=-1, keepdims=True)
            sc = sc + _t5_bias(dist, lambda bkt: tb_ref[:, bkt:bkt + 1])
            sj.append(jnp.where(dist >= 0, sc, -jnp.inf))
        m_prev = m_sc[...]
        m_new = m_prev
        for sc in sj:
            m_new = jnp.maximum(m_new, sc)
        alpha = jnp.exp(m_prev - m_new)
        l_new = alpha * l_sc[...]
        acc = alpha * acc_sc[...]
        for jn, sc in enumerate(sj):
            p = jnp.exp(sc - m_new)
            l_new = l_new + p
            vrow = jnp.concatenate(
                [jnp.broadcast_to(vn_ref[0, jn:jn + 1, hh * DIFF_DV:(hh + 1) * DIFF_DV], (DEC_RPH, DIFF_DV))
                 for hh in range(DIFF_HEADS)], axis=0)
            acc = acc + p * vrow

        lam = _lambda(lq1, lk1, lq2, lk2, lam_init)
        coef = jnp.where(rid % 2 == 0, 1.0, -lam) / l_new
        a = acc * coef
        a = a + pltpu.roll(a, rows - 1, 0)
        res = _rms(a, sg_ref[...]) * (1.0 - lam_init)
        for hh in range(DIFF_HEADS):
            for t in range(lq):
                r = hh * DEC_RPH + 2 * t
                o_ref[0, t:t + 1, hh * DIFF_DV:(hh + 1) * DIFF_DV] = res[r:r + 1, :]


def diff_attention_sample(q, k_new, v_new, cache_kt, cache_v, page_table, rel_table, lams, subln_g, lam_init, pps=8):
    b, lq, width = q.shape
    n_pages = page_table.shape[1]
    assert n_pages % pps == 0 and 2 * lq <= DEC_RPH and n_pages // pps >= 1
    n_steps = n_pages // pps
    rows = DIFF_HEADS * DEC_RPH
    head_of_row = np.arange(rows) // DEC_RPH
    tb = jnp.pad(rel_table.astype(F32).T[head_of_row], ((0, 0), (0, LANES - REL_BUCKETS)))
    vec = lambda a: a.astype(F32).reshape(1, -1)

    def page_map(i):
        return lambda bi, s, pt: (pt[bi * n_pages + jnp.minimum(s, n_steps - 1) * pps + i], 0, 0)

    tok = pl.BlockSpec((1, lq, width), lambda bi, s, pt: (bi, 0, 0))
    const = lambda shape: pl.BlockSpec(shape, lambda bi, s, pt: (0,) * len(shape))
    kpage = [pl.BlockSpec((1, width, PAGE_SIZE), page_map(i)) for i in range(pps)]
    vpage = [pl.BlockSpec((1, PAGE_SIZE * DIFF_HEADS, DIFF_DV), page_map(i)) for i in range(pps)]
    grid_spec = pltpu.PrefetchScalarGridSpec(
        num_scalar_prefetch=1,
        grid=(b, n_steps + 1),
        in_specs=[tok, tok, tok, const((rows, LANES))] + [const((1, DIFF_DH))] * 4 + [const((1, DIFF_DV))] + kpage + vpage,
        out_specs=tok,
        scratch_shapes=[pltpu.VMEM((rows, width), BF16), pltpu.VMEM((rows, 1), F32), pltpu.VMEM((rows, 1), F32),
                        pltpu.VMEM((rows, DIFF_DV), F32)],
    )
    return pl.pallas_call(
        functools.partial(_attn_sample_kernel, pps=pps, n_pages=n_pages, lq=lq, lam_init=lam_init),
        grid_spec=grid_spec,
        out_shape=jax.ShapeDtypeStruct((b, lq, width), F32),
        compiler_params=_cparams(("parallel", "arbitrary")),
        name="diff_attention_sample",
    )(page_table.reshape(-1).astype(jnp.int32), q, k_new, v_new, tb, *[vec(a) for a in lams], vec(subln_g),
      *([cache_kt] * pps), *([cache_v] * pps))


def kernel(x_prompt, x_sample, state_ret, state_ssm, state_conv, cache_k_diff, cache_v_diff, page_table, norm_mix_g, norm_ffn_g, norm_final_g, ret_w_in, ret_w_out, ssd_w_in, ssd_conv_w, ssd_conv_b, ssd_dt_bias, ssd_A_log, ssd_D, ssd_norm_g, ssd_w_out, diff_w_in, diff_lam_q1, diff_lam_k1, diff_lam_q2, diff_lam_k2, diff_subln_g, diff_w_out, rel_bias_table, ffn_w_in, ffn_w_out):
    kinds = tuple(i % 3 for i in range(DEPTH))
    n_pages = page_table.shape[1]
    bf = lambda a: a.astype(BF16)
    ret_wi, ret_wo = bf(ret_w_in), bf(ret_w_out)
    ssd_wmain = bf(ssd_w_in[:, :, :SSD_INNER + SSD_CONV_CH])
    ssd_wxbc = ssd_wmain[:, :, SSD_INNER:]
    ssd_wdt = bf(jnp.pad(ssd_w_in[:, :, SSD_INNER + SSD_CONV_CH:], ((0, 0), (0, 0), (0, LANES - SSD_HEADS))))
    ssd_wo = bf(ssd_w_out)
    diff_wi, diff_wo = bf(diff_w_in), bf(diff_w_out)
    ffn_wi, ffn_wo = bf(ffn_w_in), bf(ffn_w_out)

    n_ret = kinds.count(0)

    def run_group(x, sample):
        b, l, d = x.shape
        t = b * l
        tm = 512 if t % 512 == 0 else t
        ret_c = RET_CHUNK if l % RET_CHUNK == 0 else l
        ssd_c = SSD_CHUNK if l % SSD_CHUNK == 0 else l
        act = F32 if sample else BF16
        offset = n_pages * PAGE_SIZE if sample else 0
        rope_rows = l if l % RET_CHUNK == 0 else -(-l // BF16_ROWS) * BF16_ROWS
        cos, sin = rope_tables(offset, rope_rows)
        ret_new, ssm_new, conv_new, k_new, v_new = None, [], [], [], []
        h = x.reshape(t, d)
        for i in range(DEPTH):
            kind = kinds[i]
            j = kinds[:i].count(kind)
            g = norm_mix_g[i]
            if kind == 0:
                n_in = ret_wi.shape[2]
                (qkvg,) = norm_linear(h, g, ret_wi, j, [(0, n_in, act)], tm)
                s0, j0 = (state_ret, j) if sample else (jnp.zeros((1, b, RET_HEADS, RET_DK, RET_DV), F32), 0)
                y, ret_new = retention_core(qkvg.reshape(b, l, n_in), cos, sin, s0, j0, ret_new, j, n_ret, ret_c, act)
                y, wmix = y.reshape(t, RET_VDIM), ret_wo
            elif kind == 1:
                n_in = SSD_INNER + SSD_CONV_CH
                (zxbc,) = norm_linear(h, g, ssd_wmain, j, [(0, n_in, act)], tm)
                (dt_raw,) = norm_linear(h, g, ssd_wdt, j, [(0, LANES, F32)], tm)
                zxbc = zxbc.reshape(b, l, n_in)
                if sample:
                    buf, s0 = state_conv[j], state_ssm[j]
                    conv_new.append(zxbc[:, l - (SSD_CONV - 1):, SSD_INNER:])
                else:
                    buf = jnp.zeros((b, SSD_CONV - 1, SSD_CONV_CH), F32)
                    s0 = jnp.zeros((b, SSD_HEADS, SSD_HEADDIM, SSD_STATE), F32)
                    tail = h.reshape(b, l, d)[:, l - (SSD_CONV - 1):].reshape(b * (SSD_CONV - 1), d)
                    rows = -(-tail.shape[0] // BF16_ROWS) * BF16_ROWS
                    tail = jnp.pad(tail, ((0, rows - tail.shape[0]), (0, 0)))
                    (xbc_tail,) = norm_linear(tail, g, ssd_wxbc, j, [(0, SSD_CONV_CH, F32)], rows)
                    conv_new.append(xbc_tail[:b * (SSD_CONV - 1)].reshape(b, SSD_CONV - 1, SSD_CONV_CH))
                y, s1 = ssd_core(zxbc, dt_raw.reshape(b, l, LANES), buf, s0, ssd_conv_w[j], ssd_conv_b[j],
                                 ssd_dt_bias[j], ssd_A_log[j], ssd_D[j], ssd_norm_g[j], ssd_c, act)
                ssm_new.append(s1)
                y, wmix = y.reshape(t, SSD_INNER), ssd_wo
            else:
                lam_init = 0.8 - 0.6 * math.exp(-0.3 * i)
                lams = (diff_lam_q1[j], diff_lam_k1[j], diff_lam_q2[j], diff_lam_k2[j])
                if sample:
                    q, kn, vn = norm_linear(h, g, diff_wi, j, [(0, d, F32), (d, 2 * d, F32), (2 * d, 3 * d, F32)], tm)
                    cache_kt = jnp.transpose(cache_k_diff[j], (0, 2, 3, 4, 1)).reshape(-1, d, PAGE_SIZE)
                    cache_v = cache_v_diff[j].reshape(-1, PAGE_SIZE * DIFF_HEADS, DIFF_DV)
                    y = diff_attention_sample(q.reshape(b, l, d), kn.reshape(b, l, d), vn.reshape(b, l, d), cache_kt,
                                              cache_v, page_table, rel_bias_table, lams, diff_subln_g[j], lam_init)
                    k_new.append(kn.reshape(b, l, DIFF_HEADS, 2, DIFF_DH))
                    v_new.append(vn.reshape(b, l, DIFF_HEADS, DIFF_DV))
                else:
                    w = diff_wi[j]
                    q, kt32, kt16, v32, v16 = diff_in_proj(h.reshape(b, l, d), g, w[:, :d], w[:, d:2 * d].T, w[:, 2 * d:],
                                                           tm, ATTN_TILE)
                    y = diff_attention_prompt(q, kt16, v16, rel_bias_table, lams, diff_subln_g[j], lam_init, ATTN_TILE)
                    k_new.append(jnp.transpose(kt32.reshape(b, DIFF_HEADS, 2, DIFF_DH, l), (0, 4, 1, 2, 3)))
                    v_new.append(v32.reshape(b, l, DIFF_HEADS, DIFF_DV))
                y, wmix = y.reshape(t, d), diff_wo
            h = out_ffn(y, wmix, j, h, norm_ffn_g[i], ffn_wi, ffn_wo, i, tm,
                        g_final=norm_final_g if i == DEPTH - 1 else None)
        return h.reshape(b, l, d), ret_new, jnp.stack(ssm_new), jnp.stack(conv_new), jnp.stack(k_new), jnp.stack(v_new)

    y_p, ret_p, ssm_p, conv_p, k_p, v_p = run_group(x_prompt, False)
    y_s, ret_s, ssm_s, conv_s, k_s, v_s = run_group(x_sample, True)
    return (y_p, y_s, ret_p, ssm_p, conv_p, k_p, v_p, ret_s, ssm_s, conv_s, k_s, v_s)
```

```python
import functools
import math

import jax
import jax.numpy as jnp
import numpy as np
from jax import lax
from jax.experimental import pallas as pl
from jax.experimental.pallas import tpu as pltpu

F32 = jnp.float32
BF16 = jnp.bfloat16

D_MODEL = 1024
DEPTH = 4
PAGE_SIZE = 128
EPS = 1e-6
RET_CHUNK = 256
SSD_CHUNK = 128
ATTN_TILE = 512

RET_HEADS = 4
RET_DK = D_MODEL // RET_HEADS
RET_DV = 2 * RET_DK
RET_VDIM = RET_HEADS * RET_DV
ROPE_BASE = 10000.0
ROPE_HALF = RET_DK // 2

SSD_INNER = 2 * D_MODEL
SSD_HEADDIM = 64
SSD_HEADS = SSD_INNER // SSD_HEADDIM
SSD_GROUPS = 8
SSD_HPG = SSD_HEADS // SSD_GROUPS
SSD_STATE = 128
SSD_CONV = 4
SSD_BC = SSD_GROUPS * SSD_STATE
SSD_CONV_CH = SSD_INNER + 2 * SSD_BC
SSD_GW = SSD_HPG * SSD_HEADDIM

DIFF_HEADS = 8
DIFF_DH = D_MODEL // DIFF_HEADS // 2
DIFF_DV = 2 * DIFF_DH
REL_BUCKETS = 32
REL_MAX_DIST = 128

FFN_HIDDEN = -(-8 * D_MODEL // (3 * 256)) * 256

V7X_VMEM_BYTES = 64 * 1024 * 1024
LANES = 128
SUBLANES = 8
BF16_ROWS = 16
VMEM_LIMIT = 56 * 1024 * 1024


def _cparams(sem):
    return pltpu.CompilerParams(dimension_semantics=sem, vmem_limit_bytes=VMEM_LIMIT)


def _dot(a, b):
    return jnp.dot(a, b, preferred_element_type=F32)


def _dot_nt(a, b):
    return lax.dot_general(a, b, (((1,), (1,)), ((), ())), preferred_element_type=F32)


def _dot_tn(a, b):
    return lax.dot_general(a, b, (((0,), (0,)), ((), ())), preferred_element_type=F32)


def _rms(x, g):
    return x * lax.rsqrt(jnp.mean(x * x, axis=-1, keepdims=True) + EPS) * g


def _silu(x):
    half = 0.5 * x
    return half * jnp.tanh(half) + half


def _resident(shape):
    return pl.BlockSpec(shape, lambda *_: (0,) * len(shape), pipeline_mode=pl.Buffered(1))


def _resident_layer(stacked, layer):
    tail = stacked.shape[1:]
    return pl.BlockSpec((None,) + tail, lambda *_: (layer,) + (0,) * len(tail), pipeline_mode=pl.Buffered(1))


def _norm_linear_kernel(x_ref, g_ref, w_ref, *o_refs, outs, tn):
    xn = _rms(x_ref[...], g_ref[...]).astype(BF16)
    n = w_ref.shape[1]
    for c0 in range(0, n, tn):
        acc = _dot(xn, w_ref[:, c0:c0 + tn])
        for o_ref, (lo, hi, _) in zip(o_refs, outs):
            if lo <= c0 and c0 + tn <= hi:
                o_ref[:, c0 - lo:c0 - lo + tn] = acc.astype(o_ref.dtype)


def norm_linear(x, g, w_stack, layer, outs, tm, tn=512):
    t, d = x.shape
    n = w_stack.shape[2]
    tn = min(tn, n)
    assert t % tm == 0 and n % tn == 0 and all(lo % tn == 0 and hi % tn == 0 for lo, hi, _ in outs)
    return pl.pallas_call(
        functools.partial(_norm_linear_kernel, outs=tuple(outs), tn=tn),
        grid=(t // tm,),
        in_specs=[pl.BlockSpec((tm, d), lambda i: (i, 0)), _resident((1, d)), _resident_layer(w_stack, layer)],
        out_specs=[pl.BlockSpec((tm, hi - lo), lambda i: (i, 0)) for lo, hi, _ in outs],
        out_shape=[jax.ShapeDtypeStruct((t, hi - lo), dt) for lo, hi, dt in outs],
        compiler_params=_cparams(("parallel",)),
        name="norm_linear",
    )(x, g.reshape(1, d), w_stack)


def _out_ffn_kernel(y_ref, wmix_ref, h_ref, g_ref, wgu_ref, wo_ref, *rest, th, final):
    o_ref = rest[-1]
    h = h_ref[...] + _dot(y_ref[...].astype(BF16), wmix_ref[...])
    xn = _rms(h, g_ref[...]).astype(BF16)
    hidden = wo_ref.shape[0]
    o_ref[...] = h
    for c0 in range(0, hidden, th):
        gate = _dot(xn, wgu_ref[:, c0:c0 + th])
        up = _dot(xn, wgu_ref[:, hidden + c0:hidden + c0 + th])
        act = (_silu(gate) * up).astype(BF16)
        o_ref[...] += _dot(act, wo_ref[c0:c0 + th, :])
    if final:
        o_ref[...] = _rms(o_ref[...], rest[0][...])


def out_ffn(y, wmix_stack, j, h, g, wgu_stack, wo_stack, i, tm, g_final=None, th=256):
    t, k = y.shape
    d = h.shape[1]
    hidden = wo_stack.shape[1]
    assert hidden % th == 0 and t % tm == 0
    final = g_final is not None
    extra = [g_final.reshape(1, d)] if final else []
    return pl.pallas_call(
        functools.partial(_out_ffn_kernel, th=th, final=final),
        grid=(t // tm,),
        in_specs=[pl.BlockSpec((tm, k), lambda m: (m, 0)), _resident_layer(wmix_stack, j),
                  pl.BlockSpec((tm, d), lambda m: (m, 0)), _resident((1, d)), _resident_layer(wgu_stack, i),
                  _resident_layer(wo_stack, i)] + [_resident((1, d))] * len(extra),
        out_specs=pl.BlockSpec((tm, d), lambda m: (m, 0)),
        out_shape=jax.ShapeDtypeStruct((t, d), F32),
        compiler_params=_cparams(("parallel",)),
        name="out_ffn",
    )(y, wmix_stack, h, g.reshape(1, d), wgu_stack, wo_stack, *extra)


def _rope_kernel(inv_ref, cos_ref, sin_ref, *, offset):
    rows = cos_ref.shape[0]
    pos = (offset + pl.program_id(0) * rows + lax.broadcasted_iota(jnp.int32, (rows, 1), 0)).astype(F32)
    ang = pos * inv_ref[...]
    cos_ref[...] = jnp.cos(ang)
    sin_ref[...] = jnp.sin(ang)


def rope_tables(offset, rows):
    inv = (1.0 / (ROPE_BASE ** jnp.linspace(0.0, 1.0, ROPE_HALF, dtype=F32))).reshape(1, ROPE_HALF)
    tr = min(rows, 512)
    return pl.pallas_call(
        functools.partial(_rope_kernel, offset=offset),
        grid=(rows // tr,),
        in_specs=[_resident((1, ROPE_HALF))],
        out_specs=[pl.BlockSpec((tr, ROPE_HALF), lambda i: (i, 0))] * 2,
        out_shape=[jax.ShapeDtypeStruct((rows, ROPE_HALF), F32)] * 2,
        compiler_params=_cparams(("parallel",)),
        name="rope_tables",
    )(inv)


def _ret_kernel(q_ref, k_ref, v_ref, g_ref, cos_ref, sin_ref, s0_ref, *rest, C, c_real, carry, slot, n_chunks):
    y_ref, st_ref = rest[carry:carry + 2]
    decay_sc = rest[carry + 2]
    stage = rest[carry + 3:]
    c = pl.program_id(1)
    log_gamma = [math.log(1.0 - 2.0 ** (-5.0 - hd)) for hd in range(RET_HEADS)]
    i = lax.broadcasted_iota(jnp.int32, (C, 1), 0).astype(F32)
    j = lax.broadcasted_iota(jnp.int32, (1, C), 1).astype(F32)

    @pl.when(c == 0)
    def _():
        st_ref[slot] = s0_ref[...]
        rel = i - j
        for hd in range(RET_HEADS):
            decay_sc[hd] = jnp.where(rel >= 0, jnp.exp(jnp.maximum(rel, 0.0) * log_gamma[hd]), 0.0)

    if c_real == C:
        q_src, k_src, v_src = q_ref.at[0], k_ref.at[0], v_ref.at[0]
    else:
        for ref, st in zip((q_ref, k_ref, v_ref), stage):
            st[...] = jnp.zeros(st.shape, F32)
            st[0:c_real, :] = ref[0].astype(F32)
        q_src, k_src, v_src = stage
    cos = cos_ref[...]
    sin = sin_ref[...]

    def rot(x):
        x1, x2 = x[:, :ROPE_HALF], x[:, ROPE_HALF:]
        return jnp.concatenate([x1 * cos - x2 * sin, x2 * cos + x1 * sin], axis=-1)

    for hd in range(RET_HEADS):
        lg = log_gamma[hd]
        ks = slice(hd * RET_DK, (hd + 1) * RET_DK)
        vs = slice(hd * RET_DV, (hd + 1) * RET_DV)
        qr = rot(q_src[:, ks].astype(F32)).astype(BF16)
        kr = rot(k_src[:, ks].astype(F32)) * (RET_DK ** -0.5)
        v = v_src[:, vs].astype(BF16)
        scores = _dot_nt(qr, kr.astype(BF16)) * decay_sc[hd]
        s_prev = st_ref[slot, 0, hd]
        o = _dot(scores.astype(BF16), v) + _dot(qr, s_prev.astype(BF16)) * jnp.exp((i + 1.0) * lg)
        kw = (kr * jnp.exp((c_real - 1.0 - i) * lg)).astype(BF16)
        st_ref[slot, 0, hd] = math.exp(c_real * lg) * s_prev + _dot_tn(kw, v)
        o = o * lax.rsqrt(jnp.mean(o * o, axis=-1, keepdims=True) + EPS)
        y_ref[0, :, vs] = (_silu(g_ref[0, :, vs].astype(F32)) * o[0:c_real]).astype(y_ref.dtype)

    if st_ref.shape[0] > 1:
        @pl.when(c == n_chunks - 1)
        def _():
            for other in range(st_ref.shape[0]):
                if other != slot:
                    st_ref[other] = st_ref[slot]


def retention_core(qkvg, cos, sin, s0_stack, j0, s1_stack, j1, n_layers, c_real, out_dtype):
    b, l, _ = qkvg.shape
    C = cos.shape[0] if c_real < RET_CHUNK else RET_CHUNK
    nc = l // c_real
    qk_w, v_w = RET_HEADS * RET_DK, RET_VDIM
    k_blk, v_blk, g_blk = 1, 2 * qk_w // v_w, 2 * qk_w // v_w + 1
    stage = [] if c_real == C else [pltpu.VMEM((C, qk_w), F32), pltpu.VMEM((C, qk_w), F32), pltpu.VMEM((C, v_w), F32)]
    carry = s1_stack is not None
    st_block, st_first, slot = (1, j1, 0) if carry else (n_layers, 0, j1)
    return pl.pallas_call(
        functools.partial(_ret_kernel, C=C, c_real=c_real, carry=int(carry), slot=slot, n_chunks=nc),
        grid=(b, nc),
        in_specs=[
            pl.BlockSpec((1, c_real, qk_w), lambda bi, c: (bi, c, 0)),
            pl.BlockSpec((1, c_real, qk_w), lambda bi, c: (bi, c, k_blk)),
            pl.BlockSpec((1, c_real, v_w), lambda bi, c: (bi, c, v_blk)),
            pl.BlockSpec((1, c_real, v_w), lambda bi, c: (bi, c, g_blk)),
            pl.BlockSpec((C, ROPE_HALF), lambda bi, c: (c, 0)),
            pl.BlockSpec((C, ROPE_HALF), lambda bi, c: (c, 0)),
            pl.BlockSpec((None, 1, RET_HEADS, RET_DK, RET_DV), lambda bi, c: (j0, bi, 0, 0, 0)),
        ] + ([pl.BlockSpec(memory_space=pl.ANY)] if carry else []),
        out_specs=[
            pl.BlockSpec((1, c_real, v_w), lambda bi, c: (bi, c, 0)),
            pl.BlockSpec((st_block, 1, RET_HEADS, RET_DK, RET_DV), lambda bi, c: (st_first, bi, 0, 0, 0)),
        ],
        out_shape=[jax.ShapeDtypeStruct((b, l, RET_VDIM), out_dtype),
                   jax.ShapeDtypeStruct((n_layers, b, RET_HEADS, RET_DK, RET_DV), F32)],
        scratch_shapes=[pltpu.VMEM((RET_HEADS, C, C), F32)] + stage,
        input_output_aliases={7: 1} if carry else {},
        compiler_params=_cparams(("parallel", "arbitrary")),
        name="retention_core",
    )(qkvg, qkvg, qkvg, qkvg, cos, sin, s0_stack, *([s1_stack] if carry else []))


def _softplus(x):
    return jnp.maximum(x, 0.0) + jnp.log1p(jnp.exp(-jnp.abs(x)))


def _cumsum(x, axis):
    n = x.shape[axis]
    idx = lax.broadcasted_iota(jnp.int32, x.shape, axis)
    s = 1
    while s < n:
        x = x + jnp.where(idx >= s, pltpu.roll(x, s, axis), 0.0)
        s *= 2
    return x


def _ssd_kernel(z_ref, x_ref, bc_ref, dt_ref, dtt_ref, cw_ref, cb_ref, dtb_ref, dtbt_ref, al_ref, alt_ref,
                dsk_ref, ng_ref, buf_ref, s0_ref, y_ref, s1_ref, extx, extbc, xs_ref, bcs_ref, *, C, c_real):
    c = pl.program_id(1)
    halo = SUBLANES

    @pl.when(c == 0)
    def _():
        s1_ref[...] = s0_ref[...]
        extx[...] = jnp.zeros(extx.shape, F32)
        extbc[...] = jnp.zeros(extbc.shape, F32)
        extx[0:halo, :] = buf_ref[0, :, 0:SSD_INNER]
        extbc[0:halo, :] = buf_ref[0, :, SSD_INNER:SSD_CONV_CH]

    extx[halo:halo + c_real, :] = x_ref[0].astype(F32)
    extbc[halo:halo + c_real, :] = bc_ref[0].astype(F32)

    cw = 512
    first = halo - (SSD_CONV - 1)
    for ext, dst, ch0 in ((extx, xs_ref, 0), (extbc, bcs_ref, SSD_INNER)):
        for col in range(0, SSD_INNER, cw):
            acc = cb_ref[:, ch0 + col:ch0 + col + cw]
            for tap in range(SSD_CONV):
                acc = acc + ext[first + tap:first + tap + C, col:col + cw] * cw_ref[tap:tap + 1, ch0 + col:ch0 + col + cw]
            dst[:, col:col + cw] = _silu(acc).astype(dst.dtype)
    extx[0:halo, :] = extx[C:C + halo, :]
    extbc[0:halo, :] = extbc[C:C + halo, :]

    row = lax.broadcasted_iota(jnp.int32, (C, 1), 0)
    col_i = lax.broadcasted_iota(jnp.int32, (1, C), 1)
    dt = jnp.where(row < c_real, _softplus(dt_ref[0, 0] + dtb_ref[...]), 0.0)
    dtt = jnp.where(col_i < c_real, _softplus(dtt_ref[0, 0] + dtbt_ref[...]), 0.0)
    cs = _cumsum(dt * (-jnp.exp(al_ref[...])), 0)
    cst = _cumsum(dtt * (-jnp.exp(alt_ref[...])), 1)
    causal = row >= col_i
    lane_r = lax.broadcasted_iota(jnp.int32, (1, SSD_GW), 1) // SSD_HEADDIM
    sub_r = lax.broadcasted_iota(jnp.int32, (SSD_GW, 1), 0) // SSD_HEADDIM

    def by_head(vals, sel):
        out = vals[SSD_HPG - 1]
        for r in range(SSD_HPG - 2, -1, -1):
            out = jnp.where(sel == r, vals[r], out)
        return out

    for g in range(SSD_GROUPS):
        bm = bcs_ref[:, g * SSD_STATE:(g + 1) * SSD_STATE]
        cm = bcs_ref[:, SSD_BC + g * SSD_STATE:SSD_BC + (g + 1) * SSD_STATE]
        cb = _dot_nt(cm, bm)
        h0 = s1_ref[0, g * SSD_HPG:(g + 1) * SSD_HPG].reshape(SSD_GW, SSD_STATE)
        xg = xs_ref[:, g * SSD_GW:(g + 1) * SSD_GW]
        heads = range(g * SSD_HPG, (g + 1) * SSD_HPG)
        cs_cols = [cs[:, hh:hh + 1] for hh in heads]
        cs_last = [cs[C - 1:C, hh:hh + 1] for hh in heads]
        yg = _dot_nt(cm, h0.astype(BF16)) * by_head([jnp.exp(cc) for cc in cs_cols], lane_r)
        for r, hh in enumerate(heads):
            seg = cs_cols[r] - cst[hh:hh + 1, :]
            w = cb * jnp.exp(jnp.where(causal, seg, -jnp.inf)) * dtt[hh:hh + 1, :]
            yg = yg + _dot(w.astype(BF16), jnp.where(lane_r == r, xg, 0.0).astype(BF16))
        w_end = by_head([jnp.exp(cl - cc) * dt[:, hh:hh + 1] for cl, cc, hh in zip(cs_last, cs_cols, heads)], lane_r)
        h1 = by_head([jnp.exp(cl) for cl in cs_last], sub_r) * h0 + _dot_tn((xg * w_end).astype(BF16), bm)
        s1_ref[0, g * SSD_HPG:(g + 1) * SSD_HPG] = h1.reshape(SSD_HPG, SSD_HEADDIM, SSD_STATE)
        sl = slice(g * SSD_GW, (g + 1) * SSD_GW)
        yo = (yg[0:c_real] + dsk_ref[:, sl] * xg[0:c_real]) * _silu(z_ref[0, :, sl].astype(F32))
        yo = yo * lax.rsqrt(jnp.mean(yo * yo, axis=-1, keepdims=True) + EPS) * ng_ref[:, sl]
        y_ref[0, :, sl] = yo.astype(y_ref.dtype)


def ssd_core(zxbc, dt_raw, conv_buf, s0, conv_w, conv_b, dt_bias, a_log, d_skip, norm_g, c_real, out_dtype):
    b, l, _ = zxbc.shape
    C = SSD_CHUNK
    nc = l // c_real
    halo = SUBLANES
    dt4 = dt_raw.reshape(b, nc, c_real, LANES)
    dtt = jnp.swapaxes(dt4[..., :SSD_HEADS], -1, -2)
    if c_real < C:
        dt4 = jnp.pad(dt4, ((0, 0), (0, 0), (0, C - c_real), (0, 0)))
        dtt = jnp.pad(dtt, ((0, 0), (0, 0), (0, 0), (0, C - c_real)))
    lane_pad = LANES - SSD_HEADS
    buf8 = jnp.pad(conv_buf.astype(F32), ((0, 0), (halo - (SSD_CONV - 1), 0), (0, 0)))
    small = [
        conv_w.astype(F32), conv_b.reshape(1, -1).astype(F32),
        jnp.pad(dt_bias.astype(F32), (0, lane_pad)).reshape(1, LANES), dt_bias.astype(F32).reshape(SSD_HEADS, 1),
        jnp.pad(a_log.astype(F32), (0, lane_pad)).reshape(1, LANES), a_log.astype(F32).reshape(SSD_HEADS, 1),
        jnp.repeat(d_skip.astype(F32), SSD_HEADDIM).reshape(1, SSD_INNER), norm_g.astype(F32).reshape(1, SSD_INNER),
    ]
    w = SSD_INNER
    return pl.pallas_call(
        functools.partial(_ssd_kernel, C=C, c_real=c_real),
        grid=(b, nc),
        in_specs=[
            pl.BlockSpec((1, c_real, w), lambda bi, c: (bi, c, 0)),
            pl.BlockSpec((1, c_real, w), lambda bi, c: (bi, c, 1)),
            pl.BlockSpec((1, c_real, w), lambda bi, c: (bi, c, 2)),
            pl.BlockSpec((1, 1, C, LANES), lambda bi, c: (bi, c, 0, 0)),
            pl.BlockSpec((1, 1, SSD_HEADS, C), lambda bi, c: (bi, c, 0, 0)),
        ] + [_resident(a.shape) for a in small] + [
            pl.BlockSpec((1, halo, SSD_CONV_CH), lambda bi, c: (bi, 0, 0)),
            pl.BlockSpec((1, SSD_HEADS, SSD_HEADDIM, SSD_STATE), lambda bi, c: (bi, 0, 0, 0)),
        ],
        out_specs=[
            pl.BlockSpec((1, c_real, w), lambda bi, c: (bi, c, 0)),
            pl.BlockSpec((1, SSD_HEADS, SSD_HEADDIM, SSD_STATE), lambda bi, c: (bi, 0, 0, 0)),
        ],
        out_shape=[jax.ShapeDtypeStruct((b, l, SSD_INNER), out_dtype),
                   jax.ShapeDtypeStruct((b, SSD_HEADS, SSD_HEADDIM, SSD_STATE), F32)],
        scratch_shapes=[pltpu.VMEM((C + halo, w), F32), pltpu.VMEM((C + halo, w), F32),
                        pltpu.VMEM((C, w), F32), pltpu.VMEM((C, w), BF16)],
        compiler_params=_cparams(("parallel", "arbitrary")),
        name="ssd_core",
    )(zxbc, zxbc, zxbc, dt4, dtt, *small, buf8, s0)


def _t5_bias(dist, table_at):
    n = jnp.maximum(dist, 0)
    exact = REL_BUCKETS // 2
    nf = jnp.maximum(n, 1).astype(F32)
    large = exact + (jnp.log(nf / exact) / math.log(REL_MAX_DIST / exact) * (REL_BUCKETS - exact)).astype(jnp.int32)
    bucket = jnp.where(n < exact, n, jnp.minimum(large, REL_BUCKETS - 1))
    bias = jnp.zeros(dist.shape, F32)
    for bkt in range(REL_BUCKETS):
        bias = jnp.where(bucket == bkt, table_at(bkt), bias)
    return bias


def _lambda(lq1, lk1, lq2, lk2, lam_init):
    s1 = jnp.sum(lq1[...] * lk1[...], axis=-1, keepdims=True)
    s2 = jnp.sum(lq2[...] * lk2[...], axis=-1, keepdims=True)
    return jnp.exp(s1) - jnp.exp(s2) + lam_init


def _diff_in_proj_kernel(x_ref, g_ref, wq_ref, wkt_ref, wv_ref, q_ref, kt32_ref, kt16_ref, v32_ref, v16_ref, *, tn, kb):
    xn = _rms(x_ref[0], g_ref[...]).astype(BF16)
    d = wq_ref.shape[1]
    tm = xn.shape[0]
    for c0 in range(0, d, tn):
        q_ref[0, :, c0:c0 + tn] = (_dot(xn, wq_ref[:, c0:c0 + tn]) * (DIFF_DH ** -0.5)).astype(BF16)
        v = _dot(xn, wv_ref[:, c0:c0 + tn])
        v32_ref[0, :, c0:c0 + tn] = v
        v16_ref[0, :, c0:c0 + tn] = v.astype(BF16)
        kt = _dot_nt(wkt_ref[c0:c0 + tn, :], xn)
        kt32_ref[0, c0:c0 + tn, :] = kt
        for s in range(tm // kb):
            kt16_ref[0, s, c0:c0 + tn, :] = kt[:, s * kb:(s + 1) * kb].astype(BF16)


def diff_in_proj(x, g, wq, wkt, wv, tm, kb):
    b, l, d = x.shape
    assert l % tm == 0 and tm % kb == 0
    tok = lambda: pl.BlockSpec((1, tm, d), lambda bi, i: (bi, i, 0))
    return pl.pallas_call(
        functools.partial(_diff_in_proj_kernel, tn=512, kb=kb),
        grid=(b, l // tm),
        in_specs=[tok(), _resident((1, d)), _resident((d, d)), _resident((d, d)), _resident((d, d))],
        out_specs=[tok(), pl.BlockSpec((1, d, tm), lambda bi, i: (bi, 0, i)),
                   pl.BlockSpec((1, tm // kb, d, kb), lambda bi, i: (bi, i, 0, 0)), tok(), tok()],
        out_shape=[jax.ShapeDtypeStruct((b, l, d), BF16), jax.ShapeDtypeStruct((b, d, l), F32),
                   jax.ShapeDtypeStruct((b, l // kb, d, kb), BF16), jax.ShapeDtypeStruct((b, l, d), F32),
                   jax.ShapeDtypeStruct((b, l, d), BF16)],
        compiler_params=_cparams(("parallel", "parallel")),
        name="diff_in_proj",
    )(x, g.reshape(1, d), wq, wkt, wv)


def _attn_prompt_kernel(tbl_ref, q_ref, kt_ref, v_ref, lq1, lk1, lq2, lk2, sg_ref, o_ref,
                        bias_sc, m_sc, a_sc, *, T, lam_init):
    hd = pl.program_id(0)
    bi = pl.program_id(1)
    qi = pl.program_id(2)

    @pl.when((bi == 0) & (qi == 0))
    def _():
        i = lax.broadcasted_iota(jnp.int32, (T, 2 * T), 0)
        j = lax.broadcasted_iota(jnp.int32, (T, 2 * T), 1)
        dist = i - j + T
        bias_sc[...] = jnp.where(dist >= 0, _t5_bias(dist, lambda bkt: tbl_ref[bkt, hd]), -jnp.inf)

    lane = lax.broadcasted_iota(jnp.int32, (1, DIFF_DV), 1)
    q = q_ref[0]
    zero = jnp.zeros((), q.dtype)
    q2 = jnp.concatenate([jnp.where(lane < DIFF_DH, q, zero), jnp.where(lane >= DIFF_DH, q, zero)], axis=0)
    m_sc[...] = jnp.full(m_sc.shape, -jnp.inf, F32)
    a_sc[...] = jnp.zeros(a_sc.shape, F32)

    def update(kb, bias):
        vt = v_ref[0, pl.ds(pl.multiple_of(kb * T, T), T), :]
        vx = jnp.concatenate([vt, jnp.ones((T, DIFF_DV), BF16)], axis=1)
        s = _dot(q2, kt_ref[0, kb])
        s = s + (jnp.concatenate([bias, bias], axis=0) if getattr(bias, "ndim", 0) == 2 else bias)
        m_prev = m_sc[...]
        m_new = jnp.maximum(m_prev, jnp.max(s, axis=-1, keepdims=True))
        alpha = jnp.exp(m_prev - m_new)
        p = jnp.concatenate([jnp.exp(s[:, c:c + LANES] - m_new) for c in range(0, T, LANES)], axis=1)
        a_sc[...] = jnp.concatenate([alpha, alpha], axis=1) * a_sc[...] + _dot(p.astype(BF16), vx)
        m_sc[...] = m_new

    far_bias = tbl_ref[REL_BUCKETS - 1, hd]

    def far_body(kb, carry):
        update(kb, far_bias)
        return carry

    lax.fori_loop(0, jnp.maximum(qi - 1, 0), far_body, 0)

    @pl.when(qi >= 1)
    def _():
        update(qi - 1, bias_sc[:, 0:T])

    update(qi, bias_sc[:, T:2 * T])

    lam = _lambda(lq1, lk1, lq2, lk2, lam_init)
    o = a_sc[:T, :DIFF_DV] / a_sc[:T, DIFF_DV:] - lam * (a_sc[T:, :DIFF_DV] / a_sc[T:, DIFF_DV:])
    o_ref[0] = (_rms(o, sg_ref[...]) * (1.0 - lam_init)).astype(o_ref.dtype)


def diff_attention_prompt(q, kt, v, rel_table, lams, subln_g, lam_init, T):
    b, l, _ = q.shape
    assert T >= REL_MAX_DIST and l % T == 0 and DIFF_DV == 2 * DIFF_DH
    vec = lambda a: a.astype(F32).reshape(1, -1)
    return pl.pallas_call(
        functools.partial(_attn_prompt_kernel, T=T, lam_init=lam_init),
        grid=(DIFF_HEADS, b, l // T),
        in_specs=[
            pl.BlockSpec(memory_space=pltpu.SMEM),
            pl.BlockSpec((1, T, DIFF_DV), lambda h, bi, qi: (bi, qi, h)),
            pl.BlockSpec((1, l // T, DIFF_DV, T), lambda h, bi, qi: (bi, 0, h, 0)),
            pl.BlockSpec((1, l, DIFF_DV), lambda h, bi, qi: (bi, 0, h)),
        ] + [_resident((1, DIFF_DH))] * 4 + [_resident((1, DIFF_DV))],
        out_specs=pl.BlockSpec((1, T, DIFF_DV), lambda h, bi, qi: (bi, qi, h)),
        out_shape=jax.ShapeDtypeStruct((b, l, DIFF_HEADS * DIFF_DV), BF16),
        scratch_shapes=[pltpu.VMEM((T, 2 * T), F32), pltpu.VMEM((2 * T, LANES), F32), pltpu.VMEM((2 * T, 2 * DIFF_DV), F32)],
        compiler_params=_cparams(("arbitrary", "arbitrary", "arbitrary")),
        name="diff_attention_prompt",
    )(rel_table.astype(F32), q, kt, v, *[vec(a) for a in lams], vec(subln_g))


DEC_RPH = SUBLANES


def _attn_sample_kernel(pt_ref, q_ref, kn_ref, vn_ref, tb_ref, lq1, lk1, lq2, lk2, sg_ref, *rest,
                        pps, n_pages, lq, lam_init):
    kt_refs, v_refs = rest[:pps], rest[pps:2 * pps]
    o_ref, qm_sc, m_sc, l_sc, acc_sc = rest[2 * pps:]
    s = pl.program_id(1)
    n_steps = n_pages // pps
    past = n_pages * PAGE_SIZE
    rows = DIFF_HEADS * DEC_RPH
    width = DIFF_HEADS * DIFF_DV
    rid = lax.broadcasted_iota(jnp.int32, (rows, 1), 0)
    row_tok = (rid % DEC_RPH) // 2

    def q_rows():
        r8 = lax.broadcasted_iota(jnp.int32, (DEC_RPH, width), 0)
        lane_pair = lax.broadcasted_iota(jnp.int32, (DEC_RPH, width), 1) // DIFF_DH
        qrep = jnp.zeros((DEC_RPH, width), F32)
        for t in range(lq):
            qrep = jnp.where(r8 // 2 == t, q_ref[0, t:t + 1, :], qrep)
        qrep = qrep * (DIFF_DH ** -0.5)
        return jnp.concatenate([jnp.where(lane_pair == 2 * hh + r8 % 2, qrep, 0.0) for hh in range(DIFF_HEADS)], axis=0)

    @pl.when(s == 0)
    def _():
        qm_sc[...] = q_rows().astype(BF16)
        m_sc[...] = jnp.full(m_sc.shape, -jnp.inf, F32)
        l_sc[...] = jnp.zeros(l_sc.shape, F32)
        acc_sc[...] = jnp.zeros(acc_sc.shape, F32)

    far_bias = tb_ref[:, REL_BUCKETS - 1:REL_BUCKETS]

    def pages_update(last_near):
        qm = qm_sc[...]
        sc = []
        for i in range(pps):
            sci = _dot(qm, kt_refs[i][0].astype(BF16))
            if last_near and i == pps - 1:
                kpos = (n_pages - 1) * PAGE_SIZE + lax.broadcasted_iota(jnp.int32, (1, PAGE_SIZE), 1)
                sc.append(sci + _t5_bias((past + row_tok) - kpos, lambda bkt: tb_ref[:, bkt:bkt + 1]))
            else:
                sc.append(sci + far_bias)
        m_prev = m_sc[...]
        m_new = m_prev
        for sci in sc:
            m_new = jnp.maximum(m_new, jnp.max(sci, axis=-1, keepdims=True))
        alpha = jnp.exp(m_prev - m_new)
        p = [jnp.exp(sci - m_new) for sci in sc]
        l_new = alpha * l_sc[...]
        for pi in p:
            l_new = l_new + jnp.sum(pi, axis=-1, keepdims=True)
        l_sc[...] = l_new
        m_sc[...] = m_new
        for hh in range(DIFF_HEADS):
            sl = slice(hh * DEC_RPH, (hh + 1) * DEC_RPH)
            ph = jnp.concatenate([pi[sl] for pi in p], axis=1).astype(BF16)
            vh = jnp.concatenate([v_refs[i][0, pl.ds(hh, PAGE_SIZE, stride=DIFF_HEADS), :].astype(BF16)
                                  for i in range(pps)], axis=0)
            acc_sc[sl, :] = alpha[sl] * acc_sc[sl, :] + _dot(ph, vh)

    @pl.when(s < n_steps - 1)
    def _():
        pages_update(False)

    @pl.when(s == n_steps - 1)
    def _():
        pages_update(True)

    @pl.when(s == n_steps)
    def _():
        qf = q_rows()
        sj = []
        for jn in range(lq):
            dist = row_tok - jn
            sc = jnp.sum(qf * kn_ref[0, jn:jn + 1, :], axis=-1, keepdims=True)
            sc = sc + _t5_bias(dist, lambda bkt: tb_ref[:, bkt:bkt + 1])
            sj.append(jnp.where(dist >= 0, sc, -jnp.inf))
        m_prev = m_sc[...]
        m_new = m_prev
        for sc in sj:
            m_new = jnp.maximum(m_new, sc)
        alpha = jnp.exp(m_prev - m_new)
        l_new = alpha * l_sc[...]
        acc = alpha * acc_sc[...]
        for jn, sc in enumerate(sj):
            p = jnp.exp(sc - m_new)
            l_new = l_new + p
            vrow = jnp.concatenate(
                [jnp.broadcast_to(vn_ref[0, jn:jn + 1, hh * DIFF_DV:(hh + 1) * DIFF_DV], (DEC_RPH, DIFF_DV))
                 for hh in range(DIFF_HEADS)], axis=0)
            acc = acc + p * vrow

        lam = _lambda(lq1, lk1, lq2, lk2, lam_init)
        coef = jnp.where(rid % 2 == 0, 1.0, -lam) / l_new
        a = acc * coef
        a = a + pltpu.roll(a, rows - 1, 0)
        res = _rms(a, sg_ref[...]) * (1.0 - lam_init)
        for hh in range(DIFF_HEADS):
            for t in range(lq):
                r = hh * DEC_RPH + 2 * t
                o_ref[0, t:t + 1, hh * DIFF_DV:(hh + 1) * DIFF_DV] = res[r:r + 1, :]


def diff_attention_sample(q, k_new, v_new, cache_kt, cache_v, page_table, rel_table, lams, subln_g, lam_init, pps=8):
    b, lq, width = q.shape
    n_pages = page_table.shape[1]
    assert n_pages % pps == 0 and 2 * lq <= DEC_RPH and n_pages // pps >= 1
    n_steps = n_pages // pps
    rows = DIFF_HEADS * DEC_RPH
    head_of_row = np.arange(rows) // DEC_RPH
    tb = jnp.pad(rel_table.astype(F32).T[head_of_row], ((0, 0), (0, LANES - REL_BUCKETS)))
    vec = lambda a: a.astype(F32).reshape(1, -1)

    def page_map(i):
        return lambda bi, s, pt: (pt[bi * n_pages + jnp.minimum(s, n_steps - 1) * pps + i], 0, 0)

    tok = pl.BlockSpec((1, lq, width), lambda bi, s, pt: (bi, 0, 0))
    const = lambda shape: pl.BlockSpec(shape, lambda bi, s, pt: (0,) * len(shape))
    kpage = [pl.BlockSpec((1, width, PAGE_SIZE), page_map(i)) for i in range(pps)]
    vpage = [pl.BlockSpec((1, PAGE_SIZE * DIFF_HEADS, DIFF_DV), page_map(i)) for i in range(pps)]
    grid_spec = pltpu.PrefetchScalarGridSpec(
        num_scalar_prefetch=1,
        grid=(b, n_steps + 1),
        in_specs=[tok, tok, tok, const((rows, LANES))] + [const((1, DIFF_DH))] * 4 + [const((1, DIFF_DV))] + kpage + vpage,
        out_specs=tok,
        scratch_shapes=[pltpu.VMEM((rows, width), BF16), pltpu.VMEM((rows, 1), F32), pltpu.VMEM((rows, 1), F32),
                        pltpu.VMEM((rows, DIFF_DV), F32)],
    )
    return pl.pallas_call(
        functools.partial(_attn_sample_kernel, pps=pps, n_pages=n_pages, lq=lq, lam_init=lam_init),
        grid_spec=grid_spec,
        out_shape=jax.ShapeDtypeStruct((b, lq, width), F32),
        compiler_params=_cparams(("parallel", "arbitrary")),
        name="diff_attention_sample",
    )(page_table.reshape(-1).astype(jnp.int32), q, k_new, v_new, tb, *[vec(a) for a in lams], vec(subln_g),
      *([cache_kt] * pps), *([cache_v] * pps))


def kernel(x_prompt, x_sample, state_ret, state_ssm, state_conv, cache_k_diff, cache_v_diff, page_table, norm_mix_g, norm_ffn_g, norm_final_g, ret_w_in, ret_w_out, ssd_w_in, ssd_conv_w, ssd_conv_b, ssd_dt_bias, ssd_A_log, ssd_D, ssd_norm_g, ssd_w_out, diff_w_in, diff_lam_q1, diff_lam_k1, diff_lam_q2, diff_lam_k2, diff_subln_g, diff_w_out, rel_bias_table, ffn_w_in, ffn_w_out):
    kinds = tuple(i % 3 for i in range(DEPTH))
    n_pages = page_table.shape[1]
    bf = lambda a: a.astype(BF16)
    ret_wi, ret_wo = bf(ret_w_in), bf(ret_w_out)
    ssd_wmain = bf(ssd_w_in[:, :, :SSD_INNER + SSD_CONV_CH])
    ssd_wxbc = ssd_wmain[:, :, SSD_INNER:]
    ssd_wdt = bf(jnp.pad(ssd_w_in[:, :, SSD_INNER + SSD_CONV_CH:], ((0, 0), (0, 0), (0, LANES - SSD_HEADS))))
    ssd_wo = bf(ssd_w_out)
    diff_wi, diff_wo = bf(diff_w_in), bf(diff_w_out)
    ffn_wi, ffn_wo = bf(ffn_w_in), bf(ffn_w_out)

    n_ret = kinds.count(0)

    def run_group(x, sample):
        b, l, d = x.shape
        t = b * l
        tm = 512 if t % 512 == 0 else t
        ret_c = RET_CHUNK if l % RET_CHUNK == 0 else l
        ssd_c = SSD_CHUNK if l % SSD_CHUNK == 0 else l
        act = F32 if sample else BF16
        offset = n_pages * PAGE_SIZE if sample else 0
        rope_rows = l if l % RET_CHUNK == 0 else -(-l // BF16_ROWS) * BF16_ROWS
        cos, sin = rope_tables(offset, rope_rows)
        ret_new, ssm_new, conv_new, k_new, v_new = None, [], [], [], []
        h = x.reshape(t, d)
        for i in range(DEPTH):
            kind = kinds[i]
            j = kinds[:i].count(kind)
            g = norm_mix_g[i]
            if kind == 0:
                n_in = ret_wi.shape[2]
                (qkvg,) = norm_linear(h, g, ret_wi, j, [(0, n_in, act)], tm)
                s0, j0 = (state_ret, j) if sample else (jnp.zeros((1, b, RET_HEADS, RET_DK, RET_DV), F32), 0)
                y, ret_new = retention_core(qkvg.reshape(b, l, n_in), cos, sin, s0, j0, ret_new, j, n_ret, ret_c, act)
                y, wmix = y.reshape(t, RET_VDIM), ret_wo
            elif kind == 1:
                n_in = SSD_INNER + SSD_CONV_CH
                (zxbc,) = norm_linear(h, g, ssd_wmain, j, [(0, n_in, act)], tm)
                (dt_raw,) = norm_linear(h, g, ssd_wdt, j, [(0, LANES, F32)], tm)
                zxbc = zxbc.reshape(b, l, n_in)
                if sample:
                    buf, s0 = state_conv[j], state_ssm[j]
                    conv_new.append(zxbc[:, l - (SSD_CONV - 1):, SSD_INNER:])
                else:
                    buf = jnp.zeros((b, SSD_CONV - 1, SSD_CONV_CH), F32)
                    s0 = jnp.zeros((b, SSD_HEADS, SSD_HEADDIM, SSD_STATE), F32)
                    tail = h.reshape(b, l, d)[:, l - (SSD_CONV - 1):].reshape(b * (SSD_CONV - 1), d)
                    rows = -(-tail.shape[0] // BF16_ROWS) * BF16_ROWS
                    tail = jnp.pad(tail, ((0, rows - tail.shape[0]), (0, 0)))
                    (xbc_tail,) = norm_linear(tail, g, ssd_wxbc, j, [(0, SSD_CONV_CH, F32)], rows)
                    conv_new.append(xbc_tail[:b * (SSD_CONV - 1)].reshape(b, SSD_CONV - 1, SSD_CONV_CH))
                y, s1 = ssd_core(zxbc, dt_raw.reshape(b, l, LANES), buf, s0, ssd_conv_w[j], ssd_conv_b[j],
                                 ssd_dt_bias[j], ssd_A_log[j], ssd_D[j], ssd_norm_g[j], ssd_c, act)
                ssm_new.append(s1)
                y, wmix = y.reshape(t, SSD_INNER), ssd_wo
            else:
                lam_init = 0.8 - 0.6 * math.exp(-0.3 * i)
                lams = (diff_lam_q1[j], diff_lam_k1[j], diff_lam_q2[j], diff_lam_k2[j])
                if sample:
                    q, kn, vn = norm_linear(h, g, diff_wi, j, [(0, d, F32), (d, 2 * d, F32), (2 * d, 3 * d, F32)], tm)
                    cache_kt = jnp.transpose(cache_k_diff[j], (0, 2, 3, 4, 1)).reshape(-1, d, PAGE_SIZE)
                    cache_v = cache_v_diff[j].reshape(-1, PAGE_SIZE * DIFF_HEADS, DIFF_DV)
                    y = diff_attention_sample(q.reshape(b, l, d), kn.reshape(b, l, d), vn.reshape(b, l, d), cache_kt,
                                              cache_v, page_table, rel_bias_table, lams, diff_subln_g[j], lam_init)
                    k_new.append(kn.reshape(b, l, DIFF_HEADS, 2, DIFF_DH))
                    v_new.append(vn.reshape(b, l, DIFF_HEADS, DIFF_DV))
                else:
                    w = diff_wi[j]
                    q, kt32, kt16, v32, v16 = diff_in_proj(h.reshape(b, l, d), g, w[:, :d], w[:, d:2 * d].T, w[:, 2 * d:],
                                                           tm, ATTN_TILE)
                    y = diff_attention_prompt(q, kt16, v16, rel_bias_table, lams, diff_subln_g[j], lam_init, ATTN_TILE)
                    k_new.append(jnp.transpose(kt32.reshape(b, DIFF_HEADS, 2, DIFF_DH, l), (0, 4, 1, 2, 3)))
                    v_new.append(v32.reshape(b, l, DIFF_HEADS, DIFF_DV))
                y, wmix = y.reshape(t, d), diff_wo
            h = out_ffn(y, wmix, j, h, norm_ffn_g[i], ffn_wi, ffn_wo, i, tm,
                        g_final=norm_final_g if i == DEPTH - 1 else None)
        return h.reshape(b, l, d), ret_new, jnp.stack(ssm_new), jnp.stack(conv_new), jnp.stack(k_new), jnp.stack(v_new)

    y_p, ret_p, ssm_p, conv_p, k_p, v_p = run_group(x_prompt, False)
    y_s, ret_s, ssm_s, conv_s, k_s, v_s = run_group(x_sample, True)
    return (y_p, y_s, ret_p, ssm_p, conv_p, k_p, v_p, ret_s, ssm_s, conv_s, k_s, v_s)
```

```python
import functools
import math

import jax
import jax.numpy as jnp
import numpy as np
from jax import lax
from jax.experimental import pallas as pl
from jax.experimental.pallas import tpu as pltpu

F32 = jnp.float32
BF16 = jnp.bfloat16

D_MODEL = 1024
DEPTH = 4
PAGE_SIZE = 128
EPS = 1e-6
RET_CHUNK = 256
SSD_CHUNK = 128
ATTN_TILE = 512

RET_HEADS = 4
RET_DK = D_MODEL // RET_HEADS
RET_DV = 2 * RET_DK
RET_VDIM = RET_HEADS * RET_DV
ROPE_BASE = 10000.0
ROPE_HALF = RET_DK // 2

SSD_INNER = 2 * D_MODEL
SSD_HEADDIM = 64
SSD_HEADS = SSD_INNER // SSD_HEADDIM
SSD_GROUPS = 8
SSD_HPG = SSD_HEADS // SSD_GROUPS
SSD_STATE = 128
SSD_CONV = 4
SSD_BC = SSD_GROUPS * SSD_STATE
SSD_CONV_CH = SSD_INNER + 2 * SSD_BC
SSD_GW = SSD_HPG * SSD_HEADDIM

DIFF_HEADS = 8
DIFF_DH = D_MODEL // DIFF_HEADS // 2
DIFF_DV = 2 * DIFF_DH
REL_BUCKETS = 32
REL_MAX_DIST = 128

FFN_HIDDEN = -(-8 * D_MODEL // (3 * 256)) * 256

V7X_VMEM_BYTES = 64 * 1024 * 1024
LANES = 128
SUBLANES = 8
BF16_ROWS = 16
VMEM_LIMIT = 56 * 1024 * 1024


def _cparams(sem):
    return pltpu.CompilerParams(dimension_semantics=sem, vmem_limit_bytes=VMEM_LIMIT)


def _dot(a, b):
    return jnp.dot(a, b, preferred_element_type=F32)


def _dot_nt(a, b):
    return lax.dot_general(a, b, (((1,), (1,)), ((), ())), preferred_element_type=F32)


def _dot_tn(a, b):
    return lax.dot_general(a, b, (((0,), (0,)), ((), ())), preferred_element_type=F32)


def _rms(x, g):
    return x * lax.rsqrt(jnp.mean(x * x, axis=-1, keepdims=True) + EPS) * g


def _silu(x):
    half = 0.5 * x
    return half * jnp.tanh(half) + half


def _resident(shape):
    return pl.BlockSpec(shape, lambda *_: (0,) * len(shape), pipeline_mode=pl.Buffered(1))


def _resident_layer(stacked, layer):
    tail = stacked.shape[1:]
    return pl.BlockSpec((None,) + tail, lambda *_: (layer,) + (0,) * len(tail), pipeline_mode=pl.Buffered(1))


def _norm_linear_kernel(x_ref, g_ref, w_ref, *o_refs, outs, tn):
    xn = _rms(x_ref[...], g_ref[...]).astype(BF16)
    n = w_ref.shape[1]
    for c0 in range(0, n, tn):
        acc = _dot(xn, w_ref[:, c0:c0 + tn])
        for o_ref, (lo, hi, _) in zip(o_refs, outs):
            if lo <= c0 and c0 + tn <= hi:
                o_ref[:, c0 - lo:c0 - lo + tn] = acc.astype(o_ref.dtype)


def norm_linear(x, g, w_stack, layer, outs, tm, tn=512):
    t, d = x.shape
    n = w_stack.shape[2]
    tn = min(tn, n)
    assert t % tm == 0 and n % tn == 0 and all(lo % tn == 0 and hi % tn == 0 for lo, hi, _ in outs)
    return pl.pallas_call(
        functools.partial(_norm_linear_kernel, outs=tuple(outs), tn=tn),
        grid=(t // tm,),
        in_specs=[pl.BlockSpec((tm, d), lambda i: (i, 0)), _resident((1, d)), _resident_layer(w_stack, layer)],
        out_specs=[pl.BlockSpec((tm, hi - lo), lambda i: (i, 0)) for lo, hi, _ in outs],
        out_shape=[jax.ShapeDtypeStruct((t, hi - lo), dt) for lo, hi, dt in outs],
        compiler_params=_cparams(("parallel",)),
        name="norm_linear",
    )(x, g.reshape(1, d), w_stack)


def _out_ffn_kernel(y_ref, wmix_ref, h_ref, g_ref, wgu_ref, wo_ref, *rest, th, final):
    o_ref = rest[-1]
    h = h_ref[...] + _dot(y_ref[...].astype(BF16), wmix_ref[...])
    xn = _rms(h, g_ref[...]).astype(BF16)
    hidden = wo_ref.shape[0]
    o_ref[...] = h
    for c0 in range(0, hidden, th):
        gate = _dot(xn, wgu_ref[:, c0:c0 + th])
        up = _dot(xn, wgu_ref[:, hidden + c0:hidden + c0 + th])
        act = (_silu(gate) * up).astype(BF16)
        o_ref[...] += _dot(act, wo_ref[c0:c0 + th, :])
    if final:
        o_ref[...] = _rms(o_ref[...], rest[0][...])


def out_ffn(y, wmix_stack, j, h, g, wgu_stack, wo_stack, i, tm, g_final=None, th=256):
    t, k = y.shape
    d = h.shape[1]
    hidden = wo_stack.shape[1]
    assert hidden % th == 0 and t % tm == 0
    final = g_final is not None
    extra = [g_final.reshape(1, d)] if final else []
    return pl.pallas_call(
        functools.partial(_out_ffn_kernel, th=th, final=final),
        grid=(t // tm,),
        in_specs=[pl.BlockSpec((tm, k), lambda m: (m, 0)), _resident_layer(wmix_stack, j),
                  pl.BlockSpec((tm, d), lambda m: (m, 0)), _resident((1, d)), _resident_layer(wgu_stack, i),
                  _resident_layer(wo_stack, i)] + [_resident((1, d))] * len(extra),
        out_specs=pl.BlockSpec((tm, d), lambda m: (m, 0)),
        out_shape=jax.ShapeDtypeStruct((t, d), F32),
        compiler_params=_cparams(("parallel",)),
        name="out_ffn",
    )(y, wmix_stack, h, g.reshape(1, d), wgu_stack, wo_stack, *extra)


def _rope_kernel(inv_ref, cos_ref, sin_ref, *, offset):
    rows = cos_ref.shape[0]
    pos = (offset + pl.program_id(0) * rows + lax.broadcasted_iota(jnp.int32, (rows, 1), 0)).astype(F32)
    ang = pos * inv_ref[...]
    cos_ref[...] = jnp.cos(ang)
    sin_ref[...] = jnp.sin(ang)


def rope_tables(offset, rows):
    inv = (1.0 / (ROPE_BASE ** jnp.linspace(0.0, 1.0, ROPE_HALF, dtype=F32))).reshape(1, ROPE_HALF)
    tr = min(rows, 512)
    return pl.pallas_call(
        functools.partial(_rope_kernel, offset=offset),
        grid=(rows // tr,),
        in_specs=[_resident((1, ROPE_HALF))],
        out_specs=[pl.BlockSpec((tr, ROPE_HALF), lambda i: (i, 0))] * 2,
        out_shape=[jax.ShapeDtypeStruct((rows, ROPE_HALF), F32)] * 2,
        compiler_params=_cparams(("parallel",)),
        name="rope_tables",
    )(inv)


def _ret_kernel(q_ref, k_ref, v_ref, g_ref, cos_ref, sin_ref, s0_ref, *rest, C, c_real, carry, slot, n_chunks):
    y_ref, st_ref = rest[carry:carry + 2]
    decay_sc = rest[carry + 2]
    stage = rest[carry + 3:]
    c = pl.program_id(1)
    log_gamma = [math.log(1.0 - 2.0 ** (-5.0 - hd)) for hd in range(RET_HEADS)]
    i = lax.broadcasted_iota(jnp.int32, (C, 1), 0).astype(F32)
    j = lax.broadcasted_iota(jnp.int32, (1, C), 1).astype(F32)

    @pl.when(c == 0)
    def _():
        st_ref[slot] = s0_ref[...]
        rel = i - j
        for hd in range(RET_HEADS):
            decay_sc[hd] = jnp.where(rel >= 0, jnp.exp(jnp.maximum(rel, 0.0) * log_gamma[hd]), 0.0)

    if c_real == C:
        q_src, k_src, v_src = q_ref.at[0], k_ref.at[0], v_ref.at[0]
    else:
        for ref, st in zip((q_ref, k_ref, v_ref), stage):
            st[...] = jnp.zeros(st.shape, F32)
            st[0:c_real, :] = ref[0].astype(F32)
        q_src, k_src, v_src = stage
    cos = cos_ref[...]
    sin = sin_ref[...]

    def rot(x):
        x1, x2 = x[:, :ROPE_HALF], x[:, ROPE_HALF:]
        return jnp.concatenate([x1 * cos - x2 * sin, x2 * cos + x1 * sin], axis=-1)

    for hd in range(RET_HEADS):
        lg = log_gamma[hd]
        ks = slice(hd * RET_DK, (hd + 1) * RET_DK)
        vs = slice(hd * RET_DV, (hd + 1) * RET_DV)
        qr = rot(q_src[:, ks].astype(F32)).astype(BF16)
        kr = rot(k_src[:, ks].astype(F32)) * (RET_DK ** -0.5)
        v = v_src[:, vs].astype(BF16)
        scores = _dot_nt(qr, kr.astype(BF16)) * decay_sc[hd]
        s_prev = st_ref[slot, 0, hd]
        o = _dot(scores.astype(BF16), v) + _dot(qr, s_prev.astype(BF16)) * jnp.exp((i + 1.0) * lg)
        kw = (kr * jnp.exp((c_real - 1.0 - i) * lg)).astype(BF16)
        st_ref[slot, 0, hd] = math.exp(c_real * lg) * s_prev + _dot_tn(kw, v)
        o = o * lax.rsqrt(jnp.mean(o * o, axis=-1, keepdims=True) + EPS)
        y_ref[0, :, vs] = (_silu(g_ref[0, :, vs].astype(F32)) * o[0:c_real]).astype(y_ref.dtype)

    if st_ref.shape[0] > 1:
        @pl.when(c == n_chunks - 1)
        def _():
            for other in range(st_ref.shape[0]):
                if other != slot:
                    st_ref[other] = st_ref[slot]


def retention_core(qkvg, cos, sin, s0_stack, j0, s1_stack, j1, n_layers, c_real, out_dtype):
    b, l, _ = qkvg.shape
    C = cos.shape[0] if c_real < RET_CHUNK else RET_CHUNK
    nc = l // c_real
    qk_w, v_w = RET_HEADS * RET_DK, RET_VDIM
    k_blk, v_blk, g_blk = 1, 2 * qk_w // v_w, 2 * qk_w // v_w + 1
    stage = [] if c_real == C else [pltpu.VMEM((C, qk_w), F32), pltpu.VMEM((C, qk_w), F32), pltpu.VMEM((C, v_w), F32)]
    carry = s1_stack is not None
    st_block, st_first, slot = (1, j1, 0) if carry else (n_layers, 0, j1)
    return pl.pallas_call(
        functools.partial(_ret_kernel, C=C, c_real=c_real, carry=int(carry), slot=slot, n_chunks=nc),
        grid=(b, nc),
        in_specs=[
            pl.BlockSpec((1, c_real, qk_w), lambda bi, c: (bi, c, 0)),
            pl.BlockSpec((1, c_real, qk_w), lambda bi, c: (bi, c, k_blk)),
            pl.BlockSpec((1, c_real, v_w), lambda bi, c: (bi, c, v_blk)),
            pl.BlockSpec((1, c_real, v_w), lambda bi, c: (bi, c, g_blk)),
            pl.BlockSpec((C, ROPE_HALF), lambda bi, c: (c, 0)),
            pl.BlockSpec((C, ROPE_HALF), lambda bi, c: (c, 0)),
            pl.BlockSpec((None, 1, RET_HEADS, RET_DK, RET_DV), lambda bi, c: (j0, bi, 0, 0, 0)),
        ] + ([pl.BlockSpec(memory_space=pl.ANY)] if carry else []),
        out_specs=[
            pl.BlockSpec((1, c_real, v_w), lambda bi, c: (bi, c, 0)),
            pl.BlockSpec((st_block, 1, RET_HEADS, RET_DK, RET_DV), lambda bi, c: (st_first, bi, 0, 0, 0)),
        ],
        out_shape=[jax.ShapeDtypeStruct((b, l, RET_VDIM), out_dtype),
                   jax.ShapeDtypeStruct((n_layers, b, RET_HEADS, RET_DK, RET_DV), F32)],
        scratch_shapes=[pltpu.VMEM((RET_HEADS, C, C), F32)] + stage,
        input_output_aliases={7: 1} if carry else {},
        compiler_params=_cparams(("parallel", "arbitrary")),
        name="retention_core",
    )(qkvg, qkvg, qkvg, qkvg, cos, sin, s0_stack, *([s1_stack] if carry else []))


def _softplus(x):
    return jnp.maximum(x, 0.0) + jnp.log1p(jnp.exp(-jnp.abs(x)))


def _cumsum(x, axis):
    n = x.shape[axis]
    idx = lax.broadcasted_iota(jnp.int32, x.shape, axis)
    s = 1
    while s < n:
        x = x + jnp.where(idx >= s, pltpu.roll(x, s, axis), 0.0)
        s *= 2
    return x


def _ssd_kernel(z_ref, x_ref, bc_ref, dt_ref, dtt_ref, cw_ref, cb_ref, dtb_ref, dtbt_ref, al_ref, alt_ref,
                dsk_ref, ng_ref, buf_ref, s0_ref, y_ref, s1_ref, extx, extbc, xs_ref, bcs_ref, *, C, c_real):
    c = pl.program_id(1)
    halo = SUBLANES

    @pl.when(c == 0)
    def _():
        s1_ref[...] = s0_ref[...]
        extx[...] = jnp.zeros(extx.shape, F32)
        extbc[...] = jnp.zeros(extbc.shape, F32)
        extx[0:halo, :] = buf_ref[0, :, 0:SSD_INNER]
        extbc[0:halo, :] = buf_ref[0, :, SSD_INNER:SSD_CONV_CH]

    extx[halo:halo + c_real, :] = x_ref[0].astype(F32)
    extbc[halo:halo + c_real, :] = bc_ref[0].astype(F32)

    cw = 512
    first = halo - (SSD_CONV - 1)
    for ext, dst, ch0 in ((extx, xs_ref, 0), (extbc, bcs_ref, SSD_INNER)):
        for col in range(0, SSD_INNER, cw):
            acc = cb_ref[:, ch0 + col:ch0 + col + cw]
            for tap in range(SSD_CONV):
                acc = acc + ext[first + tap:first + tap + C, col:col + cw] * cw_ref[tap:tap + 1, ch0 + col:ch0 + col + cw]
            dst[:, col:col + cw] = _silu(acc).astype(dst.dtype)
    extx[0:halo, :] = extx[C:C + halo, :]
    extbc[0:halo, :] = extbc[C:C + halo, :]

    row = lax.broadcasted_iota(jnp.int32, (C, 1), 0)
    col_i = lax.broadcasted_iota(jnp.int32, (1, C), 1)
    dt = jnp.where(row < c_real, _softplus(dt_ref[0, 0] + dtb_ref[...]), 0.0)
    dtt = jnp.where(col_i < c_real, _softplus(dtt_ref[0, 0] + dtbt_ref[...]), 0.0)
    cs = _cumsum(dt * (-jnp.exp(al_ref[...])), 0)
    cst = _cumsum(dtt * (-jnp.exp(alt_ref[...])), 1)
    causal = row >= col_i
    lane_r = lax.broadcasted_iota(jnp.int32, (1, SSD_GW), 1) // SSD_HEADDIM
    sub_r = lax.broadcasted_iota(jnp.int32, (SSD_GW, 1), 0) // SSD_HEADDIM

    def by_head(vals, sel):
        out = vals[SSD_HPG - 1]
        for r in range(SSD_HPG - 2, -1, -1):
            out = jnp.where(sel == r, vals[r], out)
        return out

    for g in range(SSD_GROUPS):
        bm = bcs_ref[:, g * SSD_STATE:(g + 1) * SSD_STATE]
        cm = bcs_ref[:, SSD_BC + g * SSD_STATE:SSD_BC + (g + 1) * SSD_STATE]
        cb = _dot_nt(cm, bm)
        h0 = s1_ref[0, g * SSD_HPG:(g + 1) * SSD_HPG].reshape(SSD_GW, SSD_STATE)
        xg = xs_ref[:, g * SSD_GW:(g + 1) * SSD_GW]
        heads = range(g * SSD_HPG, (g + 1) * SSD_HPG)
        cs_cols = [cs[:, hh:hh + 1] for hh in heads]
        cs_last = [cs[C - 1:C, hh:hh + 1] for hh in heads]
        yg = _dot_nt(cm, h0.astype(BF16)) * by_head([jnp.exp(cc) for cc in cs_cols], lane_r)
        for r, hh in enumerate(heads):
            seg = cs_cols[r] - cst[hh:hh + 1, :]
            w = cb * jnp.exp(jnp.where(causal, seg, -jnp.inf)) * dtt[hh:hh + 1, :]
            yg = yg + _dot(w.astype(BF16), jnp.where(lane_r == r, xg, 0.0).astype(BF16))
        w_end = by_head([jnp.exp(cl - cc) * dt[:, hh:hh + 1] for cl, cc, hh in zip(cs_last, cs_cols, heads)], lane_r)
        h1 = by_head([jnp.exp(cl) for cl in cs_last], sub_r) * h0 + _dot_tn((xg * w_end).astype(BF16), bm)
        s1_ref[0, g * SSD_HPG:(g + 1) * SSD_HPG] = h1.reshape(SSD_HPG, SSD_HEADDIM, SSD_STATE)
        sl = slice(g * SSD_GW, (g + 1) * SSD_GW)
        yo = (yg[0:c_real] + dsk_ref[:, sl] * xg[0:c_real]) * _silu(z_ref[0, :, sl].astype(F32))
        yo = yo * lax.rsqrt(jnp.mean(yo * yo, axis=-1, keepdims=True) + EPS) * ng_ref[:, sl]
        y_ref[0, :, sl] = yo.astype(y_ref.dtype)


def ssd_core(zxbc, dt_raw, conv_buf, s0, conv_w, conv_b, dt_bias, a_log, d_skip, norm_g, c_real, out_dtype):
    b, l, _ = zxbc.shape
    C = SSD_CHUNK
    nc = l // c_real
    halo = SUBLANES
    dt4 = dt_raw.reshape(b, nc, c_real, LANES)
    dtt = jnp.swapaxes(dt4[..., :SSD_HEADS], -1, -2)
    if c_real < C:
        dt4 = jnp.pad(dt4, ((0, 0), (0, 0), (0, C - c_real), (0, 0)))
        dtt = jnp.pad(dtt, ((0, 0), (0, 0), (0, 0), (0, C - c_real)))
    lane_pad = LANES - SSD_HEADS
    buf8 = jnp.pad(conv_buf.astype(F32), ((0, 0), (halo - (SSD_CONV - 1), 0), (0, 0)))
    small = [
        conv_w.astype(F32), conv_b.reshape(1, -1).astype(F32),
        jnp.pad(dt_bias.astype(F32), (0, lane_pad)).reshape(1, LANES), dt_bias.astype(F32).reshape(SSD_HEADS, 1),
        jnp.pad(a_log.astype(F32), (0, lane_pad)).reshape(1, LANES), a_log.astype(F32).reshape(SSD_HEADS, 1),
        jnp.repeat(d_skip.astype(F32), SSD_HEADDIM).reshape(1, SSD_INNER), norm_g.astype(F32).reshape(1, SSD_INNER),
    ]
    w = SSD_INNER
    return pl.pallas_call(
        functools.partial(_ssd_kernel, C=C, c_real=c_real),
        grid=(b, nc),
        in_specs=[
            pl.BlockSpec((1, c_real, w), lambda bi, c: (bi, c, 0)),
            pl.BlockSpec((1, c_real, w), lambda bi, c: (bi, c, 1)),
            pl.BlockSpec((1, c_real, w), lambda bi, c: (bi, c, 2)),
            pl.BlockSpec((1, 1, C, LANES), lambda bi, c: (bi, c, 0, 0)),
            pl.BlockSpec((1, 1, SSD_HEADS, C), lambda bi, c: (bi, c, 0, 0)),
        ] + [_resident(a.shape) for a in small] + [
            pl.BlockSpec((1, halo, SSD_CONV_CH), lambda bi, c: (bi, 0, 0)),
            pl.BlockSpec((1, SSD_HEADS, SSD_HEADDIM, SSD_STATE), lambda bi, c: (bi, 0, 0, 0)),
        ],
        out_specs=[
            pl.BlockSpec((1, c_real, w), lambda bi, c: (bi, c, 0)),
            pl.BlockSpec((1, SSD_HEADS, SSD_HEADDIM, SSD_STATE), lambda bi, c: (bi, 0, 0, 0)),
        ],
        out_shape=[jax.ShapeDtypeStruct((b, l, SSD_INNER), out_dtype),
                   jax.ShapeDtypeStruct((b, SSD_HEADS, SSD_HEADDIM, SSD_STATE), F32)],
        scratch_shapes=[pltpu.VMEM((C + halo, w), F32), pltpu.VMEM((C + halo, w), F32),
                        pltpu.VMEM((C, w), F32), pltpu.VMEM((C, w), BF16)],
        compiler_params=_cparams(("parallel", "arbitrary")),
        name="ssd_core",
    )(zxbc, zxbc, zxbc, dt4, dtt, *small, buf8, s0)


def _t5_bias(dist, table_at):
    n = jnp.maximum(dist, 0)
    exact = REL_BUCKETS // 2
    nf = jnp.maximum(n, 1).astype(F32)
    large = exact + (jnp.log(nf / exact) / math.log(REL_MAX_DIST / exact) * (REL_BUCKETS - exact)).astype(jnp.int32)
    bucket = jnp.where(n < exact, n, jnp.minimum(large, REL_BUCKETS - 1))
    bias = jnp.zeros(dist.shape, F32)
    for bkt in range(REL_BUCKETS):
        bias = jnp.where(bucket == bkt, table_at(bkt), bias)
    return bias


def _lambda(lq1, lk1, lq2, lk2, lam_init):
    s1 = jnp.sum(lq1[...] * lk1[...], axis=-1, keepdims=True)
    s2 = jnp.sum(lq2[...] * lk2[...], axis=-1, keepdims=True)
    return jnp.exp(s1) - jnp.exp(s2) + lam_init


def _diff_in_proj_kernel(x_ref, g_ref, wq_ref, wkt_ref, wv_ref, q_ref, kt32_ref, kt16_ref, v32_ref, v16_ref, *, tn, kb):
    xn = _rms(x_ref[0], g_ref[...]).astype(BF16)
    d = wq_ref.shape[1]
    tm = xn.shape[0]
    for c0 in range(0, d, tn):
        q_ref[0, :, c0:c0 + tn] = (_dot(xn, wq_ref[:, c0:c0 + tn]) * (DIFF_DH ** -0.5)).astype(BF16)
        v = _dot(xn, wv_ref[:, c0:c0 + tn])
        v32_ref[0, :, c0:c0 + tn] = v
        v16_ref[0, :, c0:c0 + tn] = v.astype(BF16)
        kt = _dot_nt(wkt_ref[c0:c0 + tn, :], xn)
        kt32_ref[0, c0:c0 + tn, :] = kt
        for s in range(tm // kb):
            kt16_ref[0, s, c0:c0 + tn, :] = kt[:, s * kb:(s + 1) * kb].astype(BF16)


def diff_in_proj(x, g, wq, wkt, wv, tm, kb):
    b, l, d = x.shape
    assert l % tm == 0 and tm % kb == 0
    tok = lambda: pl.BlockSpec((1, tm, d), lambda bi, i: (bi, i, 0))
    return pl.pallas_call(
        functools.partial(_diff_in_proj_kernel, tn=512, kb=kb),
        grid=(b, l // tm),
        in_specs=[tok(), _resident((1, d)), _resident((d, d)), _resident((d, d)), _resident((d, d))],
        out_specs=[tok(), pl.BlockSpec((1, d, tm), lambda bi, i: (bi, 0, i)),
                   pl.BlockSpec((1, tm // kb, d, kb), lambda bi, i: (bi, i, 0, 0)), tok(), tok()],
        out_shape=[jax.ShapeDtypeStruct((b, l, d), BF16), jax.ShapeDtypeStruct((b, d, l), F32),
                   jax.ShapeDtypeStruct((b, l // kb, d, kb), BF16), jax.ShapeDtypeStruct((b, l, d), F32),
                   jax.ShapeDtypeStruct((b, l, d), BF16)],
        compiler_params=_cparams(("parallel", "parallel")),
        name="diff_in_proj",
    )(x, g.reshape(1, d), wq, wkt, wv)


def _attn_prompt_kernel(tbl_ref, q_ref, kt_ref, v_ref, lq1, lk1, lq2, lk2, sg_ref, o_ref,
                        bias_sc, m_sc, a_sc, *, T, lam_init):
    hd = pl.program_id(0)
    bi = pl.program_id(1)
    qi = pl.program_id(2)

    @pl.when((bi == 0) & (qi == 0))
    def _():
        i = lax.broadcasted_iota(jnp.int32, (T, 2 * T), 0)
        j = lax.broadcasted_iota(jnp.int32, (T, 2 * T), 1)
        dist = i - j + T
        bias_sc[...] = jnp.where(dist >= 0, _t5_bias(dist, lambda bkt: tbl_ref[bkt, hd]), -jnp.inf)

    lane = lax.broadcasted_iota(jnp.int32, (1, DIFF_DV), 1)
    q = q_ref[0]
    zero = jnp.zeros((), q.dtype)
    q2 = jnp.concatenate([jnp.where(lane < DIFF_DH, q, zero), jnp.where(lane >= DIFF_DH, q, zero)], axis=0)
    m_sc[...] = jnp.full(m_sc.shape, -jnp.inf, F32)
    a_sc[...] = jnp.zeros(a_sc.shape, F32)

    def update(kb, bias):
        vt = v_ref[0, pl.ds(pl.multiple_of(kb * T, T), T), :]
        vx = jnp.concatenate([vt, jnp.ones((T, DIFF_DV), BF16)], axis=1)
        s = _dot(q2, kt_ref[0, kb])
        s = s + (jnp.concatenate([bias, bias], axis=0) if getattr(bias, "ndim", 0) == 2 else bias)
        m_prev = m_sc[...]
        m_new = jnp.maximum(m_prev, jnp.max(s, axis=-1, keepdims=True))
        alpha = jnp.exp(m_prev - m_new)
        p = jnp.concatenate([jnp.exp(s[:, c:c + LANES] - m_new) for c in range(0, T, LANES)], axis=1)
        a_sc[...] = jnp.concatenate([alpha, alpha], axis=1) * a_sc[...] + _dot(p.astype(BF16), vx)
        m_sc[...] = m_new

    far_bias = tbl_ref[REL_BUCKETS - 1, hd]

    def far_body(kb, carry):
        update(kb, far_bias)
        return carry

    lax.fori_loop(0, jnp.maximum(qi - 1, 0), far_body, 0)

    @pl.when(qi >= 1)
    def _():
        update(qi - 1, bias_sc[:, 0:T])

    update(qi, bias_sc[:, T:2 * T])

    lam = _lambda(lq1, lk1, lq2, lk2, lam_init)
    o = a_sc[:T, :DIFF_DV] / a_sc[:T, DIFF_DV:] - lam * (a_sc[T:, :DIFF_DV] / a_sc[T:, DIFF_DV:])
    o_ref[0] = (_rms(o, sg_ref[...]) * (1.0 - lam_init)).astype(o_ref.dtype)


def diff_attention_prompt(q, kt, v, rel_table, lams, subln_g, lam_init, T):
    b, l, _ = q.shape
    assert T >= REL_MAX_DIST and l % T == 0 and DIFF_DV == 2 * DIFF_DH
    vec = lambda a: a.astype(F32).reshape(1, -1)
    return pl.pallas_call(
        functools.partial(_attn_prompt_kernel, T=T, lam_init=lam_init),
        grid=(DIFF_HEADS, b, l // T),
        in_specs=[
            pl.BlockSpec(memory_space=pltpu.SMEM),
            pl.BlockSpec((1, T, DIFF_DV), lambda h, bi, qi: (bi, qi, h)),
            pl.BlockSpec((1, l // T, DIFF_DV, T), lambda h, bi, qi: (bi, 0, h, 0)),
            pl.BlockSpec((1, l, DIFF_DV), lambda h, bi, qi: (bi, 0, h)),
        ] + [_resident((1, DIFF_DH))] * 4 + [_resident((1, DIFF_DV))],
        out_specs=pl.BlockSpec((1, T, DIFF_DV), lambda h, bi, qi: (bi, qi, h)),
        out_shape=jax.ShapeDtypeStruct((b, l, DIFF_HEADS * DIFF_DV), BF16),
        scratch_shapes=[pltpu.VMEM((T, 2 * T), F32), pltpu.VMEM((2 * T, LANES), F32), pltpu.VMEM((2 * T, 2 * DIFF_DV), F32)],
        compiler_params=_cparams(("arbitrary", "arbitrary", "arbitrary")),
        name="diff_attention_prompt",
    )(rel_table.astype(F32), q, kt, v, *[vec(a) for a in lams], vec(subln_g))


DEC_RPH = SUBLANES


def _attn_sample_kernel(pt_ref, q_ref, kn_ref, vn_ref, tb_ref, lq1, lk1, lq2, lk2, sg_ref, *rest,
                        pps, n_pages, lq, lam_init):
    kt_refs, v_refs = rest[:pps], rest[pps:2 * pps]
    o_ref, qm_sc, m_sc, l_sc, acc_sc = rest[2 * pps:]
    s = pl.program_id(1)
    n_steps = n_pages // pps
    past = n_pages * PAGE_SIZE
    rows = DIFF_HEADS * DEC_RPH
    width = DIFF_HEADS * DIFF_DV
    rid = lax.broadcasted_iota(jnp.int32, (rows, 1), 0)
    row_tok = (rid % DEC_RPH) // 2

    def q_rows():
        r8 = lax.broadcasted_iota(jnp.int32, (DEC_RPH, width), 0)
        lane_pair = lax.broadcasted_iota(jnp.int32, (DEC_RPH, width), 1) // DIFF_DH
        qrep = jnp.zeros((DEC_RPH, width), F32)
        for t in range(lq):
            qrep = jnp.where(r8 // 2 == t, q_ref[0, t:t + 1, :], qrep)
        qrep = qrep * (DIFF_DH ** -0.5)
        return jnp.concatenate([jnp.where(lane_pair == 2 * hh + r8 % 2, qrep, 0.0) for hh in range(DIFF_HEADS)], axis=0)

    @pl.when(s == 0)
    def _():
        qm_sc[...] = q_rows().astype(BF16)
        m_sc[...] = jnp.full(m_sc.shape, -jnp.inf, F32)
        l_sc[...] = jnp.zeros(l_sc.shape, F32)
        acc_sc[...] = jnp.zeros(acc_sc.shape, F32)

    far_bias = tb_ref[:, REL_BUCKETS - 1:REL_BUCKETS]

    def pages_update(last_near):
        qm = qm_sc[...]
        sc = []
        for i in range(pps):
            sci = _dot(qm, kt_refs[i][0].astype(BF16))
            if last_near and i == pps - 1:
                kpos = (n_pages - 1) * PAGE_SIZE + lax.broadcasted_iota(jnp.int32, (1, PAGE_SIZE), 1)
                sc.append(sci + _t5_bias((past + row_tok) - kpos, lambda bkt: tb_ref[:, bkt:bkt + 1]))
            else:
                sc.append(sci + far_bias)
        m_prev = m_sc[...]
        m_new = m_prev
        for sci in sc:
            m_new = jnp.maximum(m_new, jnp.max(sci, axis=-1, keepdims=True))
        alpha = jnp.exp(m_prev - m_new)
        p = [jnp.exp(sci - m_new) for sci in sc]
        l_new = alpha * l_sc[...]
        for pi in p:
            l_new = l_new + jnp.sum(pi, axis=-1, keepdims=True)
        l_sc[...] = l_new
        m_sc[...] = m_new
        for hh in range(DIFF_HEADS):
            sl = slice(hh * DEC_RPH, (hh + 1) * DEC_RPH)
            ph = jnp.concatenate([pi[sl] for pi in p], axis=1).astype(BF16)
            vh = jnp.concatenate([v_refs[i][0, pl.ds(hh, PAGE_SIZE, stride=DIFF_HEADS), :].astype(BF16)
                                  for i in range(pps)], axis=0)
            acc_sc[sl, :] = alpha[sl] * acc_sc[sl, :] + _dot(ph, vh)

    @pl.when(s < n_steps - 1)
    def _():
        pages_update(False)

    @pl.when(s == n_steps - 1)
    def _():
        pages_update(True)

    @pl.when(s == n_steps)
    def _():
        qf = q_rows()
        sj = []
        for jn in range(lq):
            dist = row_tok - jn
            sc = jnp.sum(qf * kn_ref[0, jn:jn + 1, :], axis=-1, keepdims=True)
            sc = sc + _t5_bias(dist, lambda bkt: tb_ref[:, bkt:bkt + 1])
            sj.append(jnp.where(dist >= 0, sc, -jnp.inf))
        m_prev = m_sc[...]
        m_new = m_prev
        for sc in sj:
            m_new = jnp.maximum(m_new, sc)
        alpha = jnp.exp(m_prev - m_new)
        l_new = alpha * l_sc[...]
        acc = alpha * acc_sc[...]
        for jn, sc in enumerate(sj):
            p = jnp.exp(sc - m_new)
            l_new = l_new + p
            vrow = jnp.concatenate(
                [jnp.broadcast_to(vn_ref[0, jn:jn + 1, hh * DIFF_DV:(hh + 1) * DIFF_DV], (DEC_RPH, DIFF_DV))
                 for hh in range(DIFF_HEADS)], axis=0)
            acc = acc + p * vrow

        lam = _lambda(lq1, lk1, lq2, lk2, lam_init)
        coef = jnp.where(rid % 2 == 0, 1.0, -lam) / l_new
        a = acc * coef
        a = a + pltpu.roll(a, rows - 1, 0)
        res = _rms(a, sg_ref[...]) * (1.0 - lam_init)
        for hh in range(DIFF_HEADS):
            for t in range(lq):
                r = hh * DEC_RPH + 2 * t
                o_ref[0, t:t + 1, hh * DIFF_DV:(hh + 1) * DIFF_DV] = res[r:r + 1, :]


def diff_attention_sample(q, k_new, v_new, cache_kt, cache_v, page_table, rel_table, lams, subln_g, lam_init, pps=16):
    b, lq, width = q.shape
    n_pages = page_table.shape[1]
    assert n_pages % pps == 0 and 2 * lq <= DEC_RPH and n_pages // pps >= 1
    n_steps = n_pages // pps
    rows = DIFF_HEADS * DEC_RPH
    head_of_row = np.arange(rows) // DEC_RPH
    tb = jnp.pad(rel_table.astype(F32).T[head_of_row], ((0, 0), (0, LANES - REL_BUCKETS)))
    vec = lambda a: a.astype(F32).reshape(1, -1)

    def page_map(i):
        return lambda bi, s, pt: (pt[bi * n_pages + jnp.minimum(s, n_steps - 1) * pps + i], 0, 0)

    tok = pl.BlockSpec((1, lq, width), lambda bi, s, pt: (bi, 0, 0))
    const = lambda shape: pl.BlockSpec(shape, lambda bi, s, pt: (0,) * len(shape))
    kpage = [pl.BlockSpec((1, width, PAGE_SIZE), page_map(i)) for i in range(pps)]
    vpage = [pl.BlockSpec((1, PAGE_SIZE * DIFF_HEADS, DIFF_DV), page_map(i)) for i in range(pps)]
    grid_spec = pltpu.PrefetchScalarGridSpec(
        num_scalar_prefetch=1,
        grid=(b, n_steps + 1),
        in_specs=[tok, tok, tok, const((rows, LANES))] + [const((1, DIFF_DH))] * 4 + [const((1, DIFF_DV))] + kpage + vpage,
        out_specs=tok,
        scratch_shapes=[pltpu.VMEM((rows, width), BF16), pltpu.VMEM((rows, 1), F32), pltpu.VMEM((rows, 1), F32),
                        pltpu.VMEM((rows, DIFF_DV), F32)],
    )
    return pl.pallas_call(
        functools.partial(_attn_sample_kernel, pps=pps, n_pages=n_pages, lq=lq, lam_init=lam_init),
        grid_spec=grid_spec,
        out_shape=jax.ShapeDtypeStruct((b, lq, width), F32),
        compiler_params=_cparams(("parallel", "arbitrary")),
        name="diff_attention_sample",
    )(page_table.reshape(-1).astype(jnp.int32), q, k_new, v_new, tb, *[vec(a) for a in lams], vec(subln_g),
      *([cache_kt] * pps), *([cache_v] * pps))


def kernel(x_prompt, x_sample, state_ret, state_ssm, state_conv, cache_k_diff, cache_v_diff, page_table, norm_mix_g, norm_ffn_g, norm_final_g, ret_w_in, ret_w_out, ssd_w_in, ssd_conv_w, ssd_conv_b, ssd_dt_bias, ssd_A_log, ssd_D, ssd_norm_g, ssd_w_out, diff_w_in, diff_lam_q1, diff_lam_k1, diff_lam_q2, diff_lam_k2, diff_subln_g, diff_w_out, rel_bias_table, ffn_w_in, ffn_w_out):
    kinds = tuple(i % 3 for i in range(DEPTH))
    n_pages = page_table.shape[1]
    bf = lambda a: a.astype(BF16)
    ret_wi, ret_wo = bf(ret_w_in), bf(ret_w_out)
    ssd_wmain = bf(ssd_w_in[:, :, :SSD_INNER + SSD_CONV_CH])
    ssd_wxbc = ssd_wmain[:, :, SSD_INNER:]
    ssd_wdt = bf(jnp.pad(ssd_w_in[:, :, SSD_INNER + SSD_CONV_CH:], ((0, 0), (0, 0), (0, LANES - SSD_HEADS))))
    ssd_wo = bf(ssd_w_out)
    diff_wi, diff_wo = bf(diff_w_in), bf(diff_w_out)
    ffn_wi, ffn_wo = bf(ffn_w_in), bf(ffn_w_out)

    n_ret = kinds.count(0)

    def run_group(x, sample):
        b, l, d = x.shape
        t = b * l
        tm = 512 if t % 512 == 0 else t
        ret_c = RET_CHUNK if l % RET_CHUNK == 0 else l
        ssd_c = SSD_CHUNK if l % SSD_CHUNK == 0 else l
        act = F32 if sample else BF16
        offset = n_pages * PAGE_SIZE if sample else 0
        rope_rows = l if l % RET_CHUNK == 0 else -(-l // BF16_ROWS) * BF16_ROWS
        cos, sin = rope_tables(offset, rope_rows)
        ret_new, ssm_new, conv_new, k_new, v_new = None, [], [], [], []
        h = x.reshape(t, d)
        for i in range(DEPTH):
            kind = kinds[i]
            j = kinds[:i].count(kind)
            g = norm_mix_g[i]
            if kind == 0:
                n_in = ret_wi.shape[2]
                (qkvg,) = norm_linear(h, g, ret_wi, j, [(0, n_in, act)], tm)
                s0, j0 = (state_ret, j) if sample else (jnp.zeros((1, b, RET_HEADS, RET_DK, RET_DV), F32), 0)
                y, ret_new = retention_core(qkvg.reshape(b, l, n_in), cos, sin, s0, j0, ret_new, j, n_ret, ret_c, act)
                y, wmix = y.reshape(t, RET_VDIM), ret_wo
            elif kind == 1:
                n_in = SSD_INNER + SSD_CONV_CH
                (zxbc,) = norm_linear(h, g, ssd_wmain, j, [(0, n_in, act)], tm)
                (dt_raw,) = norm_linear(h, g, ssd_wdt, j, [(0, LANES, F32)], tm)
                zxbc = zxbc.reshape(b, l, n_in)
                if sample:
                    buf, s0 = state_conv[j], state_ssm[j]
                    conv_new.append(zxbc[:, l - (SSD_CONV - 1):, SSD_INNER:])
                else:
                    buf = jnp.zeros((b, SSD_CONV - 1, SSD_CONV_CH), F32)
                    s0 = jnp.zeros((b, SSD_HEADS, SSD_HEADDIM, SSD_STATE), F32)
                    tail = h.reshape(b, l, d)[:, l - (SSD_CONV - 1):].reshape(b * (SSD_CONV - 1), d)
                    rows = -(-tail.shape[0] // BF16_ROWS) * BF16_ROWS
                    tail = jnp.pad(tail, ((0, rows - tail.shape[0]), (0, 0)))
                    (xbc_tail,) = norm_linear(tail, g, ssd_wxbc, j, [(0, SSD_CONV_CH, F32)], rows)
                    conv_new.append(xbc_tail[:b * (SSD_CONV - 1)].reshape(b, SSD_CONV - 1, SSD_CONV_CH))
                y, s1 = ssd_core(zxbc, dt_raw.reshape(b, l, LANES), buf, s0, ssd_conv_w[j], ssd_conv_b[j],
                                 ssd_dt_bias[j], ssd_A_log[j], ssd_D[j], ssd_norm_g[j], ssd_c, act)
                ssm_new.append(s1)
                y, wmix = y.reshape(t, SSD_INNER), ssd_wo
            else:
                lam_init = 0.8 - 0.6 * math.exp(-0.3 * i)
                lams = (diff_lam_q1[j], diff_lam_k1[j], diff_lam_q2[j], diff_lam_k2[j])
                if sample:
                    q, kn, vn = norm_linear(h, g, diff_wi, j, [(0, d, F32), (d, 2 * d, F32), (2 * d, 3 * d, F32)], tm)
                    cache_kt = jnp.transpose(cache_k_diff[j], (0, 2, 3, 4, 1)).reshape(-1, d, PAGE_SIZE)
                    cache_v = cache_v_diff[j].reshape(-1, PAGE_SIZE * DIFF_HEADS, DIFF_DV)
                    y = diff_attention_sample(q.reshape(b, l, d), kn.reshape(b, l, d), vn.reshape(b, l, d), cache_kt,
                                              cache_v, page_table, rel_bias_table, lams, diff_subln_g[j], lam_init)
                    k_new.append(kn.reshape(b, l, DIFF_HEADS, 2, DIFF_DH))
                    v_new.append(vn.reshape(b, l, DIFF_HEADS, DIFF_DV))
                else:
                    w = diff_wi[j]
                    q, kt32, kt16, v32, v16 = diff_in_proj(h.reshape(b, l, d), g, w[:, :d], w[:, d:2 * d].T, w[:, 2 * d:],
                                                           tm, ATTN_TILE)
                    y = diff_attention_prompt(q, kt16, v16, rel_bias_table, lams, diff_subln_g[j], lam_init, ATTN_TILE)
                    k_new.append(jnp.transpose(kt32.reshape(b, DIFF_HEADS, 2, DIFF_DH, l), (0, 4, 1, 2, 3)))
                    v_new.append(v32.reshape(b, l, DIFF_HEADS, DIFF_DV))
                y, wmix = y.reshape(t, d), diff_wo
            h = out_ffn(y, wmix, j, h, norm_ffn_g[i], ffn_wi, ffn_wo, i, tm,
                        g_final=norm_final_g if i == DEPTH - 1 else None)
        return h.reshape(b, l, d), ret_new, jnp.stack(ssm_new), jnp.stack(conv_new), jnp.stack(k_new), jnp.stack(v_new)

    y_p, ret_p, ssm_p, conv_p, k_p, v_p = run_group(x_prompt, False)
    y_s, ret_s, ssm_s, conv_s, k_s, v_s = run_group(x_sample, True)
    return (y_p, y_s, ret_p, ssm_p, conv_p, k_p, v_p, ret_s, ssm_s, conv_s, k_s, v_s)
```

```python
import functools
import math

import jax
import jax.numpy as jnp
import numpy as np
from jax import lax
from jax.experimental import pallas as pl
from jax.experimental.pallas import tpu as pltpu

F32 = jnp.float32
BF16 = jnp.bfloat16

D_MODEL = 1024
DEPTH = 4
PAGE_SIZE = 128
EPS = 1e-6
RET_CHUNK = 256
SSD_CHUNK = 128
ATTN_TILE = 512

RET_HEADS = 4
RET_DK = D_MODEL // RET_HEADS
RET_DV = 2 * RET_DK
RET_VDIM = RET_HEADS * RET_DV
ROPE_BASE = 10000.0
ROPE_HALF = RET_DK // 2

SSD_INNER = 2 * D_MODEL
SSD_HEADDIM = 64
SSD_HEADS = SSD_INNER // SSD_HEADDIM
SSD_GROUPS = 8
SSD_HPG = SSD_HEADS // SSD_GROUPS
SSD_STATE = 128
SSD_CONV = 4
SSD_BC = SSD_GROUPS * SSD_STATE
SSD_CONV_CH = SSD_INNER + 2 * SSD_BC
SSD_GW = SSD_HPG * SSD_HEADDIM

DIFF_HEADS = 8
DIFF_DH = D_MODEL // DIFF_HEADS // 2
DIFF_DV = 2 * DIFF_DH
REL_BUCKETS = 32
REL_MAX_DIST = 128

FFN_HIDDEN = -(-8 * D_MODEL // (3 * 256)) * 256

V7X_VMEM_BYTES = 64 * 1024 * 1024
LANES = 128
SUBLANES = 8
BF16_ROWS = 16
VMEM_LIMIT = 56 * 1024 * 1024


def _cparams(sem):
    return pltpu.CompilerParams(dimension_semantics=sem, vmem_limit_bytes=VMEM_LIMIT)


def _dot(a, b):
    return jnp.dot(a, b, preferred_element_type=F32)


def _dot_nt(a, b):
    return lax.dot_general(a, b, (((1,), (1,)), ((), ())), preferred_element_type=F32)


def _dot_tn(a, b):
    return lax.dot_general(a, b, (((0,), (0,)), ((), ())), preferred_element_type=F32)


def _rms(x, g):
    return x * lax.rsqrt(jnp.mean(x * x, axis=-1, keepdims=True) + EPS) * g


def _silu(x):
    half = 0.5 * x
    return half * jnp.tanh(half) + half


def _resident(shape):
    return pl.BlockSpec(shape, lambda *_: (0,) * len(shape), pipeline_mode=pl.Buffered(1))


def _resident_layer(stacked, layer):
    tail = stacked.shape[1:]
    return pl.BlockSpec((None,) + tail, lambda *_: (layer,) + (0,) * len(tail), pipeline_mode=pl.Buffered(1))


def _norm_linear_kernel(x_ref, g_ref, w_ref, *rest, outs, tn, side):
    side_w = rest[:side]
    o_refs = rest[side:side + len(outs)]
    side_o = rest[side + len(outs):]
    xn = _rms(x_ref[...], g_ref[...]).astype(BF16)
    n = w_ref.shape[1]
    for c0 in range(0, n, tn):
        acc = _dot(xn, w_ref[:, c0:c0 + tn])
        for o_ref, (lo, hi, _) in zip(o_refs, outs):
            if lo <= c0 and c0 + tn <= hi:
                o_ref[:, c0 - lo:c0 - lo + tn] = acc.astype(o_ref.dtype)
    for w2_ref, o2_ref in zip(side_w, side_o):
        o2_ref[...] = _dot(xn, w2_ref[...]).astype(o2_ref.dtype)


def norm_linear(x, g, w_stack, layer, outs, tm, tn=512, side=()):
    t, d = x.shape
    n = w_stack.shape[2]
    tn = min(tn, n)
    assert t % tm == 0 and n % tn == 0 and all(lo % tn == 0 and hi % tn == 0 for lo, hi, _ in outs)
    row = lambda width: pl.BlockSpec((tm, width), lambda i: (i, 0))
    return pl.pallas_call(
        functools.partial(_norm_linear_kernel, outs=tuple(outs), tn=tn, side=len(side)),
        grid=(t // tm,),
        in_specs=[row(d), _resident((1, d)), _resident_layer(w_stack, layer)] + [_resident_layer(w2, layer) for w2, _ in side],
        out_specs=[row(hi - lo) for lo, hi, _ in outs] + [row(w2.shape[2]) for w2, _ in side],
        out_shape=[jax.ShapeDtypeStruct((t, hi - lo), dt) for lo, hi, dt in outs]
        + [jax.ShapeDtypeStruct((t, w2.shape[2]), dt) for w2, dt in side],
        compiler_params=_cparams(("parallel",)),
        name="norm_linear",
    )(x, g.reshape(1, d), w_stack, *[w2 for w2, _ in side])


def _out_ffn_kernel(y_ref, wmix_ref, h_ref, g_ref, wgu_ref, wo_ref, *rest, th, final):
    o_ref = rest[-1]
    h = h_ref[...] + _dot(y_ref[...].astype(BF16), wmix_ref[...])
    xn = _rms(h, g_ref[...]).astype(BF16)
    hidden = wo_ref.shape[0]
    o_ref[...] = h
    for c0 in range(0, hidden, th):
        gate = _dot(xn, wgu_ref[:, c0:c0 + th])
        up = _dot(xn, wgu_ref[:, hidden + c0:hidden + c0 + th])
        act = (_silu(gate) * up).astype(BF16)
        o_ref[...] += _dot(act, wo_ref[c0:c0 + th, :])
    if final:
        o_ref[...] = _rms(o_ref[...], rest[0][...])


def out_ffn(y, wmix_stack, j, h, g, wgu_stack, wo_stack, i, tm, g_final=None, th=256):
    t, k = y.shape
    d = h.shape[1]
    hidden = wo_stack.shape[1]
    assert hidden % th == 0 and t % tm == 0
    final = g_final is not None
    extra = [g_final.reshape(1, d)] if final else []
    return pl.pallas_call(
        functools.partial(_out_ffn_kernel, th=th, final=final),
        grid=(t // tm,),
        in_specs=[pl.BlockSpec((tm, k), lambda m: (m, 0)), _resident_layer(wmix_stack, j),
                  pl.BlockSpec((tm, d), lambda m: (m, 0)), _resident((1, d)), _resident_layer(wgu_stack, i),
                  _resident_layer(wo_stack, i)] + [_resident((1, d))] * len(extra),
        out_specs=pl.BlockSpec((tm, d), lambda m: (m, 0)),
        out_shape=jax.ShapeDtypeStruct((t, d), F32),
        compiler_params=_cparams(("parallel",)),
        name="out_ffn",
    )(y, wmix_stack, h, g.reshape(1, d), wgu_stack, wo_stack, *extra)


def _rope_kernel(inv_ref, cos_ref, sin_ref, *, offset):
    rows = cos_ref.shape[0]
    pos = (offset + pl.program_id(0) * rows + lax.broadcasted_iota(jnp.int32, (rows, 1), 0)).astype(F32)
    ang = pos * inv_ref[...]
    cos_ref[...] = jnp.cos(ang)
    sin_ref[...] = jnp.sin(ang)


def rope_tables(offset, rows):
    inv = (1.0 / (ROPE_BASE ** jnp.linspace(0.0, 1.0, ROPE_HALF, dtype=F32))).reshape(1, ROPE_HALF)
    tr = min(rows, 512)
    return pl.pallas_call(
        functools.partial(_rope_kernel, offset=offset),
        grid=(rows // tr,),
        in_specs=[_resident((1, ROPE_HALF))],
        out_specs=[pl.BlockSpec((tr, ROPE_HALF), lambda i: (i, 0))] * 2,
        out_shape=[jax.ShapeDtypeStruct((rows, ROPE_HALF), F32)] * 2,
        compiler_params=_cparams(("parallel",)),
        name="rope_tables",
    )(inv)


def _ret_kernel(q_ref, k_ref, v_ref, g_ref, cos_ref, sin_ref, s0_ref, *rest, C, c_real, carry, slot, n_chunks):
    y_ref, st_ref = rest[carry:carry + 2]
    decay_sc = rest[carry + 2]
    stage = rest[carry + 3:]
    c = pl.program_id(1)
    log_gamma = [math.log(1.0 - 2.0 ** (-5.0 - hd)) for hd in range(RET_HEADS)]
    i = lax.broadcasted_iota(jnp.int32, (C, 1), 0).astype(F32)
    j = lax.broadcasted_iota(jnp.int32, (1, C), 1).astype(F32)

    @pl.when(c == 0)
    def _():
        st_ref[slot] = s0_ref[...]
        rel = i - j
        for hd in range(RET_HEADS):
            decay_sc[hd] = jnp.where(rel >= 0, jnp.exp(jnp.maximum(rel, 0.0) * log_gamma[hd]), 0.0)

    if c_real == C:
        q_src, k_src, v_src = q_ref.at[0], k_ref.at[0], v_ref.at[0]
    else:
        for ref, st in zip((q_ref, k_ref, v_ref), stage):
            st[...] = jnp.zeros(st.shape, F32)
            st[0:c_real, :] = ref[0].astype(F32)
        q_src, k_src, v_src = stage
    cos = cos_ref[...]
    sin = sin_ref[...]

    def rot(x):
        x1, x2 = x[:, :ROPE_HALF], x[:, ROPE_HALF:]
        return jnp.concatenate([x1 * cos - x2 * sin, x2 * cos + x1 * sin], axis=-1)

    for hd in range(RET_HEADS):
        lg = log_gamma[hd]
        ks = slice(hd * RET_DK, (hd + 1) * RET_DK)
        vs = slice(hd * RET_DV, (hd + 1) * RET_DV)
        qr = rot(q_src[:, ks].astype(F32)).astype(BF16)
        kr = rot(k_src[:, ks].astype(F32)) * (RET_DK ** -0.5)
        v = v_src[:, vs].astype(BF16)
        scores = _dot_nt(qr, kr.astype(BF16)) * decay_sc[hd]
        s_prev = st_ref[slot, 0, hd]
        o = _dot(scores.astype(BF16), v) + _dot(qr, s_prev.astype(BF16)) * jnp.exp((i + 1.0) * lg)
        kw = (kr * jnp.exp((c_real - 1.0 - i) * lg)).astype(BF16)
        st_ref[slot, 0, hd] = math.exp(c_real * lg) * s_prev + _dot_tn(kw, v)
        o = o * lax.rsqrt(jnp.mean(o * o, axis=-1, keepdims=True) + EPS)
        y_ref[0, :, vs] = (_silu(g_ref[0, :, vs].astype(F32)) * o[0:c_real]).astype(y_ref.dtype)

    if st_ref.shape[0] > 1:
        @pl.when(c == n_chunks - 1)
        def _():
            for other in range(st_ref.shape[0]):
                if other != slot:
                    st_ref[other] = st_ref[slot]


def retention_core(qkvg, cos, sin, s0_stack, j0, s1_stack, j1, n_layers, c_real, out_dtype):
    b, l, _ = qkvg.shape
    C = cos.shape[0] if c_real < RET_CHUNK else RET_CHUNK
    nc = l // c_real
    qk_w, v_w = RET_HEADS * RET_DK, RET_VDIM
    k_blk, v_blk, g_blk = 1, 2 * qk_w // v_w, 2 * qk_w // v_w + 1
    stage = [] if c_real == C else [pltpu.VMEM((C, qk_w), F32), pltpu.VMEM((C, qk_w), F32), pltpu.VMEM((C, v_w), F32)]
    carry = s1_stack is not None
    st_block, st_first, slot = (1, j1, 0) if carry else (n_layers, 0, j1)
    return pl.pallas_call(
        functools.partial(_ret_kernel, C=C, c_real=c_real, carry=int(carry), slot=slot, n_chunks=nc),
        grid=(b, nc),
        in_specs=[
            pl.BlockSpec((1, c_real, qk_w), lambda bi, c: (bi, c, 0)),
            pl.BlockSpec((1, c_real, qk_w), lambda bi, c: (bi, c, k_blk)),
            pl.BlockSpec((1, c_real, v_w), lambda bi, c: (bi, c, v_blk)),
            pl.BlockSpec((1, c_real, v_w), lambda bi, c: (bi, c, g_blk)),
            pl.BlockSpec((C, ROPE_HALF), lambda bi, c: (c, 0)),
            pl.BlockSpec((C, ROPE_HALF), lambda bi, c: (c, 0)),
            pl.BlockSpec((None, 1, RET_HEADS, RET_DK, RET_DV), lambda bi, c: (j0, bi, 0, 0, 0)),
        ] + ([pl.BlockSpec(memory_space=pl.ANY)] if carry else []),
        out_specs=[
            pl.BlockSpec((1, c_real, v_w), lambda bi, c: (bi, c, 0)),
            pl.BlockSpec((st_block, 1, RET_HEADS, RET_DK, RET_DV), lambda bi, c: (st_first, bi, 0, 0, 0)),
        ],
        out_shape=[jax.ShapeDtypeStruct((b, l, RET_VDIM), out_dtype),
                   jax.ShapeDtypeStruct((n_layers, b, RET_HEADS, RET_DK, RET_DV), F32)],
        scratch_shapes=[pltpu.VMEM((RET_HEADS, C, C), F32)] + stage,
        input_output_aliases={7: 1} if carry else {},
        compiler_params=_cparams(("parallel", "arbitrary")),
        name="retention_core",
    )(qkvg, qkvg, qkvg, qkvg, cos, sin, s0_stack, *([s1_stack] if carry else []))


def _softplus(x):
    return jnp.maximum(x, 0.0) + jnp.log1p(jnp.exp(-jnp.abs(x)))


def _cumsum(x, axis):
    n = x.shape[axis]
    idx = lax.broadcasted_iota(jnp.int32, x.shape, axis)
    s = 1
    while s < n:
        x = x + jnp.where(idx >= s, pltpu.roll(x, s, axis), 0.0)
        s *= 2
    return x


def _ssd_kernel(z_ref, x_ref, bc_ref, dt_ref, dtt_ref, cw_ref, cb_ref, dtb_ref, dtbt_ref, al_ref, alt_ref,
                dsk_ref, ng_ref, buf_ref, s0_ref, y_ref, s1_ref, extx, extbc, xs_ref, bcs_ref, *, C, c_real):
    c = pl.program_id(1)
    halo = SUBLANES

    @pl.when(c == 0)
    def _():
        s1_ref[...] = s0_ref[...]
        extx[...] = jnp.zeros(extx.shape, F32)
        extbc[...] = jnp.zeros(extbc.shape, F32)
        extx[0:halo, :] = buf_ref[0, :, 0:SSD_INNER]
        extbc[0:halo, :] = buf_ref[0, :, SSD_INNER:SSD_CONV_CH]

    extx[halo:halo + c_real, :] = x_ref[0].astype(F32)
    extbc[halo:halo + c_real, :] = bc_ref[0].astype(F32)

    cw = 512
    first = halo - (SSD_CONV - 1)
    for ext, dst, ch0 in ((extx, xs_ref, 0), (extbc, bcs_ref, SSD_INNER)):
        for col in range(0, SSD_INNER, cw):
            acc = cb_ref[:, ch0 + col:ch0 + col + cw]
            for tap in range(SSD_CONV):
                acc = acc + ext[first + tap:first + tap + C, col:col + cw] * cw_ref[tap:tap + 1, ch0 + col:ch0 + col + cw]
            dst[:, col:col + cw] = _silu(acc).astype(dst.dtype)
    extx[0:halo, :] = extx[C:C + halo, :]
    extbc[0:halo, :] = extbc[C:C + halo, :]

    row = lax.broadcasted_iota(jnp.int32, (C, 1), 0)
    col_i = lax.broadcasted_iota(jnp.int32, (1, C), 1)
    dt = jnp.where(row < c_real, _softplus(dt_ref[0, 0] + dtb_ref[...]), 0.0)
    dtt = jnp.where(col_i < c_real, _softplus(dtt_ref[0, 0] + dtbt_ref[...]), 0.0)
    cs = _cumsum(dt * (-jnp.exp(al_ref[...])), 0)
    cst = _cumsum(dtt * (-jnp.exp(alt_ref[...])), 1)
    causal = row >= col_i
    lane_r = lax.broadcasted_iota(jnp.int32, (1, SSD_GW), 1) // SSD_HEADDIM
    sub_r = lax.broadcasted_iota(jnp.int32, (SSD_GW, 1), 0) // SSD_HEADDIM

    def by_head(vals, sel):
        out = vals[SSD_HPG - 1]
        for r in range(SSD_HPG - 2, -1, -1):
            out = jnp.where(sel == r, vals[r], out)
        return out

    for g in range(SSD_GROUPS):
        bm = bcs_ref[:, g * SSD_STATE:(g + 1) * SSD_STATE]
        cm = bcs_ref[:, SSD_BC + g * SSD_STATE:SSD_BC + (g + 1) * SSD_STATE]
        cb = _dot_nt(cm, bm)
        h0 = s1_ref[0, g * SSD_HPG:(g + 1) * SSD_HPG].reshape(SSD_GW, SSD_STATE)
        xg = xs_ref[:, g * SSD_GW:(g + 1) * SSD_GW]
        heads = range(g * SSD_HPG, (g + 1) * SSD_HPG)
        cs_cols = [cs[:, hh:hh + 1] for hh in heads]
        cs_last = [cs[C - 1:C, hh:hh + 1] for hh in heads]
        yg = _dot_nt(cm, h0.astype(BF16)) * by_head([jnp.exp(cc) for cc in cs_cols], lane_r)
        ws, xs = [], []
        for r, hh in enumerate(heads):
            seg = cs_cols[r] - cst[hh:hh + 1, :]
            w = cb * jnp.exp(jnp.where(causal, seg, -jnp.inf)) * dtt[hh:hh + 1, :]
            ws.append(w.astype(BF16))
            xs.append(jnp.where(lane_r == r, xg, 0.0).astype(BF16))
        yg = yg + _dot(jnp.concatenate(ws, axis=1), jnp.concatenate(xs, axis=0))
        w_end = by_head([jnp.exp(cl - cc) * dt[:, hh:hh + 1] for cl, cc, hh in zip(cs_last, cs_cols, heads)], lane_r)
        h1 = by_head([jnp.exp(cl) for cl in cs_last], sub_r) * h0 + _dot_tn((xg * w_end).astype(BF16), bm)
        s1_ref[0, g * SSD_HPG:(g + 1) * SSD_HPG] = h1.reshape(SSD_HPG, SSD_HEADDIM, SSD_STATE)
        sl = slice(g * SSD_GW, (g + 1) * SSD_GW)
        yo = (yg[0:c_real] + dsk_ref[:, sl] * xg[0:c_real]) * _silu(z_ref[0, :, sl].astype(F32))
        yo = yo * lax.rsqrt(jnp.mean(yo * yo, axis=-1, keepdims=True) + EPS) * ng_ref[:, sl]
        y_ref[0, :, sl] = yo.astype(y_ref.dtype)


def ssd_core(zxbc, dt_raw, conv_buf, s0, conv_w, conv_b, dt_bias, a_log, d_skip, norm_g, c_real, out_dtype):
    b, l, _ = zxbc.shape
    C = SSD_CHUNK
    nc = l // c_real
    halo = SUBLANES
    dt4 = dt_raw.reshape(b, nc, c_real, LANES)
    dtt = jnp.swapaxes(dt4[..., :SSD_HEADS], -1, -2)
    if c_real < C:
        dt4 = jnp.pad(dt4, ((0, 0), (0, 0), (0, C - c_real), (0, 0)))
        dtt = jnp.pad(dtt, ((0, 0), (0, 0), (0, 0), (0, C - c_real)))
    lane_pad = LANES - SSD_HEADS
    buf8 = jnp.pad(conv_buf.astype(F32), ((0, 0), (halo - (SSD_CONV - 1), 0), (0, 0)))
    small = [
        conv_w.astype(F32), conv_b.reshape(1, -1).astype(F32),
        jnp.pad(dt_bias.astype(F32), (0, lane_pad)).reshape(1, LANES), dt_bias.astype(F32).reshape(SSD_HEADS, 1),
        jnp.pad(a_log.astype(F32), (0, lane_pad)).reshape(1, LANES), a_log.astype(F32).reshape(SSD_HEADS, 1),
        jnp.repeat(d_skip.astype(F32), SSD_HEADDIM).reshape(1, SSD_INNER), norm_g.astype(F32).reshape(1, SSD_INNER),
    ]
    w = SSD_INNER
    return pl.pallas_call(
        functools.partial(_ssd_kernel, C=C, c_real=c_real),
        grid=(b, nc),
        in_specs=[
            pl.BlockSpec((1, c_real, w), lambda bi, c: (bi, c, 0)),
            pl.BlockSpec((1, c_real, w), lambda bi, c: (bi, c, 1)),
            pl.BlockSpec((1, c_real, w), lambda bi, c: (bi, c, 2)),
            pl.BlockSpec((1, 1, C, LANES), lambda bi, c: (bi, c, 0, 0)),
            pl.BlockSpec((1, 1, SSD_HEADS, C), lambda bi, c: (bi, c, 0, 0)),
        ] + [_resident(a.shape) for a in small] + [
            pl.BlockSpec((1, halo, SSD_CONV_CH), lambda bi, c: (bi, 0, 0)),
            pl.BlockSpec((1, SSD_HEADS, SSD_HEADDIM, SSD_STATE), lambda bi, c: (bi, 0, 0, 0)),
        ],
        out_specs=[
            pl.BlockSpec((1, c_real, w), lambda bi, c: (bi, c, 0)),
            pl.BlockSpec((1, SSD_HEADS, SSD_HEADDIM, SSD_STATE), lambda bi, c: (bi, 0, 0, 0)),
        ],
        out_shape=[jax.ShapeDtypeStruct((b, l, SSD_INNER), out_dtype),
                   jax.ShapeDtypeStruct((b, SSD_HEADS, SSD_HEADDIM, SSD_STATE), F32)],
        scratch_shapes=[pltpu.VMEM((C + halo, w), F32), pltpu.VMEM((C + halo, w), F32),
                        pltpu.VMEM((C, w), F32), pltpu.VMEM((C, w), BF16)],
        compiler_params=_cparams(("parallel", "arbitrary")),
        name="ssd_core",
    )(zxbc, zxbc, zxbc, dt4, dtt, *small, buf8, s0)


def _t5_bias(dist, table_at):
    n = jnp.maximum(dist, 0)
    exact = REL_BUCKETS // 2
    nf = jnp.maximum(n, 1).astype(F32)
    large = exact + (jnp.log(nf / exact) / math.log(REL_MAX_DIST / exact) * (REL_BUCKETS - exact)).astype(jnp.int32)
    bucket = jnp.where(n < exact, n, jnp.minimum(large, REL_BUCKETS - 1))
    bias = jnp.zeros(dist.shape, F32)
    for bkt in range(REL_BUCKETS):
        bias = jnp.where(bucket == bkt, table_at(bkt), bias)
    return bias


def _lambda(lq1, lk1, lq2, lk2, lam_init):
    s1 = jnp.sum(lq1[...] * lk1[...], axis=-1, keepdims=True)
    s2 = jnp.sum(lq2[...] * lk2[...], axis=-1, keepdims=True)
    return jnp.exp(s1) - jnp.exp(s2) + lam_init


def _diff_in_proj_kernel(x_ref, g_ref, wq_ref, wkt_ref, wv_ref, q_ref, kt32_ref, kt16_ref, v32_ref, v16_ref, *, tn, kb):
    xn = _rms(x_ref[0], g_ref[...]).astype(BF16)
    d = wq_ref.shape[1]
    tm = xn.shape[0]
    for c0 in range(0, d, tn):
        q_ref[0, :, c0:c0 + tn] = (_dot(xn, wq_ref[:, c0:c0 + tn]) * (DIFF_DH ** -0.5)).astype(BF16)
        v = _dot(xn, wv_ref[:, c0:c0 + tn])
        v32_ref[0, :, c0:c0 + tn] = v
        v16_ref[0, :, c0:c0 + tn] = v.astype(BF16)
        kt = _dot_nt(wkt_ref[c0:c0 + tn, :], xn)
        kt32_ref[0, c0:c0 + tn, :] = kt
        for s in range(tm // kb):
            kt16_ref[0, s, c0:c0 + tn, :] = kt[:, s * kb:(s + 1) * kb].astype(BF16)


def diff_in_proj(x, g, wq, wkt, wv, tm, kb):
    b, l, d = x.shape
    assert l % tm == 0 and tm % kb == 0
    tok = lambda: pl.BlockSpec((1, tm, d), lambda bi, i: (bi, i, 0))
    return pl.pallas_call(
        functools.partial(_diff_in_proj_kernel, tn=512, kb=kb),
        grid=(b, l // tm),
        in_specs=[tok(), _resident((1, d)), _resident((d, d)), _resident((d, d)), _resident((d, d))],
        out_specs=[tok(), pl.BlockSpec((1, d, tm), lambda bi, i: (bi, 0, i)),
                   pl.BlockSpec((1, tm // kb, d, kb), lambda bi, i: (bi, i, 0, 0)), tok(), tok()],
        out_shape=[jax.ShapeDtypeStruct((b, l, d), BF16), jax.ShapeDtypeStruct((b, d, l), F32),
                   jax.ShapeDtypeStruct((b, l // kb, d, kb), BF16), jax.ShapeDtypeStruct((b, l, d), F32),
                   jax.ShapeDtypeStruct((b, l, d), BF16)],
        compiler_params=_cparams(("parallel", "parallel")),
        name="diff_in_proj",
    )(x, g.reshape(1, d), wq, wkt, wv)


def _attn_prompt_kernel(tbl_ref, q_ref, kt_ref, v_ref, lq1, lk1, lq2, lk2, sg_ref, o_ref,
                        bias_sc, m_sc, a_sc, *, T, lam_init):
    hd = pl.program_id(0)
    bi = pl.program_id(1)
    qi = pl.program_id(2)

    @pl.when((bi == 0) & (qi == 0))
    def _():
        i = lax.broadcasted_iota(jnp.int32, (T, 2 * T), 0)
        j = lax.broadcasted_iota(jnp.int32, (T, 2 * T), 1)
        dist = i - j + T
        bias_sc[...] = jnp.where(dist >= 0, _t5_bias(dist, lambda bkt: tbl_ref[bkt, hd]), -jnp.inf)

    lane = lax.broadcasted_iota(jnp.int32, (1, DIFF_DV), 1)
    q = q_ref[0]
    zero = jnp.zeros((), q.dtype)
    q2 = jnp.concatenate([jnp.where(lane < DIFF_DH, q, zero), jnp.where(lane >= DIFF_DH, q, zero)], axis=0)
    m_sc[...] = jnp.full(m_sc.shape, -jnp.inf, F32)
    a_sc[...] = jnp.zeros(a_sc.shape, F32)

    def update(kb, bias):
        vt = v_ref[0, pl.ds(pl.multiple_of(kb * T, T), T), :]
        vx = jnp.concatenate([vt, jnp.ones((T, DIFF_DV), BF16)], axis=1)
        s = _dot(q2, kt_ref[0, kb])
        s = s + (jnp.concatenate([bias, bias], axis=0) if getattr(bias, "ndim", 0) == 2 else bias)
        m_prev = m_sc[...]
        m_new = jnp.maximum(m_prev, jnp.max(s, axis=-1, keepdims=True))
        alpha = jnp.exp(m_prev - m_new)
        p = jnp.concatenate([jnp.exp(s[:, c:c + LANES] - m_new) for c in range(0, T, LANES)], axis=1)
        a_sc[...] = jnp.concatenate([alpha, alpha], axis=1) * a_sc[...] + _dot(p.astype(BF16), vx)
        m_sc[...] = m_new

    far_bias = tbl_ref[REL_BUCKETS - 1, hd]

    def far_body(kb, carry):
        update(kb, far_bias)
        return carry

    lax.fori_loop(0, jnp.maximum(qi - 1, 0), far_body, 0)

    @pl.when(qi >= 1)
    def _():
        update(qi - 1, bias_sc[:, 0:T])

    update(qi, bias_sc[:, T:2 * T])

    lam = _lambda(lq1, lk1, lq2, lk2, lam_init)
    o = a_sc[:T, :DIFF_DV] / a_sc[:T, DIFF_DV:] - lam * (a_sc[T:, :DIFF_DV] / a_sc[T:, DIFF_DV:])
    o_ref[0] = (_rms(o, sg_ref[...]) * (1.0 - lam_init)).astype(o_ref.dtype)


def diff_attention_prompt(q, kt, v, rel_table, lams, subln_g, lam_init, T):
    b, l, _ = q.shape
    assert T >= REL_MAX_DIST and l % T == 0 and DIFF_DV == 2 * DIFF_DH
    vec = lambda a: a.astype(F32).reshape(1, -1)
    return pl.pallas_call(
        functools.partial(_attn_prompt_kernel, T=T, lam_init=lam_init),
        grid=(DIFF_HEADS, b, l // T),
        in_specs=[
            pl.BlockSpec(memory_space=pltpu.SMEM),
            pl.BlockSpec((1, T, DIFF_DV), lambda h, bi, qi: (bi, qi, h)),
            pl.BlockSpec((1, l // T, DIFF_DV, T), lambda h, bi, qi: (bi, 0, h, 0)),
            pl.BlockSpec((1, l, DIFF_DV), lambda h, bi, qi: (bi, 0, h)),
        ] + [_resident((1, DIFF_DH))] * 4 + [_resident((1, DIFF_DV))],
        out_specs=pl.BlockSpec((1, T, DIFF_DV), lambda h, bi, qi: (bi, qi, h)),
        out_shape=jax.ShapeDtypeStruct((b, l, DIFF_HEADS * DIFF_DV), BF16),
        scratch_shapes=[pltpu.VMEM((T, 2 * T), F32), pltpu.VMEM((2 * T, LANES), F32), pltpu.VMEM((2 * T, 2 * DIFF_DV), F32)],
        compiler_params=_cparams(("arbitrary", "arbitrary", "arbitrary")),
        name="diff_attention_prompt",
    )(rel_table.astype(F32), q, kt, v, *[vec(a) for a in lams], vec(subln_g))


DEC_RPH = SUBLANES


def _attn_sample_kernel(pt_ref, q_ref, kn_ref, vn_ref, tb_ref, lq1, lk1, lq2, lk2, sg_ref, *rest,
                        pps, n_pages, lq, lam_init):
    kt_refs, v_refs = rest[:pps], rest[pps:2 * pps]
    o_ref, qm_sc, m_sc, l_sc, acc_sc = rest[2 * pps:]
    s = pl.program_id(1)
    n_steps = n_pages // pps
    past = n_pages * PAGE_SIZE
    rows = DIFF_HEADS * DEC_RPH
    width = DIFF_HEADS * DIFF_DV
    rid = lax.broadcasted_iota(jnp.int32, (rows, 1), 0)
    row_tok = (rid % DEC_RPH) // 2

    def q_rows():
        r8 = lax.broadcasted_iota(jnp.int32, (DEC_RPH, width), 0)
        lane_pair = lax.broadcasted_iota(jnp.int32, (DEC_RPH, width), 1) // DIFF_DH
        qrep = jnp.zeros((DEC_RPH, width), F32)
        for t in range(lq):
            qrep = jnp.where(r8 // 2 == t, q_ref[0, t:t + 1, :], qrep)
        qrep = qrep * (DIFF_DH ** -0.5)
        return jnp.concatenate([jnp.where(lane_pair == 2 * hh + r8 % 2, qrep, 0.0) for hh in range(DIFF_HEADS)], axis=0)

    @pl.when(s == 0)
    def _():
        qm_sc[...] = q_rows().astype(BF16)
        m_sc[...] = jnp.full(m_sc.shape, -jnp.inf, F32)
        l_sc[...] = jnp.zeros(l_sc.shape, F32)
        acc_sc[...] = jnp.zeros(acc_sc.shape, F32)

    far_bias = tb_ref[:, REL_BUCKETS - 1:REL_BUCKETS]

    def pages_update(last_near):
        qm = qm_sc[...]
        sc = []
        for i in range(pps):
            sci = _dot(qm, kt_refs[i][0].astype(BF16))
            if last_near and i == pps - 1:
                kpos = (n_pages - 1) * PAGE_SIZE + lax.broadcasted_iota(jnp.int32, (1, PAGE_SIZE), 1)
                sc.append(sci + _t5_bias((past + row_tok) - kpos, lambda bkt: tb_ref[:, bkt:bkt + 1]))
            else:
                sc.append(sci + far_bias)
        m_prev = m_sc[...]
        m_new = m_prev
        for sci in sc:
            m_new = jnp.maximum(m_new, jnp.max(sci, axis=-1, keepdims=True))
        alpha = jnp.exp(m_prev - m_new)
        p = [jnp.exp(sci - m_new) for sci in sc]
        l_new = alpha * l_sc[...]
        for pi in p:
            l_new = l_new + jnp.sum(pi, axis=-1, keepdims=True)
        l_sc[...] = l_new
        m_sc[...] = m_new
        for hh in range(DIFF_HEADS):
            sl = slice(hh * DEC_RPH, (hh + 1) * DEC_RPH)
            ph = jnp.concatenate([pi[sl] for pi in p], axis=1).astype(BF16)
            vh = jnp.concatenate([v_refs[i][0, pl.ds(hh, PAGE_SIZE, stride=DIFF_HEADS), :].astype(BF16)
                                  for i in range(pps)], axis=0)
            acc_sc[sl, :] = alpha[sl] * acc_sc[sl, :] + _dot(ph, vh)

    @pl.when(s < n_steps - 1)
    def _():
        pages_update(False)

    @pl.when(s == n_steps - 1)
    def _():
        pages_update(True)

    @pl.when(s == n_steps)
    def _():
        qf = q_rows()
        sj = []
        for jn in range(lq):
            dist = row_tok - jn
            sc = jnp.sum(qf * kn_ref[0, jn:jn + 1, :], axis=-1, keepdims=True)
            sc = sc + _t5_bias(dist, lambda bkt: tb_ref[:, bkt:bkt + 1])
            sj.append(jnp.where(dist >= 0, sc, -jnp.inf))
        m_prev = m_sc[...]
        m_new = m_prev
        for sc in sj:
            m_new = jnp.maximum(m_new, sc)
        alpha = jnp.exp(m_prev - m_new)
        l_new = alpha * l_sc[...]
        acc = alpha * acc_sc[...]
        for jn, sc in enumerate(sj):
            p = jnp.exp(sc - m_new)
            l_new = l_new + p
            vrow = jnp.concatenate(
                [jnp.broadcast_to(vn_ref[0, jn:jn + 1, hh * DIFF_DV:(hh + 1) * DIFF_DV], (DEC_RPH, DIFF_DV))
                 for hh in range(DIFF_HEADS)], axis=0)
            acc = acc + p * vrow

        lam = _lambda(lq1, lk1, lq2, lk2, lam_init)
        coef = jnp.where(rid % 2 == 0, 1.0, -lam) / l_new
        a = acc * coef
        a = a + pltpu.roll(a, rows - 1, 0)
        res = _rms(a, sg_ref[...]) * (1.0 - lam_init)
        for hh in range(DIFF_HEADS):
            for t in range(lq):
                r = hh * DEC_RPH + 2 * t
                o_ref[0, t:t + 1, hh * DIFF_DV:(hh + 1) * DIFF_DV] = res[r:r + 1, :]


def diff_attention_sample(q, k_new, v_new, cache_kt, cache_v, page_table, rel_table, lams, subln_g, lam_init, pps=16):
    b, lq, width = q.shape
    n_pages = page_table.shape[1]
    assert n_pages % pps == 0 and 2 * lq <= DEC_RPH and n_pages // pps >= 1
    n_steps = n_pages // pps
    rows = DIFF_HEADS * DEC_RPH
    head_of_row = np.arange(rows) // DEC_RPH
    tb = jnp.pad(rel_table.astype(F32).T[head_of_row], ((0, 0), (0, LANES - REL_BUCKETS)))
    vec = lambda a: a.astype(F32).reshape(1, -1)

    def page_map(i):
        return lambda bi, s, pt: (pt[bi * n_pages + jnp.minimum(s, n_steps - 1) * pps + i], 0, 0)

    tok = pl.BlockSpec((1, lq, width), lambda bi, s, pt: (bi, 0, 0))
    const = lambda shape: pl.BlockSpec(shape, lambda bi, s, pt: (0,) * len(shape))
    kpage = [pl.BlockSpec((1, width, PAGE_SIZE), page_map(i)) for i in range(pps)]
    vpage = [pl.BlockSpec((1, PAGE_SIZE * DIFF_HEADS, DIFF_DV), page_map(i)) for i in range(pps)]
    grid_spec = pltpu.PrefetchScalarGridSpec(
        num_scalar_prefetch=1,
        grid=(b, n_steps + 1),
        in_specs=[tok, tok, tok, const((rows, LANES))] + [const((1, DIFF_DH))] * 4 + [const((1, DIFF_DV))] + kpage + vpage,
        out_specs=tok,
        scratch_shapes=[pltpu.VMEM((rows, width), BF16), pltpu.VMEM((rows, 1), F32), pltpu.VMEM((rows, 1), F32),
                        pltpu.VMEM((rows, DIFF_DV), F32)],
    )
    return pl.pallas_call(
        functools.partial(_attn_sample_kernel, pps=pps, n_pages=n_pages, lq=lq, lam_init=lam_init),
        grid_spec=grid_spec,
        out_shape=jax.ShapeDtypeStruct((b, lq, width), F32),
        compiler_params=_cparams(("parallel", "arbitrary")),
        name="diff_attention_sample",
    )(page_table.reshape(-1).astype(jnp.int32), q, k_new, v_new, tb, *[vec(a) for a in lams], vec(subln_g),
      *([cache_kt] * pps), *([cache_v] * pps))


def kernel(x_prompt, x_sample, state_ret, state_ssm, state_conv, cache_k_diff, cache_v_diff, page_table, norm_mix_g, norm_ffn_g, norm_final_g, ret_w_in, ret_w_out, ssd_w_in, ssd_conv_w, ssd_conv_b, ssd_dt_bias, ssd_A_log, ssd_D, ssd_norm_g, ssd_w_out, diff_w_in, diff_lam_q1, diff_lam_k1, diff_lam_q2, diff_lam_k2, diff_subln_g, diff_w_out, rel_bias_table, ffn_w_in, ffn_w_out):
    kinds = tuple(i % 3 for i in range(DEPTH))
    n_pages = page_table.shape[1]
    bf = lambda a: a.astype(BF16)
    ret_wi, ret_wo = bf(ret_w_in), bf(ret_w_out)
    ssd_wmain = bf(ssd_w_in[:, :, :SSD_INNER + SSD_CONV_CH])
    ssd_wxbc = ssd_wmain[:, :, SSD_INNER:]
    ssd_wdt = bf(jnp.pad(ssd_w_in[:, :, SSD_INNER + SSD_CONV_CH:], ((0, 0), (0, 0), (0, LANES - SSD_HEADS))))
    ssd_wo = bf(ssd_w_out)
    diff_wi, diff_wo = bf(diff_w_in), bf(diff_w_out)
    ffn_wi, ffn_wo = bf(ffn_w_in), bf(ffn_w_out)

    n_ret = kinds.count(0)

    def run_group(x, sample):
        b, l, d = x.shape
        t = b * l
        tm = 512 if t % 512 == 0 else t
        ret_c = RET_CHUNK if l % RET_CHUNK == 0 else l
        ssd_c = SSD_CHUNK if l % SSD_CHUNK == 0 else l
        act = F32 if sample else BF16
        offset = n_pages * PAGE_SIZE if sample else 0
        rope_rows = l if l % RET_CHUNK == 0 else -(-l // BF16_ROWS) * BF16_ROWS
        cos, sin = rope_tables(offset, rope_rows)
        ret_new, ssm_new, conv_new, k_new, v_new = None, [], [], [], []
        h = x.reshape(t, d)
        for i in range(DEPTH):
            kind = kinds[i]
            j = kinds[:i].count(kind)
            g = norm_mix_g[i]
            if kind == 0:
                n_in = ret_wi.shape[2]
                (qkvg,) = norm_linear(h, g, ret_wi, j, [(0, n_in, act)], tm)
                s0, j0 = (state_ret, j) if sample else (jnp.zeros((1, b, RET_HEADS, RET_DK, RET_DV), F32), 0)
                y, ret_new = retention_core(qkvg.reshape(b, l, n_in), cos, sin, s0, j0, ret_new, j, n_ret, ret_c, act)
                y, wmix = y.reshape(t, RET_VDIM), ret_wo
            elif kind == 1:
                n_in = SSD_INNER + SSD_CONV_CH
                zxbc, dt_raw = norm_linear(h, g, ssd_wmain, j, [(0, n_in, act)], tm, side=[(ssd_wdt, F32)])
                zxbc = zxbc.reshape(b, l, n_in)
                if sample:
                    buf, s0 = state_conv[j], state_ssm[j]
                    conv_new.append(zxbc[:, l - (SSD_CONV - 1):, SSD_INNER:])
                else:
                    buf = jnp.zeros((b, SSD_CONV - 1, SSD_CONV_CH), F32)
                    s0 = jnp.zeros((b, SSD_HEADS, SSD_HEADDIM, SSD_STATE), F32)
                    tail = h.reshape(b, l, d)[:, l - (SSD_CONV - 1):].reshape(b * (SSD_CONV - 1), d)
                    rows = -(-tail.shape[0] // BF16_ROWS) * BF16_ROWS
                    tail = jnp.pad(tail, ((0, rows - tail.shape[0]), (0, 0)))
                    (xbc_tail,) = norm_linear(tail, g, ssd_wxbc, j, [(0, SSD_CONV_CH, F32)], rows)
                    conv_new.append(xbc_tail[:b * (SSD_CONV - 1)].reshape(b, SSD_CONV - 1, SSD_CONV_CH))
                y, s1 = ssd_core(zxbc, dt_raw.reshape(b, l, LANES), buf, s0, ssd_conv_w[j], ssd_conv_b[j],
                                 ssd_dt_bias[j], ssd_A_log[j], ssd_D[j], ssd_norm_g[j], ssd_c, act)
                ssm_new.append(s1)
                y, wmix = y.reshape(t, SSD_INNER), ssd_wo
            else:
                lam_init = 0.8 - 0.6 * math.exp(-0.3 * i)
                lams = (diff_lam_q1[j], diff_lam_k1[j], diff_lam_q2[j], diff_lam_k2[j])
                if sample:
                    q, kn, vn = norm_linear(h, g, diff_wi, j, [(0, d, F32), (d, 2 * d, F32), (2 * d, 3 * d, F32)], tm)
                    cache_kt = jnp.transpose(cache_k_diff[j], (0, 2, 3, 4, 1)).reshape(-1, d, PAGE_SIZE)
                    cache_v = cache_v_diff[j].reshape(-1, PAGE_SIZE * DIFF_HEADS, DIFF_DV)
                    y = diff_attention_sample(q.reshape(b, l, d), kn.reshape(b, l, d), vn.reshape(b, l, d), cache_kt,
                                              cache_v, page_table, rel_bias_table, lams, diff_subln_g[j], lam_init)
                    k_new.append(kn.reshape(b, l, DIFF_HEADS, 2, DIFF_DH))
                    v_new.append(vn.reshape(b, l, DIFF_HEADS, DIFF_DV))
                else:
                    w = diff_wi[j]
                    q, kt32, kt16, v32, v16 = diff_in_proj(h.reshape(b, l, d), g, w[:, :d], w[:, d:2 * d].T, w[:, 2 * d:],
                                                           tm, ATTN_TILE)
                    y = diff_attention_prompt(q, kt16, v16, rel_bias_table, lams, diff_subln_g[j], lam_init, ATTN_TILE)
                    k_new.append(jnp.transpose(kt32.reshape(b, DIFF_HEADS, 2, DIFF_DH, l), (0, 4, 1, 2, 3)))
                    v_new.append(v32.reshape(b, l, DIFF_HEADS, DIFF_DV))
                y, wmix = y.reshape(t, d), diff_wo
            h = out_ffn(y, wmix, j, h, norm_ffn_g[i], ffn_wi, ffn_wo, i, tm,
                        g_final=norm_final_g if i == DEPTH - 1 else None)
        return h.reshape(b, l, d), ret_new, jnp.stack(ssm_new), jnp.stack(conv_new), jnp.stack(k_new), jnp.stack(v_new)

    y_p, ret_p, ssm_p, conv_p, k_p, v_p = run_group(x_prompt, False)
    y_s, ret_s, ssm_s, conv_s, k_s, v_s = run_group(x_sample, True)
    return (y_p, y_s, ret_p, ssm_p, conv_p, k_p, v_p, ret_s, ssm_s, conv_s, k_s, v_s)
```

```python
import functools
import math

import jax
import jax.numpy as jnp
import numpy as np
from jax import lax
from jax.experimental import pallas as pl
from jax.experimental.pallas import tpu as pltpu

F32 = jnp.float32
BF16 = jnp.bfloat16

D_MODEL = 1024
DEPTH = 4
PAGE_SIZE = 128
EPS = 1e-6
RET_CHUNK = 256
SSD_CHUNK = 128
ATTN_TILE = 512

RET_HEADS = 4
RET_DK = D_MODEL // RET_HEADS
RET_DV = 2 * RET_DK
RET_VDIM = RET_HEADS * RET_DV
ROPE_BASE = 10000.0
ROPE_HALF = RET_DK // 2

SSD_INNER = 2 * D_MODEL
SSD_HEADDIM = 64
SSD_HEADS = SSD_INNER // SSD_HEADDIM
SSD_GROUPS = 8
SSD_HPG = SSD_HEADS // SSD_GROUPS
SSD_STATE = 128
SSD_CONV = 4
SSD_BC = SSD_GROUPS * SSD_STATE
SSD_CONV_CH = SSD_INNER + 2 * SSD_BC
SSD_GW = SSD_HPG * SSD_HEADDIM

DIFF_HEADS = 8
DIFF_DH = D_MODEL // DIFF_HEADS // 2
DIFF_DV = 2 * DIFF_DH
REL_BUCKETS = 32
REL_MAX_DIST = 128

FFN_HIDDEN = -(-8 * D_MODEL // (3 * 256)) * 256

V7X_VMEM_BYTES = 64 * 1024 * 1024
LANES = 128
SUBLANES = 8
BF16_ROWS = 16
VMEM_LIMIT = V7X_VMEM_BYTES * 7 // 8


def _cparams(sem):
    return pltpu.CompilerParams(dimension_semantics=sem, vmem_limit_bytes=VMEM_LIMIT)


def _dot(a, b):
    return jnp.dot(a, b, preferred_element_type=F32)


def _dot_nt(a, b):
    return lax.dot_general(a, b, (((1,), (1,)), ((), ())), preferred_element_type=F32)


def _dot_tn(a, b):
    return lax.dot_general(a, b, (((0,), (0,)), ((), ())), preferred_element_type=F32)


def _rms(x, g):
    return x * lax.rsqrt(jnp.mean(x * x, axis=-1, keepdims=True) + EPS) * g


def _silu(x):
    half = 0.5 * x
    return half * jnp.tanh(half) + half


def _resident(shape):
    return pl.BlockSpec(shape, lambda *_: (0,) * len(shape), pipeline_mode=pl.Buffered(1))


def _resident_layer(stacked, layer):
    tail = stacked.shape[1:]
    return pl.BlockSpec((None,) + tail, lambda *_: (layer,) + (0,) * len(tail), pipeline_mode=pl.Buffered(1))


def _norm_linear_kernel(x_ref, g_ref, w_ref, *rest, outs, tn, side):
    side_w = rest[:side]
    o_refs = rest[side:side + len(outs)]
    side_o = rest[side + len(outs):]
    xn = _rms(x_ref[...], g_ref[...]).astype(BF16)
    n = w_ref.shape[1]
    for c0 in range(0, n, tn):
        acc = _dot(xn, w_ref[:, c0:c0 + tn])
        for o_ref, (lo, hi, _) in zip(o_refs, outs):
            if lo <= c0 and c0 + tn <= hi:
                o_ref[:, c0 - lo:c0 - lo + tn] = acc.astype(o_ref.dtype)
    for w2_ref, o2_ref in zip(side_w, side_o):
        o2_ref[...] = _dot(xn, w2_ref[...]).astype(o2_ref.dtype)


def norm_linear(x, g, w_stack, layer, outs, tm, tn=512, side=()):
    t, d = x.shape
    n = w_stack.shape[2]
    tn = min(tn, n)
    assert t % tm == 0 and n % tn == 0 and all(lo % tn == 0 and hi % tn == 0 for lo, hi, _ in outs)
    row = lambda width: pl.BlockSpec((tm, width), lambda i: (i, 0))
    return pl.pallas_call(
        functools.partial(_norm_linear_kernel, outs=tuple(outs), tn=tn, side=len(side)),
        grid=(t // tm,),
        in_specs=[row(d), _resident((1, d)), _resident_layer(w_stack, layer)] + [_resident_layer(w2, layer) for w2, _ in side],
        out_specs=[row(hi - lo) for lo, hi, _ in outs] + [row(w2.shape[2]) for w2, _ in side],
        out_shape=[jax.ShapeDtypeStruct((t, hi - lo), dt) for lo, hi, dt in outs]
        + [jax.ShapeDtypeStruct((t, w2.shape[2]), dt) for w2, dt in side],
        compiler_params=_cparams(("parallel",)),
        name="norm_linear",
    )(x, g.reshape(1, d), w_stack, *[w2 for w2, _ in side])


def _out_ffn_kernel(y_ref, wmix_ref, h_ref, g_ref, wgu_ref, wo_ref, *rest, th, final):
    o_ref = rest[-1]
    h = h_ref[...] + _dot(y_ref[...].astype(BF16), wmix_ref[...])
    xn = _rms(h, g_ref[...]).astype(BF16)
    hidden = wo_ref.shape[0]
    o_ref[...] = h
    for c0 in range(0, hidden, th):
        gate = _dot(xn, wgu_ref[:, c0:c0 + th])
        up = _dot(xn, wgu_ref[:, hidden + c0:hidden + c0 + th])
        act = (_silu(gate) * up).astype(BF16)
        o_ref[...] += _dot(act, wo_ref[c0:c0 + th, :])
    if final:
        o_ref[...] = _rms(o_ref[...], rest[0][...])


def out_ffn(y, wmix_stack, j, h, g, wgu_stack, wo_stack, i, tm, g_final=None, th=256):
    t, k = y.shape
    d = h.shape[1]
    hidden = wo_stack.shape[1]
    assert hidden % th == 0 and t % tm == 0
    final = g_final is not None
    extra = [g_final.reshape(1, d)] if final else []
    return pl.pallas_call(
        functools.partial(_out_ffn_kernel, th=th, final=final),
        grid=(t // tm,),
        in_specs=[pl.BlockSpec((tm, k), lambda m: (m, 0)), _resident_layer(wmix_stack, j),
                  pl.BlockSpec((tm, d), lambda m: (m, 0)), _resident((1, d)), _resident_layer(wgu_stack, i),
                  _resident_layer(wo_stack, i)] + [_resident((1, d))] * len(extra),
        out_specs=pl.BlockSpec((tm, d), lambda m: (m, 0)),
        out_shape=jax.ShapeDtypeStruct((t, d), F32),
        compiler_params=_cparams(("parallel",)),
        name="out_ffn",
    )(y, wmix_stack, h, g.reshape(1, d), wgu_stack, wo_stack, *extra)


def _rope_kernel(inv_ref, cos_ref, sin_ref, *, offset):
    rows = cos_ref.shape[0]
    pos = (offset + pl.program_id(0) * rows + lax.broadcasted_iota(jnp.int32, (rows, 1), 0)).astype(F32)
    ang = pos * inv_ref[...]
    cos_ref[...] = jnp.cos(ang)
    sin_ref[...] = jnp.sin(ang)


def rope_tables(offset, rows):
    inv = (1.0 / (ROPE_BASE ** jnp.linspace(0.0, 1.0, ROPE_HALF, dtype=F32))).reshape(1, ROPE_HALF)
    tr = min(rows, 512)
    return pl.pallas_call(
        functools.partial(_rope_kernel, offset=offset),
        grid=(rows // tr,),
        in_specs=[_resident((1, ROPE_HALF))],
        out_specs=[pl.BlockSpec((tr, ROPE_HALF), lambda i: (i, 0))] * 2,
        out_shape=[jax.ShapeDtypeStruct((rows, ROPE_HALF), F32)] * 2,
        compiler_params=_cparams(("parallel",)),
        name="rope_tables",
    )(inv)


def _ret_kernel(q_ref, k_ref, v_ref, g_ref, cos_ref, sin_ref, s0_ref, *rest, C, c_real, carry, slot, n_chunks):
    y_ref, st_ref = rest[carry:carry + 2]
    decay_sc = rest[carry + 2]
    stage = rest[carry + 3:]
    c = pl.program_id(1)
    log_gamma = [math.log(1.0 - 2.0 ** (-5.0 - hd)) for hd in range(RET_HEADS)]
    i = lax.broadcasted_iota(jnp.int32, (C, 1), 0).astype(F32)
    j = lax.broadcasted_iota(jnp.int32, (1, C), 1).astype(F32)

    @pl.when(c == 0)
    def _():
        st_ref[slot] = s0_ref[...]
        rel = i - j
        for hd in range(RET_HEADS):
            decay_sc[hd] = jnp.where(rel >= 0, jnp.exp(jnp.maximum(rel, 0.0) * log_gamma[hd]), 0.0)

    if c_real == C:
        q_src, k_src, v_src = q_ref.at[0], k_ref.at[0], v_ref.at[0]
    else:
        for ref, st in zip((q_ref, k_ref, v_ref), stage):
            st[...] = jnp.zeros(st.shape, F32)
            st[0:c_real, :] = ref[0].astype(F32)
        q_src, k_src, v_src = stage
    cos = cos_ref[...]
    sin = sin_ref[...]

    def rot(x):
        x1, x2 = x[:, :ROPE_HALF], x[:, ROPE_HALF:]
        return jnp.concatenate([x1 * cos - x2 * sin, x2 * cos + x1 * sin], axis=-1)

    for hd in range(RET_HEADS):
        lg = log_gamma[hd]
        ks = slice(hd * RET_DK, (hd + 1) * RET_DK)
        vs = slice(hd * RET_DV, (hd + 1) * RET_DV)
        qr = rot(q_src[:, ks].astype(F32)).astype(BF16)
        kr = rot(k_src[:, ks].astype(F32)) * (RET_DK ** -0.5)
        v = v_src[:, vs].astype(BF16)
        scores = _dot_nt(qr, kr.astype(BF16)) * decay_sc[hd]
        s_prev = st_ref[slot, 0, hd]
        o = _dot(scores.astype(BF16), v) + _dot(qr, s_prev.astype(BF16)) * jnp.exp((i + 1.0) * lg)
        kw = (kr * jnp.exp((c_real - 1.0 - i) * lg)).astype(BF16)
        st_ref[slot, 0, hd] = math.exp(c_real * lg) * s_prev + _dot_tn(kw, v)
        o = o * lax.rsqrt(jnp.mean(o * o, axis=-1, keepdims=True) + EPS)
        y_ref[0, :, vs] = (_silu(g_ref[0, :, vs].astype(F32)) * o[0:c_real]).astype(y_ref.dtype)

    if st_ref.shape[0] > 1:
        @pl.when(c == n_chunks - 1)
        def _():
            for other in range(st_ref.shape[0]):
                if other != slot:
                    st_ref[other] = st_ref[slot]


def retention_core(qkvg, cos, sin, s0_stack, j0, s1_stack, j1, n_layers, c_real, out_dtype):
    b, l, _ = qkvg.shape
    C = cos.shape[0] if c_real < RET_CHUNK else RET_CHUNK
    nc = l // c_real
    qk_w, v_w = RET_HEADS * RET_DK, RET_VDIM
    k_blk, v_blk, g_blk = 1, 2 * qk_w // v_w, 2 * qk_w // v_w + 1
    stage = [] if c_real == C else [pltpu.VMEM((C, qk_w), F32), pltpu.VMEM((C, qk_w), F32), pltpu.VMEM((C, v_w), F32)]
    carry = s1_stack is not None
    st_block, st_first, slot = (1, j1, 0) if carry else (n_layers, 0, j1)
    return pl.pallas_call(
        functools.partial(_ret_kernel, C=C, c_real=c_real, carry=int(carry), slot=slot, n_chunks=nc),
        grid=(b, nc),
        in_specs=[
            pl.BlockSpec((1, c_real, qk_w), lambda bi, c: (bi, c, 0)),
            pl.BlockSpec((1, c_real, qk_w), lambda bi, c: (bi, c, k_blk)),
            pl.BlockSpec((1, c_real, v_w), lambda bi, c: (bi, c, v_blk)),
            pl.BlockSpec((1, c_real, v_w), lambda bi, c: (bi, c, g_blk)),
            pl.BlockSpec((C, ROPE_HALF), lambda bi, c: (c, 0)),
            pl.BlockSpec((C, ROPE_HALF), lambda bi, c: (c, 0)),
            pl.BlockSpec((None, 1, RET_HEADS, RET_DK, RET_DV), lambda bi, c: (j0, bi, 0, 0, 0)),
        ] + ([pl.BlockSpec(memory_space=pl.ANY)] if carry else []),
        out_specs=[
            pl.BlockSpec((1, c_real, v_w), lambda bi, c: (bi, c, 0)),
            pl.BlockSpec((st_block, 1, RET_HEADS, RET_DK, RET_DV), lambda bi, c: (st_first, bi, 0, 0, 0)),
        ],
        out_shape=[jax.ShapeDtypeStruct((b, l, RET_VDIM), out_dtype),
                   jax.ShapeDtypeStruct((n_layers, b, RET_HEADS, RET_DK, RET_DV), F32)],
        scratch_shapes=[pltpu.VMEM((RET_HEADS, C, C), F32)] + stage,
        input_output_aliases={7: 1} if carry else {},
        compiler_params=_cparams(("parallel", "arbitrary")),
        name="retention_core",
    )(qkvg, qkvg, qkvg, qkvg, cos, sin, s0_stack, *([s1_stack] if carry else []))


def _softplus(x):
    return jnp.maximum(x, 0.0) + jnp.log1p(jnp.exp(-jnp.abs(x)))


def _cumsum(x, axis):
    n = x.shape[axis]
    idx = lax.broadcasted_iota(jnp.int32, x.shape, axis)
    s = 1
    while s < n:
        x = x + jnp.where(idx >= s, pltpu.roll(x, s, axis), 0.0)
        s *= 2
    return x


def _ssd_kernel(z_ref, x_ref, bc_ref, dt_ref, dtt_ref, cw_ref, cb_ref, dtb_ref, dtbt_ref, al_ref, alt_ref,
                dsk_ref, ng_ref, buf_ref, s0_ref, y_ref, s1_ref, extx, extbc, xs_ref, bcs_ref, *, C, c_real):
    c = pl.program_id(1)
    halo = SUBLANES

    @pl.when(c == 0)
    def _():
        s1_ref[...] = s0_ref[...]
        extx[...] = jnp.zeros(extx.shape, F32)
        extbc[...] = jnp.zeros(extbc.shape, F32)
        extx[0:halo, :] = buf_ref[0, :, 0:SSD_INNER]
        extbc[0:halo, :] = buf_ref[0, :, SSD_INNER:SSD_CONV_CH]

    extx[halo:halo + c_real, :] = x_ref[0].astype(F32)
    extbc[halo:halo + c_real, :] = bc_ref[0].astype(F32)

    cw = 512
    first = halo - (SSD_CONV - 1)
    for ext, dst, ch0 in ((extx, xs_ref, 0), (extbc, bcs_ref, SSD_INNER)):
        for col in range(0, SSD_INNER, cw):
            acc = cb_ref[:, ch0 + col:ch0 + col + cw]
            for tap in range(SSD_CONV):
                acc = acc + ext[first + tap:first + tap + C, col:col + cw] * cw_ref[tap:tap + 1, ch0 + col:ch0 + col + cw]
            dst[:, col:col + cw] = _silu(acc).astype(dst.dtype)
    extx[0:halo, :] = extx[C:C + halo, :]
    extbc[0:halo, :] = extbc[C:C + halo, :]

    row = lax.broadcasted_iota(jnp.int32, (C, 1), 0)
    col_i = lax.broadcasted_iota(jnp.int32, (1, C), 1)
    dt = jnp.where(row < c_real, _softplus(dt_ref[0, 0] + dtb_ref[...]), 0.0)
    dtt = jnp.where(col_i < c_real, _softplus(dtt_ref[0, 0] + dtbt_ref[...]), 0.0)
    cs = _cumsum(dt * (-jnp.exp(al_ref[...])), 0)
    cst = _cumsum(dtt * (-jnp.exp(alt_ref[...])), 1)
    causal = row >= col_i
    lane_r = lax.broadcasted_iota(jnp.int32, (1, SSD_GW), 1) // SSD_HEADDIM
    sub_r = lax.broadcasted_iota(jnp.int32, (SSD_GW, 1), 0) // SSD_HEADDIM

    def by_head(vals, sel):
        out = vals[SSD_HPG - 1]
        for r in range(SSD_HPG - 2, -1, -1):
            out = jnp.where(sel == r, vals[r], out)
        return out

    for g in range(SSD_GROUPS):
        bm = bcs_ref[:, g * SSD_STATE:(g + 1) * SSD_STATE]
        cm = bcs_ref[:, SSD_BC + g * SSD_STATE:SSD_BC + (g + 1) * SSD_STATE]
        cb = _dot_nt(cm, bm)
        h0 = s1_ref[0, g * SSD_HPG:(g + 1) * SSD_HPG].reshape(SSD_GW, SSD_STATE)
        xg = xs_ref[:, g * SSD_GW:(g + 1) * SSD_GW]
        heads = range(g * SSD_HPG, (g + 1) * SSD_HPG)
        cs_cols = [cs[:, hh:hh + 1] for hh in heads]
        cs_last = [cs[C - 1:C, hh:hh + 1] for hh in heads]
        yg = _dot_nt(cm, h0.astype(BF16)) * by_head([jnp.exp(cc) for cc in cs_cols], lane_r)
        for r, hh in enumerate(heads):
            seg = cs_cols[r] - cst[hh:hh + 1, :]
            w = cb * jnp.exp(jnp.where(causal, seg, -jnp.inf)) * dtt[hh:hh + 1, :]
            yg = yg + _dot(w.astype(BF16), jnp.where(lane_r == r, xg, 0.0).astype(BF16))
        w_end = by_head([jnp.exp(cl - cc) * dt[:, hh:hh + 1] for cl, cc, hh in zip(cs_last, cs_cols, heads)], lane_r)
        h1 = by_head([jnp.exp(cl) for cl in cs_last], sub_r) * h0 + _dot_tn((xg * w_end).astype(BF16), bm)
        s1_ref[0, g * SSD_HPG:(g + 1) * SSD_HPG] = h1.reshape(SSD_HPG, SSD_HEADDIM, SSD_STATE)
        sl = slice(g * SSD_GW, (g + 1) * SSD_GW)
        yo = (yg[0:c_real] + dsk_ref[:, sl] * xg[0:c_real]) * _silu(z_ref[0, :, sl].astype(F32))
        yo = yo * lax.rsqrt(jnp.mean(yo * yo, axis=-1, keepdims=True) + EPS) * ng_ref[:, sl]
        y_ref[0, :, sl] = yo.astype(y_ref.dtype)


def ssd_core(zxbc, dt_raw, conv_buf, s0, conv_w, conv_b, dt_bias, a_log, d_skip, norm_g, c_real, out_dtype):
    b, l, _ = zxbc.shape
    C = SSD_CHUNK
    nc = l // c_real
    halo = SUBLANES
    dt4 = dt_raw.reshape(b, nc, c_real, LANES)
    dtt = jnp.swapaxes(dt4[..., :SSD_HEADS], -1, -2)
    if c_real < C:
        dt4 = jnp.pad(dt4, ((0, 0), (0, 0), (0, C - c_real), (0, 0)))
        dtt = jnp.pad(dtt, ((0, 0), (0, 0), (0, 0), (0, C - c_real)))
    lane_pad = LANES - SSD_HEADS
    buf8 = jnp.pad(conv_buf.astype(F32), ((0, 0), (halo - (SSD_CONV - 1), 0), (0, 0)))
    small = [
        conv_w.astype(F32), conv_b.reshape(1, -1).astype(F32),
        jnp.pad(dt_bias.astype(F32), (0, lane_pad)).reshape(1, LANES), dt_bias.astype(F32).reshape(SSD_HEADS, 1),
        jnp.pad(a_log.astype(F32), (0, lane_pad)).reshape(1, LANES), a_log.astype(F32).reshape(SSD_HEADS, 1),
        jnp.repeat(d_skip.astype(F32), SSD_HEADDIM).reshape(1, SSD_INNER), norm_g.astype(F32).reshape(1, SSD_INNER),
    ]
    w = SSD_INNER
    return pl.pallas_call(
        functools.partial(_ssd_kernel, C=C, c_real=c_real),
        grid=(b, nc),
        in_specs=[
            pl.BlockSpec((1, c_real, w), lambda bi, c: (bi, c, 0)),
            pl.BlockSpec((1, c_real, w), lambda bi, c: (bi, c, 1)),
            pl.BlockSpec((1, c_real, w), lambda bi, c: (bi, c, 2)),
            pl.BlockSpec((1, 1, C, LANES), lambda bi, c: (bi, c, 0, 0)),
            pl.BlockSpec((1, 1, SSD_HEADS, C), lambda bi, c: (bi, c, 0, 0)),
        ] + [_resident(a.shape) for a in small] + [
            pl.BlockSpec((1, halo, SSD_CONV_CH), lambda bi, c: (bi, 0, 0)),
            pl.BlockSpec((1, SSD_HEADS, SSD_HEADDIM, SSD_STATE), lambda bi, c: (bi, 0, 0, 0)),
        ],
        out_specs=[
            pl.BlockSpec((1, c_real, w), lambda bi, c: (bi, c, 0)),
            pl.BlockSpec((1, SSD_HEADS, SSD_HEADDIM, SSD_STATE), lambda bi, c: (bi, 0, 0, 0)),
        ],
        out_shape=[jax.ShapeDtypeStruct((b, l, SSD_INNER), out_dtype),
                   jax.ShapeDtypeStruct((b, SSD_HEADS, SSD_HEADDIM, SSD_STATE), F32)],
        scratch_shapes=[pltpu.VMEM((C + halo, w), F32), pltpu.VMEM((C + halo, w), F32),
                        pltpu.VMEM((C, w), F32), pltpu.VMEM((C, w), BF16)],
        compiler_params=_cparams(("parallel", "arbitrary")),
        name="ssd_core",
    )(zxbc, zxbc, zxbc, dt4, dtt, *small, buf8, s0)


def _t5_bias(dist, table_at):
    n = jnp.maximum(dist, 0)
    exact = REL_BUCKETS // 2
    nf = jnp.maximum(n, 1).astype(F32)
    large = exact + (jnp.log(nf / exact) / math.log(REL_MAX_DIST / exact) * (REL_BUCKETS - exact)).astype(jnp.int32)
    bucket = jnp.where(n < exact, n, jnp.minimum(large, REL_BUCKETS - 1))
    bias = jnp.zeros(dist.shape, F32)
    for bkt in range(REL_BUCKETS):
        bias = jnp.where(bucket == bkt, table_at(bkt), bias)
    return bias


def _lambda(lq1, lk1, lq2, lk2, lam_init):
    s1 = jnp.sum(lq1[...] * lk1[...], axis=-1, keepdims=True)
    s2 = jnp.sum(lq2[...] * lk2[...], axis=-1, keepdims=True)
    return jnp.exp(s1) - jnp.exp(s2) + lam_init


def _diff_in_proj_kernel(x_ref, g_ref, wq_ref, wkt_ref, wv_ref, q_ref, kt32_ref, kt16_ref, v32_ref, v16_ref, *, tn, kb):
    xn = _rms(x_ref[0], g_ref[...]).astype(BF16)
    d = wq_ref.shape[1]
    tm = xn.shape[0]
    for c0 in range(0, d, tn):
        q_ref[0, :, c0:c0 + tn] = (_dot(xn, wq_ref[:, c0:c0 + tn]) * (DIFF_DH ** -0.5)).astype(BF16)
        v = _dot(xn, wv_ref[:, c0:c0 + tn])
        v32_ref[0, :, c0:c0 + tn] = v
        v16_ref[0, :, c0:c0 + tn] = v.astype(BF16)
        kt = _dot_nt(wkt_ref[c0:c0 + tn, :], xn)
        kt32_ref[0, c0:c0 + tn, :] = kt
        for s in range(tm // kb):
            kt16_ref[0, s, c0:c0 + tn, :] = kt[:, s * kb:(s + 1) * kb].astype(BF16)


def diff_in_proj(x, g, wq, wkt, wv, tm, kb):
    b, l, d = x.shape
    assert l % tm == 0 and tm % kb == 0
    tok = lambda: pl.BlockSpec((1, tm, d), lambda bi, i: (bi, i, 0))
    return pl.pallas_call(
        functools.partial(_diff_in_proj_kernel, tn=512, kb=kb),
        grid=(b, l // tm),
        in_specs=[tok(), _resident((1, d)), _resident((d, d)), _resident((d, d)), _resident((d, d))],
        out_specs=[tok(), pl.BlockSpec((1, d, tm), lambda bi, i: (bi, 0, i)),
                   pl.BlockSpec((1, tm // kb, d, kb), lambda bi, i: (bi, i, 0, 0)), tok(), tok()],
        out_shape=[jax.ShapeDtypeStruct((b, l, d), BF16), jax.ShapeDtypeStruct((b, d, l), F32),
                   jax.ShapeDtypeStruct((b, l // kb, d, kb), BF16), jax.ShapeDtypeStruct((b, l, d), F32),
                   jax.ShapeDtypeStruct((b, l, d), BF16)],
        compiler_params=_cparams(("parallel", "parallel")),
        name="diff_in_proj",
    )(x, g.reshape(1, d), wq, wkt, wv)


def _attn_prompt_kernel(tbl_ref, q_ref, kt_ref, v_ref, lq1, lk1, lq2, lk2, sg_ref, o_ref,
                        bias_sc, m_sc, a_sc, *, T, lam_init):
    hd = pl.program_id(0)
    bi = pl.program_id(1)
    qi = pl.program_id(2)

    @pl.when((bi == 0) & (qi == 0))
    def _():
        i = lax.broadcasted_iota(jnp.int32, (T, 2 * T), 0)
        j = lax.broadcasted_iota(jnp.int32, (T, 2 * T), 1)
        dist = i - j + T
        bias_sc[...] = jnp.where(dist >= 0, _t5_bias(dist, lambda bkt: tbl_ref[bkt, hd]), -jnp.inf)

    lane = lax.broadcasted_iota(jnp.int32, (1, DIFF_DV), 1)
    q = q_ref[0]
    zero = jnp.zeros((), q.dtype)
    q2 = jnp.concatenate([jnp.where(lane < DIFF_DH, q, zero), jnp.where(lane >= DIFF_DH, q, zero)], axis=0)
    m_sc[...] = jnp.full(m_sc.shape, -jnp.inf, F32)
    a_sc[...] = jnp.zeros(a_sc.shape, F32)

    def update(kb, bias):
        vt = v_ref[0, pl.ds(pl.multiple_of(kb * T, T), T), :]
        vx = jnp.concatenate([vt, jnp.ones((T, DIFF_DV), BF16)], axis=1)
        s = _dot(q2, kt_ref[0, kb])
        s = s + (jnp.concatenate([bias, bias], axis=0) if getattr(bias, "ndim", 0) == 2 else bias)
        m_prev = m_sc[...]
        m_new = jnp.maximum(m_prev, jnp.max(s, axis=-1, keepdims=True))
        alpha = jnp.exp(m_prev - m_new)
        p = jnp.concatenate([jnp.exp(s[:, c:c + LANES] - m_new) for c in range(0, T, LANES)], axis=1)
        a_sc[...] = jnp.concatenate([alpha, alpha], axis=1) * a_sc[...] + _dot(p.astype(BF16), vx)
        m_sc[...] = m_new

    far_bias = tbl_ref[REL_BUCKETS - 1, hd]

    def far_body(kb, carry):
        update(kb, far_bias)
        return carry

    lax.fori_loop(0, jnp.maximum(qi - 1, 0), far_body, 0)

    @pl.when(qi >= 1)
    def _():
        update(qi - 1, bias_sc[:, 0:T])

    update(qi, bias_sc[:, T:2 * T])

    lam = _lambda(lq1, lk1, lq2, lk2, lam_init)
    o = a_sc[:T, :DIFF_DV] / a_sc[:T, DIFF_DV:] - lam * (a_sc[T:, :DIFF_DV] / a_sc[T:, DIFF_DV:])
    o_ref[0] = (_rms(o, sg_ref[...]) * (1.0 - lam_init)).astype(o_ref.dtype)


def diff_attention_prompt(q, kt, v, rel_table, lams, subln_g, lam_init, T):
    b, l, _ = q.shape
    assert T >= REL_MAX_DIST and l % T == 0 and DIFF_DV == 2 * DIFF_DH
    vec = lambda a: a.astype(F32).reshape(1, -1)
    return pl.pallas_call(
        functools.partial(_attn_prompt_kernel, T=T, lam_init=lam_init),
        grid=(DIFF_HEADS, b, l // T),
        in_specs=[
            pl.BlockSpec(memory_space=pltpu.SMEM),
            pl.BlockSpec((1, T, DIFF_DV), lambda h, bi, qi: (bi, qi, h)),
            pl.BlockSpec((1, l // T, DIFF_DV, T), lambda h, bi, qi: (bi, 0, h, 0)),
            pl.BlockSpec((1, l, DIFF_DV), lambda h, bi, qi: (bi, 0, h)),
        ] + [_resident((1, DIFF_DH))] * 4 + [_resident((1, DIFF_DV))],
        out_specs=pl.BlockSpec((1, T, DIFF_DV), lambda h, bi, qi: (bi, qi, h)),
        out_shape=jax.ShapeDtypeStruct((b, l, DIFF_HEADS * DIFF_DV), BF16),
        scratch_shapes=[pltpu.VMEM((T, 2 * T), F32), pltpu.VMEM((2 * T, LANES), F32), pltpu.VMEM((2 * T, 2 * DIFF_DV), F32)],
        compiler_params=_cparams(("arbitrary", "arbitrary", "arbitrary")),
        name="diff_attention_prompt",
    )(rel_table.astype(F32), q, kt, v, *[vec(a) for a in lams], vec(subln_g))


DEC_RPH = SUBLANES


def _attn_sample_kernel(pt_ref, q_ref, kn_ref, vn_ref, tb_ref, lq1, lk1, lq2, lk2, sg_ref, *rest,
                        pps, n_pages, lq, lam_init):
    kt_refs, v_refs = rest[:pps], rest[pps:2 * pps]
    o_ref, qm_sc, m_sc, l_sc, acc_sc = rest[2 * pps:]
    s = pl.program_id(1)
    n_steps = n_pages // pps
    past = n_pages * PAGE_SIZE
    rows = DIFF_HEADS * DEC_RPH
    width = DIFF_HEADS * DIFF_DV
    rid = lax.broadcasted_iota(jnp.int32, (rows, 1), 0)
    row_tok = (rid % DEC_RPH) // 2

    def q_rows():
        r8 = lax.broadcasted_iota(jnp.int32, (DEC_RPH, width), 0)
        lane_pair = lax.broadcasted_iota(jnp.int32, (DEC_RPH, width), 1) // DIFF_DH
        qrep = jnp.zeros((DEC_RPH, width), F32)
        for t in range(lq):
            qrep = jnp.where(r8 // 2 == t, q_ref[0, t:t + 1, :], qrep)
        qrep = qrep * (DIFF_DH ** -0.5)
        return jnp.concatenate([jnp.where(lane_pair == 2 * hh + r8 % 2, qrep, 0.0) for hh in range(DIFF_HEADS)], axis=0)

    @pl.when(s == 0)
    def _():
        qm_sc[...] = q_rows().astype(BF16)
        m_sc[...] = jnp.full(m_sc.shape, -jnp.inf, F32)
        l_sc[...] = jnp.zeros(l_sc.shape, F32)
        acc_sc[...] = jnp.zeros(acc_sc.shape, F32)

    far_bias = tb_ref[:, REL_BUCKETS - 1:REL_BUCKETS]

    def pages_update(last_near):
        qm = qm_sc[...]
        sc = []
        for i in range(pps):
            sci = _dot(qm, kt_refs[i][0].astype(BF16))
            if last_near and i == pps - 1:
                kpos = (n_pages - 1) * PAGE_SIZE + lax.broadcasted_iota(jnp.int32, (1, PAGE_SIZE), 1)
                sc.append(sci + _t5_bias((past + row_tok) - kpos, lambda bkt: tb_ref[:, bkt:bkt + 1]))
            else:
                sc.append(sci + far_bias)
        m_prev = m_sc[...]
        m_new = m_prev
        for sci in sc:
            m_new = jnp.maximum(m_new, jnp.max(sci, axis=-1, keepdims=True))
        alpha = jnp.exp(m_prev - m_new)
        p = [jnp.exp(sci - m_new) for sci in sc]
        l_new = alpha * l_sc[...]
        for pi in p:
            l_new = l_new + jnp.sum(pi, axis=-1, keepdims=True)
        l_sc[...] = l_new
        m_sc[...] = m_new
        for hh in range(DIFF_HEADS):
            sl = slice(hh * DEC_RPH, (hh + 1) * DEC_RPH)
            ph = jnp.concatenate([pi[sl] for pi in p], axis=1).astype(BF16)
            vh = jnp.concatenate([v_refs[i][0, pl.ds(hh, PAGE_SIZE, stride=DIFF_HEADS), :].astype(BF16)
                                  for i in range(pps)], axis=0)
            acc_sc[sl, :] = alpha[sl] * acc_sc[sl, :] + _dot(ph, vh)

    @pl.when(s < n_steps - 1)
    def _():
        pages_update(False)

    @pl.when(s == n_steps - 1)
    def _():
        pages_update(True)

    @pl.when(s == n_steps)
    def _():
        qf = q_rows()
        sj = []
        for jn in range(lq):
            dist = row_tok - jn
            sc = jnp.sum(qf * kn_ref[0, jn:jn + 1, :], axis=-1, keepdims=True)
            sc = sc + _t5_bias(dist, lambda bkt: tb_ref[:, bkt:bkt + 1])
            sj.append(jnp.where(dist >= 0, sc, -jnp.inf))
        m_prev = m_sc[...]
        m_new = m_prev
        for sc in sj:
            m_new = jnp.maximum(m_new, sc)
        alpha = jnp.exp(m_prev - m_new)
        l_new = alpha * l_sc[...]
        acc = alpha * acc_sc[...]
        for jn, sc in enumerate(sj):
            p = jnp.exp(sc - m_new)
            l_new = l_new + p
            vrow = jnp.concatenate(
                [jnp.broadcast_to(vn_ref[0, jn:jn + 1, hh * DIFF_DV:(hh + 1) * DIFF_DV], (DEC_RPH, DIFF_DV))
                 for hh in range(DIFF_HEADS)], axis=0)
            acc = acc + p * vrow

        lam = _lambda(lq1, lk1, lq2, lk2, lam_init)
        coef = jnp.where(rid % 2 == 0, 1.0, -lam) / l_new
        a = acc * coef
        a = a + pltpu.roll(a, rows - 1, 0)
        res = _rms(a, sg_ref[...]) * (1.0 - lam_init)
        for hh in range(DIFF_HEADS):
            for t in range(lq):
                r = hh * DEC_RPH + 2 * t
                o_ref[0, t:t + 1, hh * DIFF_DV:(hh + 1) * DIFF_DV] = res[r:r + 1, :]


def diff_attention_sample(q, k_new, v_new, cache_kt, cache_v, page_table, rel_table, lams, subln_g, lam_init, pps=16):
    b, lq, width = q.shape
    n_pages = page_table.shape[1]
    assert n_pages % pps == 0 and 2 * lq <= DEC_RPH and n_pages // pps >= 1
    n_steps = n_pages // pps
    rows = DIFF_HEADS * DEC_RPH
    head_of_row = np.arange(rows) // DEC_RPH
    tb = jnp.pad(rel_table.astype(F32).T[head_of_row], ((0, 0), (0, LANES - REL_BUCKETS)))
    vec = lambda a: a.astype(F32).reshape(1, -1)

    def page_map(i):
        return lambda bi, s, pt: (pt[bi * n_pages + jnp.minimum(s, n_steps - 1) * pps + i], 0, 0)

    tok = pl.BlockSpec((1, lq, width), lambda bi, s, pt: (bi, 0, 0))
    const = lambda shape: pl.BlockSpec(shape, lambda bi, s, pt: (0,) * len(shape))
    kpage = [pl.BlockSpec((1, width, PAGE_SIZE), page_map(i)) for i in range(pps)]
    vpage = [pl.BlockSpec((1, PAGE_SIZE * DIFF_HEADS, DIFF_DV), page_map(i)) for i in range(pps)]
    grid_spec = pltpu.PrefetchScalarGridSpec(
        num_scalar_prefetch=1,
        grid=(b, n_steps + 1),
        in_specs=[tok, tok, tok, const((rows, LANES))] + [const((1, DIFF_DH))] * 4 + [const((1, DIFF_DV))] + kpage + vpage,
        out_specs=tok,
        scratch_shapes=[pltpu.VMEM((rows, width), BF16), pltpu.VMEM((rows, 1), F32), pltpu.VMEM((rows, 1), F32),
                        pltpu.VMEM((rows, DIFF_DV), F32)],
    )
    return pl.pallas_call(
        functools.partial(_attn_sample_kernel, pps=pps, n_pages=n_pages, lq=lq, lam_init=lam_init),
        grid_spec=grid_spec,
        out_shape=jax.ShapeDtypeStruct((b, lq, width), F32),
        compiler_params=_cparams(("parallel", "arbitrary")),
        name="diff_attention_sample",
    )(page_table.reshape(-1).astype(jnp.int32), q, k_new, v_new, tb, *[vec(a) for a in lams], vec(subln_g),
      *([cache_kt] * pps), *([cache_v] * pps))


def kernel(x_prompt, x_sample, state_ret, state_ssm, state_conv, cache_k_diff, cache_v_diff, page_table, norm_mix_g, norm_ffn_g, norm_final_g, ret_w_in, ret_w_out, ssd_w_in, ssd_conv_w, ssd_conv_b, ssd_dt_bias, ssd_A_log, ssd_D, ssd_norm_g, ssd_w_out, diff_w_in, diff_lam_q1, diff_lam_k1, diff_lam_q2, diff_lam_k2, diff_subln_g, diff_w_out, rel_bias_table, ffn_w_in, ffn_w_out):
    kinds = tuple(i % 3 for i in range(DEPTH))
    n_pages = page_table.shape[1]
    bf = lambda a: a.astype(BF16)
    ret_wi, ret_wo = bf(ret_w_in), bf(ret_w_out)
    ssd_wmain = bf(ssd_w_in[:, :, :SSD_INNER + SSD_CONV_CH])
    ssd_wxbc = ssd_wmain[:, :, SSD_INNER:]
    ssd_wdt = bf(jnp.pad(ssd_w_in[:, :, SSD_INNER + SSD_CONV_CH:], ((0, 0), (0, 0), (0, LANES - SSD_HEADS))))
    ssd_wo = bf(ssd_w_out)
    diff_wi, diff_wo = bf(diff_w_in), bf(diff_w_out)
    ffn_wi, ffn_wo = bf(ffn_w_in), bf(ffn_w_out)

    n_ret = kinds.count(0)

    def run_group(x, sample):
        b, l, d = x.shape
        t = b * l
        tm = 512 if t % 512 == 0 else t
        ret_c = RET_CHUNK if l % RET_CHUNK == 0 else l
        ssd_c = SSD_CHUNK if l % SSD_CHUNK == 0 else l
        act = F32 if sample else BF16
        offset = n_pages * PAGE_SIZE if sample else 0
        rope_rows = l if l % RET_CHUNK == 0 else -(-l // BF16_ROWS) * BF16_ROWS
        cos, sin = rope_tables(offset, rope_rows)
        ret_new, ssm_new, conv_new, k_new, v_new = None, [], [], [], []
        h = x.reshape(t, d)
        for i in range(DEPTH):
            kind = kinds[i]
            j = kinds[:i].count(kind)
            g = norm_mix_g[i]
            if kind == 0:
                n_in = ret_wi.shape[2]
                (qkvg,) = norm_linear(h, g, ret_wi, j, [(0, n_in, act)], tm)
                s0, j0 = (state_ret, j) if sample else (jnp.zeros((1, b, RET_HEADS, RET_DK, RET_DV), F32), 0)
                y, ret_new = retention_core(qkvg.reshape(b, l, n_in), cos, sin, s0, j0, ret_new, j, n_ret, ret_c, act)
                y, wmix = y.reshape(t, RET_VDIM), ret_wo
            elif kind == 1:
                n_in = SSD_INNER + SSD_CONV_CH
                zxbc, dt_raw = norm_linear(h, g, ssd_wmain, j, [(0, n_in, act)], tm, side=[(ssd_wdt, F32)])
                zxbc = zxbc.reshape(b, l, n_in)
                if sample:
                    buf, s0 = state_conv[j], state_ssm[j]
                    conv_new.append(zxbc[:, l - (SSD_CONV - 1):, SSD_INNER:])
                else:
                    buf = jnp.zeros((b, SSD_CONV - 1, SSD_CONV_CH), F32)
                    s0 = jnp.zeros((b, SSD_HEADS, SSD_HEADDIM, SSD_STATE), F32)
                    tail = h.reshape(b, l, d)[:, l - (SSD_CONV - 1):].reshape(b * (SSD_CONV - 1), d)
                    rows = -(-tail.shape[0] // BF16_ROWS) * BF16_ROWS
                    tail = jnp.pad(tail, ((0, rows - tail.shape[0]), (0, 0)))
                    (xbc_tail,) = norm_linear(tail, g, ssd_wxbc, j, [(0, SSD_CONV_CH, F32)], rows)
                    conv_new.append(xbc_tail[:b * (SSD_CONV - 1)].reshape(b, SSD_CONV - 1, SSD_CONV_CH))
                y, s1 = ssd_core(zxbc, dt_raw.reshape(b, l, LANES), buf, s0, ssd_conv_w[j], ssd_conv_b[j],
                                 ssd_dt_bias[j], ssd_A_log[j], ssd_D[j], ssd_norm_g[j], ssd_c, act)
                ssm_new.append(s1)
                y, wmix = y.reshape(t, SSD_INNER), ssd_wo
            else:
                lam_init = 0.8 - 0.6 * math.exp(-0.3 * i)
                lams = (diff_lam_q1[j], diff_lam_k1[j], diff_lam_q2[j], diff_lam_k2[j])
                if sample:
                    q, kn, vn = norm_linear(h, g, diff_wi, j, [(0, d, F32), (d, 2 * d, F32), (2 * d, 3 * d, F32)], tm)
                    cache_kt = jnp.transpose(cache_k_diff[j], (0, 2, 3, 4, 1)).reshape(-1, d, PAGE_SIZE)
                    cache_v = cache_v_diff[j].reshape(-1, PAGE_SIZE * DIFF_HEADS, DIFF_DV)
                    y = diff_attention_sample(q.reshape(b, l, d), kn.reshape(b, l, d), vn.reshape(b, l, d), cache_kt,
                                              cache_v, page_table, rel_bias_table, lams, diff_subln_g[j], lam_init)
                    k_new.append(kn.reshape(b, l, DIFF_HEADS, 2, DIFF_DH))
                    v_new.append(vn.reshape(b, l, DIFF_HEADS, DIFF_DV))
                else:
                    w = diff_wi[j]
                    q, kt32, kt16, v32, v16 = diff_in_proj(h.reshape(b, l, d), g, w[:, :d], w[:, d:2 * d].T, w[:, 2 * d:],
                                                           tm, ATTN_TILE)
                    y = diff_attention_prompt(q, kt16, v16, rel_bias_table, lams, diff_subln_g[j], lam_init, ATTN_TILE)
                    k_new.append(jnp.transpose(kt32.reshape(b, DIFF_HEADS, 2, DIFF_DH, l), (0, 4, 1, 2, 3)))
                    v_new.append(v32.reshape(b, l, DIFF_HEADS, DIFF_DV))
                y, wmix = y.reshape(t, d), diff_wo
            h = out_ffn(y, wmix, j, h, norm_ffn_g[i], ffn_wi, ffn_wo, i, tm,
                        g_final=norm_final_g if i == DEPTH - 1 else None)
        return h.reshape(b, l, d), ret_new, jnp.stack(ssm_new), jnp.stack(conv_new), jnp.stack(k_new), jnp.stack(v_new)

    y_p, ret_p, ssm_p, conv_p, k_p, v_p = run_group(x_prompt, False)
    y_s, ret_s, ssm_s, conv_s, k_s, v_s = run_group(x_sample, True)
    return (y_p, y_s, ret_p, ssm_p, conv_p, k_p, v_p, ret_s, ssm_s, conv_s, k_s, v_s)
```

```python
import functools
import math

import jax
import jax.numpy as jnp
import numpy as np
from jax import lax
from jax.experimental import pallas as pl
from jax.experimental.pallas import tpu as pltpu

F32 = jnp.float32
BF16 = jnp.bfloat16

D_MODEL = 1024
DEPTH = 4
PAGE_SIZE = 128
EPS = 1e-6
RET_CHUNK = 256
SSD_CHUNK = 128
ATTN_TILE = 512

RET_HEADS = 4
RET_DK = D_MODEL // RET_HEADS
RET_DV = 2 * RET_DK
RET_VDIM = RET_HEADS * RET_DV
ROPE_BASE = 10000.0
ROPE_HALF = RET_DK // 2

SSD_INNER = 2 * D_MODEL
SSD_HEADDIM = 64
SSD_HEADS = SSD_INNER // SSD_HEADDIM
SSD_GROUPS = 8
SSD_HPG = SSD_HEADS // SSD_GROUPS
SSD_STATE = 128
SSD_CONV = 4
SSD_BC = SSD_GROUPS * SSD_STATE
SSD_CONV_CH = SSD_INNER + 2 * SSD_BC
SSD_GW = SSD_HPG * SSD_HEADDIM

DIFF_HEADS = 8
DIFF_DH = D_MODEL // DIFF_HEADS // 2
DIFF_DV = 2 * DIFF_DH
REL_BUCKETS = 32
REL_MAX_DIST = 128

FFN_HIDDEN = -(-8 * D_MODEL // (3 * 256)) * 256

V7X_VMEM_BYTES = 64 * 1024 * 1024
LANES = 128
SUBLANES = 8
BF16_ROWS = 16
VMEM_LIMIT = V7X_VMEM_BYTES * 7 // 8


def _cparams(sem):
    return pltpu.CompilerParams(dimension_semantics=sem, vmem_limit_bytes=VMEM_LIMIT)


def _dot(a, b):
    return jnp.dot(a, b, preferred_element_type=F32)


def _dot_nt(a, b):
    return lax.dot_general(a, b, (((1,), (1,)), ((), ())), preferred_element_type=F32)


def _dot_tn(a, b):
    return lax.dot_general(a, b, (((0,), (0,)), ((), ())), preferred_element_type=F32)


def _rms(x, g):
    return x * lax.rsqrt(jnp.mean(x * x, axis=-1, keepdims=True) + EPS) * g


def _silu(x):
    half = 0.5 * x
    return half * jnp.tanh(half) + half


def _resident(shape):
    return pl.BlockSpec(shape, lambda *_: (0,) * len(shape), pipeline_mode=pl.Buffered(1))


def _resident_layer(stacked, layer):
    tail = stacked.shape[1:]
    return pl.BlockSpec((None,) + tail, lambda *_: (layer,) + (0,) * len(tail), pipeline_mode=pl.Buffered(1))


def _norm_linear_kernel(x_ref, g_ref, w_ref, *rest, outs, tn, side):
    side_w = rest[:side]
    o_refs = rest[side:side + len(outs)]
    side_o = rest[side + len(outs):]
    xn = _rms(x_ref[...], g_ref[...]).astype(BF16)
    n = w_ref.shape[1]
    for c0 in range(0, n, tn):
        acc = _dot(xn, w_ref[:, c0:c0 + tn])
        for o_ref, (lo, hi, _) in zip(o_refs, outs):
            if lo <= c0 and c0 + tn <= hi:
                o_ref[:, c0 - lo:c0 - lo + tn] = acc.astype(o_ref.dtype)
    for w2_ref, o2_ref in zip(side_w, side_o):
        o2_ref[...] = _dot(xn, w2_ref[...]).astype(o2_ref.dtype)


def norm_linear(x, g, w_stack, layer, outs, tm, tn=512, side=()):
    t, d = x.shape
    n = w_stack.shape[2]
    tn = min(tn, n)
    assert t % tm == 0 and n % tn == 0 and all(lo % tn == 0 and hi % tn == 0 for lo, hi, _ in outs)
    row = lambda width: pl.BlockSpec((tm, width), lambda i: (i, 0))
    return pl.pallas_call(
        functools.partial(_norm_linear_kernel, outs=tuple(outs), tn=tn, side=len(side)),
        grid=(t // tm,),
        in_specs=[row(d), _resident((1, d)), _resident_layer(w_stack, layer)] + [_resident_layer(w2, layer) for w2, _ in side],
        out_specs=[row(hi - lo) for lo, hi, _ in outs] + [row(w2.shape[2]) for w2, _ in side],
        out_shape=[jax.ShapeDtypeStruct((t, hi - lo), dt) for lo, hi, dt in outs]
        + [jax.ShapeDtypeStruct((t, w2.shape[2]), dt) for w2, dt in side],
        compiler_params=_cparams(("parallel",)),
        name="norm_linear",
    )(x, g.reshape(1, d), w_stack, *[w2 for w2, _ in side])


def _out_ffn_kernel(y_ref, wmix_ref, h_ref, g_ref, wgu_ref, wo_ref, *rest, th, final):
    o_ref = rest[-1]
    h = h_ref[...] + _dot(y_ref[...].astype(BF16), wmix_ref[...])
    xn = _rms(h, g_ref[...]).astype(BF16)
    hidden = wo_ref.shape[0]
    o_ref[...] = h
    for c0 in range(0, hidden, th):
        gate = _dot(xn, wgu_ref[:, c0:c0 + th])
        up = _dot(xn, wgu_ref[:, hidden + c0:hidden + c0 + th])
        act = (_silu(gate) * up).astype(BF16)
        o_ref[...] += _dot(act, wo_ref[c0:c0 + th, :])
    if final:
        o_ref[...] = _rms(o_ref[...], rest[0][...])


def out_ffn(y, wmix_stack, j, h, g, wgu_stack, wo_stack, i, tm, g_final=None, th=256):
    t, k = y.shape
    d = h.shape[1]
    hidden = wo_stack.shape[1]
    assert hidden % th == 0 and t % tm == 0
    final = g_final is not None
    extra = [g_final.reshape(1, d)] if final else []
    return pl.pallas_call(
        functools.partial(_out_ffn_kernel, th=th, final=final),
        grid=(t // tm,),
        in_specs=[pl.BlockSpec((tm, k), lambda m: (m, 0)), _resident_layer(wmix_stack, j),
                  pl.BlockSpec((tm, d), lambda m: (m, 0)), _resident((1, d)), _resident_layer(wgu_stack, i),
                  _resident_layer(wo_stack, i)] + [_resident((1, d))] * len(extra),
        out_specs=pl.BlockSpec((tm, d), lambda m: (m, 0)),
        out_shape=jax.ShapeDtypeStruct((t, d), F32),
        compiler_params=_cparams(("parallel",)),
        name="out_ffn",
    )(y, wmix_stack, h, g.reshape(1, d), wgu_stack, wo_stack, *extra)


def _rope_kernel(inv_ref, cos_ref, sin_ref, *, offset):
    rows = cos_ref.shape[0]
    pos = (offset + pl.program_id(0) * rows + lax.broadcasted_iota(jnp.int32, (rows, 1), 0)).astype(F32)
    ang = pos * inv_ref[...]
    cos_ref[...] = jnp.cos(ang)
    sin_ref[...] = jnp.sin(ang)


def rope_tables(offset, rows):
    inv = (1.0 / (ROPE_BASE ** jnp.linspace(0.0, 1.0, ROPE_HALF, dtype=F32))).reshape(1, ROPE_HALF)
    tr = min(rows, 512)
    return pl.pallas_call(
        functools.partial(_rope_kernel, offset=offset),
        grid=(rows // tr,),
        in_specs=[_resident((1, ROPE_HALF))],
        out_specs=[pl.BlockSpec((tr, ROPE_HALF), lambda i: (i, 0))] * 2,
        out_shape=[jax.ShapeDtypeStruct((rows, ROPE_HALF), F32)] * 2,
        compiler_params=_cparams(("parallel",)),
        name="rope_tables",
    )(inv)


def _ret_kernel(q_ref, k_ref, v_ref, g_ref, cos_ref, sin_ref, s0_ref, *rest, C, c_real, carry, slot, n_chunks):
    y_ref, st_ref = rest[carry:carry + 2]
    decay_sc = rest[carry + 2]
    stage = rest[carry + 3:]
    c = pl.program_id(1)
    log_gamma = [math.log(1.0 - 2.0 ** (-5.0 - hd)) for hd in range(RET_HEADS)]
    i = lax.broadcasted_iota(jnp.int32, (C, 1), 0).astype(F32)
    j = lax.broadcasted_iota(jnp.int32, (1, C), 1).astype(F32)

    @pl.when(c == 0)
    def _():
        st_ref[slot] = s0_ref[...]
        rel = i - j
        for hd in range(RET_HEADS):
            decay_sc[hd] = jnp.where(rel >= 0, jnp.exp(jnp.maximum(rel, 0.0) * log_gamma[hd]), 0.0)

    if c_real == C:
        q_src, k_src, v_src = q_ref.at[0], k_ref.at[0], v_ref.at[0]
    else:
        for ref, st in zip((q_ref, k_ref, v_ref), stage):
            st[...] = jnp.zeros(st.shape, F32)
            st[0:c_real, :] = ref[0].astype(F32)
        q_src, k_src, v_src = stage
    cos = cos_ref[...]
    sin = sin_ref[...]

    def rot(x):
        x1, x2 = x[:, :ROPE_HALF], x[:, ROPE_HALF:]
        return jnp.concatenate([x1 * cos - x2 * sin, x2 * cos + x1 * sin], axis=-1)

    for hd in range(RET_HEADS):
        lg = log_gamma[hd]
        ks = slice(hd * RET_DK, (hd + 1) * RET_DK)
        vs = slice(hd * RET_DV, (hd + 1) * RET_DV)
        qr = rot(q_src[:, ks].astype(F32)).astype(BF16)
        kr = rot(k_src[:, ks].astype(F32)) * (RET_DK ** -0.5)
        v = v_src[:, vs].astype(BF16)
        scores = _dot_nt(qr, kr.astype(BF16)) * decay_sc[hd]
        s_prev = st_ref[slot, 0, hd]
        o = _dot(scores.astype(BF16), v) + _dot(qr, s_prev.astype(BF16)) * jnp.exp((i + 1.0) * lg)
        kw = (kr * jnp.exp((c_real - 1.0 - i) * lg)).astype(BF16)
        st_ref[slot, 0, hd] = math.exp(c_real * lg) * s_prev + _dot_tn(kw, v)
        o = o * lax.rsqrt(jnp.mean(o * o, axis=-1, keepdims=True) + EPS)
        y_ref[0, :, vs] = (_silu(g_ref[0, :, vs].astype(F32)) * o[0:c_real]).astype(y_ref.dtype)

    if st_ref.shape[0] > 1:
        @pl.when(c == n_chunks - 1)
        def _():
            for other in range(st_ref.shape[0]):
                if other != slot:
                    st_ref[other] = st_ref[slot]


def retention_core(qkvg, cos, sin, s0_stack, j0, s1_stack, j1, n_layers, c_real, out_dtype):
    b, l, _ = qkvg.shape
    C = cos.shape[0] if c_real < RET_CHUNK else RET_CHUNK
    nc = l // c_real
    qk_w, v_w = RET_HEADS * RET_DK, RET_VDIM
    k_blk, v_blk, g_blk = 1, 2 * qk_w // v_w, 2 * qk_w // v_w + 1
    stage = [] if c_real == C else [pltpu.VMEM((C, qk_w), F32), pltpu.VMEM((C, qk_w), F32), pltpu.VMEM((C, v_w), F32)]
    carry = s1_stack is not None
    st_block, st_first, slot = (1, j1, 0) if carry else (n_layers, 0, j1)
    return pl.pallas_call(
        functools.partial(_ret_kernel, C=C, c_real=c_real, carry=int(carry), slot=slot, n_chunks=nc),
        grid=(b, nc),
        in_specs=[
            pl.BlockSpec((1, c_real, qk_w), lambda bi, c: (bi, c, 0)),
            pl.BlockSpec((1, c_real, qk_w), lambda bi, c: (bi, c, k_blk)),
            pl.BlockSpec((1, c_real, v_w), lambda bi, c: (bi, c, v_blk)),
            pl.BlockSpec((1, c_real, v_w), lambda bi, c: (bi, c, g_blk)),
            pl.BlockSpec((C, ROPE_HALF), lambda bi, c: (c, 0)),
            pl.BlockSpec((C, ROPE_HALF), lambda bi, c: (c, 0)),
            pl.BlockSpec((None, 1, RET_HEADS, RET_DK, RET_DV), lambda bi, c: (j0, bi, 0, 0, 0)),
        ] + ([pl.BlockSpec(memory_space=pl.ANY)] if carry else []),
        out_specs=[
            pl.BlockSpec((1, c_real, v_w), lambda bi, c: (bi, c, 0)),
            pl.BlockSpec((st_block, 1, RET_HEADS, RET_DK, RET_DV), lambda bi, c: (st_first, bi, 0, 0, 0)),
        ],
        out_shape=[jax.ShapeDtypeStruct((b, l, RET_VDIM), out_dtype),
                   jax.ShapeDtypeStruct((n_layers, b, RET_HEADS, RET_DK, RET_DV), F32)],
        scratch_shapes=[pltpu.VMEM((RET_HEADS, C, C), F32)] + stage,
        input_output_aliases={7: 1} if carry else {},
        compiler_params=_cparams(("parallel", "arbitrary")),
        name="retention_core",
    )(qkvg, qkvg, qkvg, qkvg, cos, sin, s0_stack, *([s1_stack] if carry else []))


def _softplus(x):
    return jnp.maximum(x, 0.0) + jnp.log1p(jnp.exp(-jnp.abs(x)))


def _cumsum(x, axis):
    n = x.shape[axis]
    idx = lax.broadcasted_iota(jnp.int32, x.shape, axis)
    s = 1
    while s < n:
        x = x + jnp.where(idx >= s, pltpu.roll(x, s, axis), 0.0)
        s *= 2
    return x


def _ssd_kernel(z_ref, x_ref, bc_ref, dt_ref, dtt_ref, cw_ref, cb_ref, dtb_ref, dtbt_ref, al_ref, alt_ref,
                dsk_ref, ng_ref, buf_ref, s0_ref, y_ref, s1_ref, extx, extbc, xs_ref, bcs_ref, *, C, c_real):
    c = pl.program_id(1)
    halo = SUBLANES

    @pl.when(c == 0)
    def _():
        s1_ref[...] = s0_ref[...]
        extx[...] = jnp.zeros(extx.shape, F32)
        extbc[...] = jnp.zeros(extbc.shape, F32)
        extx[0:halo, :] = buf_ref[0, :, 0:SSD_INNER]
        extbc[0:halo, :] = buf_ref[0, :, SSD_INNER:SSD_CONV_CH]

    extx[halo:halo + c_real, :] = x_ref[0].astype(F32)
    extbc[halo:halo + c_real, :] = bc_ref[0].astype(F32)

    cw = 512
    first = halo - (SSD_CONV - 1)
    for ext, dst, ch0 in ((extx, xs_ref, 0), (extbc, bcs_ref, SSD_INNER)):
        for col in range(0, SSD_INNER, cw):
            acc = cb_ref[:, ch0 + col:ch0 + col + cw]
            for tap in range(SSD_CONV):
                acc = acc + ext[first + tap:first + tap + C, col:col + cw] * cw_ref[tap:tap + 1, ch0 + col:ch0 + col + cw]
            dst[:, col:col + cw] = _silu(acc).astype(dst.dtype)
    extx[0:halo, :] = extx[C:C + halo, :]
    extbc[0:halo, :] = extbc[C:C + halo, :]

    row = lax.broadcasted_iota(jnp.int32, (C, 1), 0)
    col_i = lax.broadcasted_iota(jnp.int32, (1, C), 1)
    dt = jnp.where(row < c_real, _softplus(dt_ref[0, 0] + dtb_ref[...]), 0.0)
    dtt = jnp.where(col_i < c_real, _softplus(dtt_ref[0, 0] + dtbt_ref[...]), 0.0)
    cs = _cumsum(dt * (-jnp.exp(al_ref[...])), 0)
    cst = _cumsum(dtt * (-jnp.exp(alt_ref[...])), 1)
    causal = row >= col_i
    lane_r = lax.broadcasted_iota(jnp.int32, (1, SSD_GW), 1) // SSD_HEADDIM
    sub_r = lax.broadcasted_iota(jnp.int32, (SSD_GW, 1), 0) // SSD_HEADDIM

    def by_head(vals, sel):
        out = vals[SSD_HPG - 1]
        for r in range(SSD_HPG - 2, -1, -1):
            out = jnp.where(sel == r, vals[r], out)
        return out

    for g in range(SSD_GROUPS):
        bm = bcs_ref[:, g * SSD_STATE:(g + 1) * SSD_STATE]
        cm = bcs_ref[:, SSD_BC + g * SSD_STATE:SSD_BC + (g + 1) * SSD_STATE]
        cb = _dot_nt(cm, bm)
        h0 = s1_ref[0, g * SSD_HPG:(g + 1) * SSD_HPG].reshape(SSD_GW, SSD_STATE)
        xg = xs_ref[:, g * SSD_GW:(g + 1) * SSD_GW]
        heads = range(g * SSD_HPG, (g + 1) * SSD_HPG)
        cs_cols = [cs[:, hh:hh + 1] for hh in heads]
        cs_last = [cs[C - 1:C, hh:hh + 1] for hh in heads]
        yg = _dot_nt(cm, h0.astype(BF16)) * by_head([jnp.exp(cc) for cc in cs_cols], lane_r)
        for r, hh in enumerate(heads):
            seg = cs_cols[r] - cst[hh:hh + 1, :]
            w = cb * jnp.exp(jnp.where(causal, seg, -jnp.inf)) * dtt[hh:hh + 1, :]
            yg = yg + _dot(w.astype(BF16), jnp.where(lane_r == r, xg, 0.0).astype(BF16))
        w_end = by_head([jnp.exp(cl - cc) * dt[:, hh:hh + 1] for cl, cc, hh in zip(cs_last, cs_cols, heads)], lane_r)
        h1 = by_head([jnp.exp(cl) for cl in cs_last], sub_r) * h0 + _dot_tn((xg * w_end).astype(BF16), bm)
        s1_ref[0, g * SSD_HPG:(g + 1) * SSD_HPG] = h1.reshape(SSD_HPG, SSD_HEADDIM, SSD_STATE)
        sl = slice(g * SSD_GW, (g + 1) * SSD_GW)
        yo = (yg[0:c_real] + dsk_ref[:, sl] * xg[0:c_real]) * _silu(z_ref[0, :, sl].astype(F32))
        yo = yo * lax.rsqrt(jnp.mean(yo * yo, axis=-1, keepdims=True) + EPS) * ng_ref[:, sl]
        y_ref[0, :, sl] = yo.astype(y_ref.dtype)


def ssd_core(zxbc, dt_raw, conv_buf, s0, conv_w, conv_b, dt_bias, a_log, d_skip, norm_g, c_real, out_dtype):
    b, l, _ = zxbc.shape
    C = SSD_CHUNK
    nc = l // c_real
    halo = SUBLANES
    dt4 = dt_raw.reshape(b, nc, c_real, LANES)
    dtt = jnp.swapaxes(dt4[..., :SSD_HEADS], -1, -2)
    if c_real < C:
        dt4 = jnp.pad(dt4, ((0, 0), (0, 0), (0, C - c_real), (0, 0)))
        dtt = jnp.pad(dtt, ((0, 0), (0, 0), (0, 0), (0, C - c_real)))
    lane_pad = LANES - SSD_HEADS
    buf8 = jnp.pad(conv_buf.astype(F32), ((0, 0), (halo - (SSD_CONV - 1), 0), (0, 0)))
    small = [
        conv_w.astype(F32), conv_b.reshape(1, -1).astype(F32),
        jnp.pad(dt_bias.astype(F32), (0, lane_pad)).reshape(1, LANES), dt_bias.astype(F32).reshape(SSD_HEADS, 1),
        jnp.pad(a_log.astype(F32), (0, lane_pad)).reshape(1, LANES), a_log.astype(F32).reshape(SSD_HEADS, 1),
        jnp.repeat(d_skip.astype(F32), SSD_HEADDIM).reshape(1, SSD_INNER), norm_g.astype(F32).reshape(1, SSD_INNER),
    ]
    w = SSD_INNER
    return pl.pallas_call(
        functools.partial(_ssd_kernel, C=C, c_real=c_real),
        grid=(b, nc),
        in_specs=[
            pl.BlockSpec((1, c_real, w), lambda bi, c: (bi, c, 0)),
            pl.BlockSpec((1, c_real, w), lambda bi, c: (bi, c, 1)),
            pl.BlockSpec((1, c_real, w), lambda bi, c: (bi, c, 2)),
            pl.BlockSpec((1, 1, C, LANES), lambda bi, c: (bi, c, 0, 0)),
            pl.BlockSpec((1, 1, SSD_HEADS, C), lambda bi, c: (bi, c, 0, 0)),
        ] + [_resident(a.shape) for a in small] + [
            pl.BlockSpec((1, halo, SSD_CONV_CH), lambda bi, c: (bi, 0, 0)),
            pl.BlockSpec((1, SSD_HEADS, SSD_HEADDIM, SSD_STATE), lambda bi, c: (bi, 0, 0, 0)),
        ],
        out_specs=[
            pl.BlockSpec((1, c_real, w), lambda bi, c: (bi, c, 0)),
            pl.BlockSpec((1, SSD_HEADS, SSD_HEADDIM, SSD_STATE), lambda bi, c: (bi, 0, 0, 0)),
        ],
        out_shape=[jax.ShapeDtypeStruct((b, l, SSD_INNER), out_dtype),
                   jax.ShapeDtypeStruct((b, SSD_HEADS, SSD_HEADDIM, SSD_STATE), F32)],
        scratch_shapes=[pltpu.VMEM((C + halo, w), F32), pltpu.VMEM((C + halo, w), F32),
                        pltpu.VMEM((C, w), F32), pltpu.VMEM((C, w), BF16)],
        compiler_params=_cparams(("parallel", "arbitrary")),
        name="ssd_core",
    )(zxbc, zxbc, zxbc, dt4, dtt, *small, buf8, s0)


def _t5_bias(dist, table_at):
    n = jnp.maximum(dist, 0)
    exact = REL_BUCKETS // 2
    nf = jnp.maximum(n, 1).astype(F32)
    large = exact + (jnp.log(nf / exact) / math.log(REL_MAX_DIST / exact) * (REL_BUCKETS - exact)).astype(jnp.int32)
    bucket = jnp.where(n < exact, n, jnp.minimum(large, REL_BUCKETS - 1))
    bias = jnp.zeros(dist.shape, F32)
    for bkt in range(REL_BUCKETS):
        bias = jnp.where(bucket == bkt, table_at(bkt), bias)
    return bias


def _lambda(lq1, lk1, lq2, lk2, lam_init):
    s1 = jnp.sum(lq1[...] * lk1[...], axis=-1, keepdims=True)
    s2 = jnp.sum(lq2[...] * lk2[...], axis=-1, keepdims=True)
    return jnp.exp(s1) - jnp.exp(s2) + lam_init


def _diff_in_proj_kernel(x_ref, g_ref, wq_ref, wkt_ref, wv_ref, q_ref, kt32_ref, kt16_ref, v32_ref, v16_ref, *, tn, kb):
    xn = _rms(x_ref[0], g_ref[...]).astype(BF16)
    d = wq_ref.shape[1]
    tm = xn.shape[0]
    for c0 in range(0, d, tn):
        q_ref[0, :, c0:c0 + tn] = (_dot(xn, wq_ref[:, c0:c0 + tn]) * (DIFF_DH ** -0.5)).astype(BF16)
        v = _dot(xn, wv_ref[:, c0:c0 + tn])
        v32_ref[0, :, c0:c0 + tn] = v
        v16_ref[0, :, c0:c0 + tn] = v.astype(BF16)
        kt = _dot_nt(wkt_ref[c0:c0 + tn, :], xn)
        kt32_ref[0, c0:c0 + tn, :] = kt
        for s in range(tm // kb):
            kt16_ref[0, s, c0:c0 + tn, :] = kt[:, s * kb:(s + 1) * kb].astype(BF16)


def diff_in_proj(x, g, wq, wkt, wv, tm, kb):
    b, l, d = x.shape
    assert l % tm == 0 and tm % kb == 0
    tok = lambda: pl.BlockSpec((1, tm, d), lambda bi, i: (bi, i, 0))
    return pl.pallas_call(
        functools.partial(_diff_in_proj_kernel, tn=512, kb=kb),
        grid=(b, l // tm),
        in_specs=[tok(), _resident((1, d)), _resident((d, d)), _resident((d, d)), _resident((d, d))],
        out_specs=[tok(), pl.BlockSpec((1, d, tm), lambda bi, i: (bi, 0, i)),
                   pl.BlockSpec((1, tm // kb, d, kb), lambda bi, i: (bi, i, 0, 0)), tok(), tok()],
        out_shape=[jax.ShapeDtypeStruct((b, l, d), BF16), jax.ShapeDtypeStruct((b, d, l), F32),
                   jax.ShapeDtypeStruct((b, l // kb, d, kb), BF16), jax.ShapeDtypeStruct((b, l, d), F32),
                   jax.ShapeDtypeStruct((b, l, d), BF16)],
        compiler_params=_cparams(("parallel", "parallel")),
        name="diff_in_proj",
    )(x, g.reshape(1, d), wq, wkt, wv)


def _attn_prompt_kernel(tbl_ref, q_ref, kt_ref, v_ref, lq1, lk1, lq2, lk2, sg_ref, o_ref,
                        bias_sc, m_sc, a_sc, *, T, lam_init):
    hd = pl.program_id(0)
    bi = pl.program_id(1)
    qi = pl.program_id(2)

    @pl.when((bi == 0) & (qi == 0))
    def _():
        i = lax.broadcasted_iota(jnp.int32, (T, 2 * T), 0)
        j = lax.broadcasted_iota(jnp.int32, (T, 2 * T), 1)
        dist = i - j + T
        bias_sc[...] = jnp.where(dist >= 0, _t5_bias(dist, lambda bkt: tbl_ref[bkt, hd]), -jnp.inf)

    lane = lax.broadcasted_iota(jnp.int32, (1, DIFF_DV), 1)
    q = q_ref[0]
    zero = jnp.zeros((), q.dtype)
    q2 = jnp.concatenate([jnp.where(lane < DIFF_DH, q, zero), jnp.where(lane >= DIFF_DH, q, zero)], axis=0)
    m_sc[...] = jnp.full(m_sc.shape, -jnp.inf, F32)
    a_sc[...] = jnp.zeros(a_sc.shape, F32)

    def update(kb, bias):
        vt = v_ref[0, pl.ds(pl.multiple_of(kb * T, T), T), :]
        vx = jnp.concatenate([vt, jnp.ones((T, DIFF_DV), BF16)], axis=1)
        s = _dot(q2, kt_ref[0, kb])
        s = s + (jnp.concatenate([bias, bias], axis=0) if getattr(bias, "ndim", 0) == 2 else bias)
        m_prev = m_sc[...]
        m_new = jnp.maximum(m_prev, jnp.max(s, axis=-1, keepdims=True))
        alpha = jnp.exp(m_prev - m_new)
        p = jnp.concatenate([jnp.exp(s[:, c:c + LANES] - m_new) for c in range(0, T, LANES)], axis=1)
        a_sc[...] = jnp.concatenate([alpha, alpha], axis=1) * a_sc[...] + _dot(p.astype(BF16), vx)
        m_sc[...] = m_new

    far_bias = tbl_ref[REL_BUCKETS - 1, hd]

    def far_body(kb, carry):
        update(kb, far_bias)
        return carry

    lax.fori_loop(0, jnp.maximum(qi - 1, 0), far_body, 0)

    @pl.when(qi >= 1)
    def _():
        update(qi - 1, bias_sc[:, 0:T])

    update(qi, bias_sc[:, T:2 * T])

    lam = _lambda(lq1, lk1, lq2, lk2, lam_init)
    o = a_sc[:T, :DIFF_DV] / a_sc[:T, DIFF_DV:] - lam * (a_sc[T:, :DIFF_DV] / a_sc[T:, DIFF_DV:])
    o_ref[0] = (_rms(o, sg_ref[...]) * (1.0 - lam_init)).astype(o_ref.dtype)


def diff_attention_prompt(q, kt, v, rel_table, lams, subln_g, lam_init, T):
    b, l, _ = q.shape
    assert T >= REL_MAX_DIST and l % T == 0 and DIFF_DV == 2 * DIFF_DH
    vec = lambda a: a.astype(F32).reshape(1, -1)
    return pl.pallas_call(
        functools.partial(_attn_prompt_kernel, T=T, lam_init=lam_init),
        grid=(DIFF_HEADS, b, l // T),
        in_specs=[
            pl.BlockSpec(memory_space=pltpu.SMEM),
            pl.BlockSpec((1, T, DIFF_DV), lambda h, bi, qi: (bi, qi, h)),
            pl.BlockSpec((1, l // T, DIFF_DV, T), lambda h, bi, qi: (bi, 0, h, 0)),
            pl.BlockSpec((1, l, DIFF_DV), lambda h, bi, qi: (bi, 0, h)),
        ] + [_resident((1, DIFF_DH))] * 4 + [_resident((1, DIFF_DV))],
        out_specs=pl.BlockSpec((1, T, DIFF_DV), lambda h, bi, qi: (bi, qi, h)),
        out_shape=jax.ShapeDtypeStruct((b, l, DIFF_HEADS * DIFF_DV), BF16),
        scratch_shapes=[pltpu.VMEM((T, 2 * T), F32), pltpu.VMEM((2 * T, LANES), F32), pltpu.VMEM((2 * T, 2 * DIFF_DV), F32)],
        compiler_params=_cparams(("arbitrary", "arbitrary", "arbitrary")),
        name="diff_attention_prompt",
    )(rel_table.astype(F32), q, kt, v, *[vec(a) for a in lams], vec(subln_g))


DEC_RPH = SUBLANES


def _attn_sample_kernel(pt_ref, q_ref, kn_ref, vn_ref, tb_ref, lq1, lk1, lq2, lk2, sg_ref, *rest,
                        pps, n_pages, lq, lam_init):
    kt_refs, v_refs = rest[:pps], rest[pps:2 * pps]
    o_ref, qm_sc, m_sc, l_sc, acc_sc = rest[2 * pps:]
    s = pl.program_id(1)
    n_steps = n_pages // pps
    past = n_pages * PAGE_SIZE
    rows = DIFF_HEADS * DEC_RPH
    width = DIFF_HEADS * DIFF_DV
    rid = lax.broadcasted_iota(jnp.int32, (rows, 1), 0)
    row_tok = (rid % DEC_RPH) // 2

    def q_rows():
        r8 = lax.broadcasted_iota(jnp.int32, (DEC_RPH, width), 0)
        lane_pair = lax.broadcasted_iota(jnp.int32, (DEC_RPH, width), 1) // DIFF_DH
        qrep = jnp.zeros((DEC_RPH, width), F32)
        for t in range(lq):
            qrep = jnp.where(r8 // 2 == t, q_ref[0, t:t + 1, :], qrep)
        qrep = qrep * (DIFF_DH ** -0.5)
        return jnp.concatenate([jnp.where(lane_pair == 2 * hh + r8 % 2, qrep, 0.0) for hh in range(DIFF_HEADS)], axis=0)

    @pl.when(s == 0)
    def _():
        qm_sc[...] = q_rows().astype(BF16)
        m_sc[...] = jnp.full(m_sc.shape, -jnp.inf, F32)
        l_sc[...] = jnp.zeros(l_sc.shape, F32)
        acc_sc[...] = jnp.zeros(acc_sc.shape, F32)

    far_bias = tb_ref[:, REL_BUCKETS - 1:REL_BUCKETS]

    def pages_update(last_near):
        qm = qm_sc[...]
        sc = []
        for i in range(pps):
            sci = _dot(qm, kt_refs[i][0].astype(BF16))
            if last_near and i == pps - 1:
                kpos = (n_pages - 1) * PAGE_SIZE + lax.broadcasted_iota(jnp.int32, (1, PAGE_SIZE), 1)
                sc.append(sci + _t5_bias((past + row_tok) - kpos, lambda bkt: tb_ref[:, bkt:bkt + 1]))
            else:
                sc.append(sci + far_bias)
        m_prev = m_sc[...]
        m_new = m_prev
        for sci in sc:
            m_new = jnp.maximum(m_new, jnp.max(sci, axis=-1, keepdims=True))
        alpha = jnp.exp(m_prev - m_new)
        p = [jnp.exp(sci - m_new) for sci in sc]
        l_new = alpha * l_sc[...]
        for pi in p:
            l_new = l_new + jnp.sum(pi, axis=-1, keepdims=True)
        l_sc[...] = l_new
        m_sc[...] = m_new
        for hh in range(DIFF_HEADS):
            sl = slice(hh * DEC_RPH, (hh + 1) * DEC_RPH)
            ph = jnp.concatenate([pi[sl] for pi in p], axis=1).astype(BF16)
            vh = jnp.concatenate([v_refs[i][0, pl.ds(hh, PAGE_SIZE, stride=DIFF_HEADS), :].astype(BF16)
                                  for i in range(pps)], axis=0)
            acc_sc[sl, :] = alpha[sl] * acc_sc[sl, :] + _dot(ph, vh)

    @pl.when(s < n_steps - 1)
    def _():
        pages_update(False)

    @pl.when(s == n_steps - 1)
    def _():
        pages_update(True)

    @pl.when(s == n_steps)
    def _():
        qf = q_rows()
        sj = []
        for jn in range(lq):
            dist = row_tok - jn
            sc = jnp.sum(qf * kn_ref[0, jn:jn + 1, :], axis=-1, keepdims=True)
            sc = sc + _t5_bias(dist, lambda bkt: tb_ref[:, bkt:bkt + 1])
            sj.append(jnp.where(dist >= 0, sc, -jnp.inf))
        m_prev = m_sc[...]
        m_new = m_prev
        for sc in sj:
            m_new = jnp.maximum(m_new, sc)
        alpha = jnp.exp(m_prev - m_new)
        l_new = alpha * l_sc[...]
        acc = alpha * acc_sc[...]
        for jn, sc in enumerate(sj):
            p = jnp.exp(sc - m_new)
            l_new = l_new + p
            vrow = jnp.concatenate(
                [jnp.broadcast_to(vn_ref[0, jn:jn + 1, hh * DIFF_DV:(hh + 1) * DIFF_DV], (DEC_RPH, DIFF_DV))
                 for hh in range(DIFF_HEADS)], axis=0)
            acc = acc + p * vrow

        lam = _lambda(lq1, lk1, lq2, lk2, lam_init)
        coef = jnp.where(rid % 2 == 0, 1.0, -lam) / l_new
        a = acc * coef
        a = a + pltpu.roll(a, rows - 1, 0)
        res = _rms(a, sg_ref[...]) * (1.0 - lam_init)
        for hh in range(DIFF_HEADS):
            for t in range(lq):
                r = hh * DEC_RPH + 2 * t
                o_ref[0, t:t + 1, hh * DIFF_DV:(hh + 1) * DIFF_DV] = res[r:r + 1, :]


def diff_attention_sample(q, k_new, v_new, cache_kt, cache_v, page_table, rel_table, lams, subln_g, lam_init, pps=16):
    b, lq, width = q.shape
    n_pages = page_table.shape[1]
    assert n_pages % pps == 0 and 2 * lq <= DEC_RPH and n_pages // pps >= 1
    n_steps = n_pages // pps
    rows = DIFF_HEADS * DEC_RPH
    head_of_row = np.arange(rows) // DEC_RPH
    tb = jnp.pad(rel_table.astype(F32).T[head_of_row], ((0, 0), (0, LANES - REL_BUCKETS)))
    vec = lambda a: a.astype(F32).reshape(1, -1)

    def page_map(i):
        return lambda bi, s, pt: (pt[bi * n_pages + jnp.minimum(s, n_steps - 1) * pps + i], 0, 0)

    tok = pl.BlockSpec((1, lq, width), lambda bi, s, pt: (bi, 0, 0))
    const = lambda shape: pl.BlockSpec(shape, lambda bi, s, pt: (0,) * len(shape))
    kpage = [pl.BlockSpec((1, width, PAGE_SIZE), page_map(i)) for i in range(pps)]
    vpage = [pl.BlockSpec((1, PAGE_SIZE * DIFF_HEADS, DIFF_DV), page_map(i)) for i in range(pps)]
    grid_spec = pltpu.PrefetchScalarGridSpec(
        num_scalar_prefetch=1,
        grid=(b, n_steps + 1),
        in_specs=[tok, tok, tok, const((rows, LANES))] + [const((1, DIFF_DH))] * 4 + [const((1, DIFF_DV))] + kpage + vpage,
        out_specs=tok,
        scratch_shapes=[pltpu.VMEM((rows, width), BF16), pltpu.VMEM((rows, 1), F32), pltpu.VMEM((rows, 1), F32),
                        pltpu.VMEM((rows, DIFF_DV), F32)],
    )
    return pl.pallas_call(
        functools.partial(_attn_sample_kernel, pps=pps, n_pages=n_pages, lq=lq, lam_init=lam_init),
        grid_spec=grid_spec,
        out_shape=jax.ShapeDtypeStruct((b, lq, width), F32),
        compiler_params=_cparams(("parallel", "arbitrary")),
        name="diff_attention_sample",
    )(page_table.reshape(-1).astype(jnp.int32), q, k_new, v_new, tb, *[vec(a) for a in lams], vec(subln_g),
      *([cache_kt] * pps), *([cache_v] * pps))


def kernel(x_prompt, x_sample, state_ret, state_ssm, state_conv, cache_k_diff, cache_v_diff, page_table, norm_mix_g, norm_ffn_g, norm_final_g, ret_w_in, ret_w_out, ssd_w_in, ssd_conv_w, ssd_conv_b, ssd_dt_bias, ssd_A_log, ssd_D, ssd_norm_g, ssd_w_out, diff_w_in, diff_lam_q1, diff_lam_k1, diff_lam_q2, diff_lam_k2, diff_subln_g, diff_w_out, rel_bias_table, ffn_w_in, ffn_w_out):
    kinds = tuple(i % 3 for i in range(DEPTH))
    n_pages = page_table.shape[1]
    bf = lambda a: a.astype(BF16)
    ret_wi, ret_wo = bf(ret_w_in), bf(ret_w_out)
    ssd_wmain = bf(ssd_w_in[:, :, :SSD_INNER + SSD_CONV_CH])
    ssd_wxbc = ssd_wmain[:, :, SSD_INNER:]
    ssd_wdt = bf(jnp.pad(ssd_w_in[:, :, SSD_INNER + SSD_CONV_CH:], ((0, 0), (0, 0), (0, LANES - SSD_HEADS))))
    ssd_wo = bf(ssd_w_out)
    diff_wi, diff_wo = bf(diff_w_in), bf(diff_w_out)
    ffn_wi, ffn_wo = bf(ffn_w_in), bf(ffn_w_out)

    n_ret = kinds.count(0)

    def run_group(x, sample):
        b, l, d = x.shape
        t = b * l
        tm = next((c for c in (1024, 512) if t % c == 0), t)
        ret_c = RET_CHUNK if l % RET_CHUNK == 0 else l
        ssd_c = SSD_CHUNK if l % SSD_CHUNK == 0 else l
        act = F32 if sample else BF16
        offset = n_pages * PAGE_SIZE if sample else 0
        rope_rows = l if l % RET_CHUNK == 0 else -(-l // BF16_ROWS) * BF16_ROWS
        cos, sin = rope_tables(offset, rope_rows)
        ret_new, ssm_new, conv_new, k_new, v_new = None, [], [], [], []
        h = x.reshape(t, d)
        for i in range(DEPTH):
            kind = kinds[i]
            j = kinds[:i].count(kind)
            g = norm_mix_g[i]
            if kind == 0:
                n_in = ret_wi.shape[2]
                (qkvg,) = norm_linear(h, g, ret_wi, j, [(0, n_in, act)], tm)
                s0, j0 = (state_ret, j) if sample else (jnp.zeros((1, b, RET_HEADS, RET_DK, RET_DV), F32), 0)
                y, ret_new = retention_core(qkvg.reshape(b, l, n_in), cos, sin, s0, j0, ret_new, j, n_ret, ret_c, act)
                y, wmix = y.reshape(t, RET_VDIM), ret_wo
            elif kind == 1:
                n_in = SSD_INNER + SSD_CONV_CH
                zxbc, dt_raw = norm_linear(h, g, ssd_wmain, j, [(0, n_in, act)], tm, side=[(ssd_wdt, F32)])
                zxbc = zxbc.reshape(b, l, n_in)
                if sample:
                    buf, s0 = state_conv[j], state_ssm[j]
                    conv_new.append(zxbc[:, l - (SSD_CONV - 1):, SSD_INNER:])
                else:
                    buf = jnp.zeros((b, SSD_CONV - 1, SSD_CONV_CH), F32)
                    s0 = jnp.zeros((b, SSD_HEADS, SSD_HEADDIM, SSD_STATE), F32)
                    tail = h.reshape(b, l, d)[:, l - (SSD_CONV - 1):].reshape(b * (SSD_CONV - 1), d)
                    rows = -(-tail.shape[0] // BF16_ROWS) * BF16_ROWS
                    tail = jnp.pad(tail, ((0, rows - tail.shape[0]), (0, 0)))
                    (xbc_tail,) = norm_linear(tail, g, ssd_wxbc, j, [(0, SSD_CONV_CH, F32)], rows)
                    conv_new.append(xbc_tail[:b * (SSD_CONV - 1)].reshape(b, SSD_CONV - 1, SSD_CONV_CH))
                y, s1 = ssd_core(zxbc, dt_raw.reshape(b, l, LANES), buf, s0, ssd_conv_w[j], ssd_conv_b[j],
                                 ssd_dt_bias[j], ssd_A_log[j], ssd_D[j], ssd_norm_g[j], ssd_c, act)
                ssm_new.append(s1)
                y, wmix = y.reshape(t, SSD_INNER), ssd_wo
            else:
                lam_init = 0.8 - 0.6 * math.exp(-0.3 * i)
                lams = (diff_lam_q1[j], diff_lam_k1[j], diff_lam_q2[j], diff_lam_k2[j])
                if sample:
                    q, kn, vn = norm_linear(h, g, diff_wi, j, [(0, d, F32), (d, 2 * d, F32), (2 * d, 3 * d, F32)], tm)
                    cache_kt = jnp.transpose(cache_k_diff[j], (0, 2, 3, 4, 1)).reshape(-1, d, PAGE_SIZE)
                    cache_v = cache_v_diff[j].reshape(-1, PAGE_SIZE * DIFF_HEADS, DIFF_DV)
                    y = diff_attention_sample(q.reshape(b, l, d), kn.reshape(b, l, d), vn.reshape(b, l, d), cache_kt,
                                              cache_v, page_table, rel_bias_table, lams, diff_subln_g[j], lam_init)
                    k_new.append(kn.reshape(b, l, DIFF_HEADS, 2, DIFF_DH))
                    v_new.append(vn.reshape(b, l, DIFF_HEADS, DIFF_DV))
                else:
                    w = diff_wi[j]
                    q, kt32, kt16, v32, v16 = diff_in_proj(h.reshape(b, l, d), g, w[:, :d], w[:, d:2 * d].T, w[:, 2 * d:],
                                                           ATTN_TILE, ATTN_TILE)
                    y = diff_attention_prompt(q, kt16, v16, rel_bias_table, lams, diff_subln_g[j], lam_init, ATTN_TILE)
                    k_new.append(jnp.transpose(kt32.reshape(b, DIFF_HEADS, 2, DIFF_DH, l), (0, 4, 1, 2, 3)))
                    v_new.append(v32.reshape(b, l, DIFF_HEADS, DIFF_DV))
                y, wmix = y.reshape(t, d), diff_wo
            h = out_ffn(y, wmix, j, h, norm_ffn_g[i], ffn_wi, ffn_wo, i, tm,
                        g_final=norm_final_g if i == DEPTH - 1 else None)
        return h.reshape(b, l, d), ret_new, jnp.stack(ssm_new), jnp.stack(conv_new), jnp.stack(k_new), jnp.stack(v_new)

    y_p, ret_p, ssm_p, conv_p, k_p, v_p = run_group(x_prompt, False)
    y_s, ret_s, ssm_s, conv_s, k_s, v_s = run_group(x_sample, True)
    return (y_p, y_s, ret_p, ssm_p, conv_p, k_p, v_p, ret_s, ssm_s, conv_s, k_s, v_s)
```

```python
import functools
import math

import jax
import jax.numpy as jnp
import numpy as np
from jax import lax
from jax.experimental import pallas as pl
from jax.experimental.pallas import tpu as pltpu

F32 = jnp.float32
BF16 = jnp.bfloat16

D_MODEL = 1024
DEPTH = 4
PAGE_SIZE = 128
EPS = 1e-6
RET_CHUNK = 256
SSD_CHUNK = 128
ATTN_TILE = 512

RET_HEADS = 4
RET_DK = D_MODEL // RET_HEADS
RET_DV = 2 * RET_DK
RET_VDIM = RET_HEADS * RET_DV
ROPE_BASE = 10000.0
ROPE_HALF = RET_DK // 2

SSD_INNER = 2 * D_MODEL
SSD_HEADDIM = 64
SSD_HEADS = SSD_INNER // SSD_HEADDIM
SSD_GROUPS = 8
SSD_HPG = SSD_HEADS // SSD_GROUPS
SSD_STATE = 128
SSD_CONV = 4
SSD_BC = SSD_GROUPS * SSD_STATE
SSD_CONV_CH = SSD_INNER + 2 * SSD_BC
SSD_GW = SSD_HPG * SSD_HEADDIM

DIFF_HEADS = 8
DIFF_DH = D_MODEL // DIFF_HEADS // 2
DIFF_DV = 2 * DIFF_DH
REL_BUCKETS = 32
REL_MAX_DIST = 128

FFN_HIDDEN = -(-8 * D_MODEL // (3 * 256)) * 256

V7X_VMEM_BYTES = 64 * 1024 * 1024
LANES = 128
SUBLANES = 8
BF16_ROWS = 16
VMEM_LIMIT = V7X_VMEM_BYTES * 7 // 8


def _cparams(sem):
    return pltpu.CompilerParams(dimension_semantics=sem, vmem_limit_bytes=VMEM_LIMIT)


def _dot(a, b):
    return jnp.dot(a, b, preferred_element_type=F32)


def _dot_nt(a, b):
    return lax.dot_general(a, b, (((1,), (1,)), ((), ())), preferred_element_type=F32)


def _dot_tn(a, b):
    return lax.dot_general(a, b, (((0,), (0,)), ((), ())), preferred_element_type=F32)


def _rms(x, g):
    return x * lax.rsqrt(jnp.mean(x * x, axis=-1, keepdims=True) + EPS) * g


def _silu(x):
    half = 0.5 * x
    return half * jnp.tanh(half) + half


def _resident(shape):
    return pl.BlockSpec(shape, lambda *_: (0,) * len(shape), pipeline_mode=pl.Buffered(1))


def _resident_layer(stacked, layer):
    tail = stacked.shape[1:]
    return pl.BlockSpec((None,) + tail, lambda *_: (layer,) + (0,) * len(tail), pipeline_mode=pl.Buffered(1))


def _norm_linear_kernel(x_ref, g_ref, w_ref, *rest, outs, tn, side, tail):
    side_w = rest[:side]
    o_refs = rest[side:side + len(outs)]
    side_o = rest[side + len(outs):side + len(outs) + side]
    xn = _rms(x_ref[...], g_ref[...]).astype(BF16)
    n = w_ref.shape[1]
    tm = xn.shape[0]
    for c0 in range(0, n, tn):
        acc = _dot(xn, w_ref[:, c0:c0 + tn])
        for o_ref, (lo, hi, _) in zip(o_refs, outs):
            if lo <= c0 and c0 + tn <= hi:
                o_ref[:, c0 - lo:c0 - lo + tn] = acc.astype(o_ref.dtype)
        if tail is not None and tail[0] <= c0 and c0 + tn <= tail[1]:
            rest[-1][:, c0 - tail[0]:c0 - tail[0] + tn] = acc[tm - SUBLANES:tm]
    for w2_ref, o2_ref in zip(side_w, side_o):
        o2_ref[...] = _dot(xn, w2_ref[...]).astype(o2_ref.dtype)


def norm_linear(x, g, w_stack, layer, outs, tm, tn=512, side=(), tail=None):
    t, d = x.shape
    n = w_stack.shape[2]
    tn = min(tn, n)
    assert t % tm == 0 and n % tn == 0 and all(lo % tn == 0 and hi % tn == 0 for lo, hi, _ in outs)
    row = lambda width: pl.BlockSpec((tm, width), lambda i: (i, 0))
    return pl.pallas_call(
        functools.partial(_norm_linear_kernel, outs=tuple(outs), tn=tn, side=len(side), tail=tail),
        grid=(t // tm,),
        in_specs=[row(d), _resident((1, d)), _resident_layer(w_stack, layer)] + [_resident_layer(w2, layer) for w2, _ in side],
        out_specs=[row(hi - lo) for lo, hi, _ in outs] + [row(w2.shape[2]) for w2, _ in side]
        + ([pl.BlockSpec((SUBLANES, tail[1] - tail[0]), lambda i: (i, 0))] if tail else []),
        out_shape=[jax.ShapeDtypeStruct((t, hi - lo), dt) for lo, hi, dt in outs]
        + [jax.ShapeDtypeStruct((t, w2.shape[2]), dt) for w2, dt in side]
        + ([jax.ShapeDtypeStruct((t // tm * SUBLANES, tail[1] - tail[0]), F32)] if tail else []),
        compiler_params=_cparams(("parallel",)),
        name="norm_linear",
    )(x, g.reshape(1, d), w_stack, *[w2 for w2, _ in side])


def _out_ffn_kernel(y_ref, wmix_ref, h_ref, g_ref, wgu_ref, wo_ref, *rest, th, final):
    o_ref = rest[-1]
    h = h_ref[...] + _dot(y_ref[...].astype(BF16), wmix_ref[...])
    xn = _rms(h, g_ref[...]).astype(BF16)
    hidden = wo_ref.shape[0]
    o_ref[...] = h
    for c0 in range(0, hidden, th):
        gate = _dot(xn, wgu_ref[:, c0:c0 + th])
        up = _dot(xn, wgu_ref[:, hidden + c0:hidden + c0 + th])
        act = (_silu(gate) * up).astype(BF16)
        o_ref[...] += _dot(act, wo_ref[c0:c0 + th, :])
    if final:
        o_ref[...] = _rms(o_ref[...], rest[0][...])


def out_ffn(y, wmix_stack, j, h, g, wgu_stack, wo_stack, i, tm, g_final=None, th=256):
    t, k = y.shape
    d = h.shape[1]
    hidden = wo_stack.shape[1]
    assert hidden % th == 0 and t % tm == 0
    final = g_final is not None
    extra = [g_final.reshape(1, d)] if final else []
    return pl.pallas_call(
        functools.partial(_out_ffn_kernel, th=th, final=final),
        grid=(t // tm,),
        in_specs=[pl.BlockSpec((tm, k), lambda m: (m, 0)), _resident_layer(wmix_stack, j),
                  pl.BlockSpec((tm, d), lambda m: (m, 0)), _resident((1, d)), _resident_layer(wgu_stack, i),
                  _resident_layer(wo_stack, i)] + [_resident((1, d))] * len(extra),
        out_specs=pl.BlockSpec((tm, d), lambda m: (m, 0)),
        out_shape=jax.ShapeDtypeStruct((t, d), F32),
        compiler_params=_cparams(("parallel",)),
        name="out_ffn",
    )(y, wmix_stack, h, g.reshape(1, d), wgu_stack, wo_stack, *extra)


def _rope_kernel(inv_ref, cos_ref, sin_ref, *, offset):
    rows = cos_ref.shape[0]
    pos = (offset + pl.program_id(0) * rows + lax.broadcasted_iota(jnp.int32, (rows, 1), 0)).astype(F32)
    ang = pos * inv_ref[...]
    cos_ref[...] = jnp.cos(ang)
    sin_ref[...] = jnp.sin(ang)


def rope_tables(offset, rows):
    inv = (1.0 / (ROPE_BASE ** jnp.linspace(0.0, 1.0, ROPE_HALF, dtype=F32))).reshape(1, ROPE_HALF)
    tr = min(rows, 512)
    return pl.pallas_call(
        functools.partial(_rope_kernel, offset=offset),
        grid=(rows // tr,),
        in_specs=[_resident((1, ROPE_HALF))],
        out_specs=[pl.BlockSpec((tr, ROPE_HALF), lambda i: (i, 0))] * 2,
        out_shape=[jax.ShapeDtypeStruct((rows, ROPE_HALF), F32)] * 2,
        compiler_params=_cparams(("parallel",)),
        name="rope_tables",
    )(inv)


def _ret_kernel(q_ref, k_ref, v_ref, g_ref, cos_ref, sin_ref, s0_ref, *rest, C, c_real, carry, slot, n_chunks):
    y_ref, st_ref = rest[carry:carry + 2]
    decay_sc = rest[carry + 2]
    stage = rest[carry + 3:]
    c = pl.program_id(1)
    log_gamma = [math.log(1.0 - 2.0 ** (-5.0 - hd)) for hd in range(RET_HEADS)]
    i = lax.broadcasted_iota(jnp.int32, (C, 1), 0).astype(F32)
    j = lax.broadcasted_iota(jnp.int32, (1, C), 1).astype(F32)

    @pl.when(c == 0)
    def _():
        st_ref[slot] = s0_ref[...]
        rel = i - j
        for hd in range(RET_HEADS):
            decay_sc[hd] = jnp.where(rel >= 0, jnp.exp(jnp.maximum(rel, 0.0) * log_gamma[hd]), 0.0)

    if c_real == C:
        q_src, k_src, v_src = q_ref.at[0], k_ref.at[0], v_ref.at[0]
    else:
        for ref, st in zip((q_ref, k_ref, v_ref), stage):
            st[...] = jnp.zeros(st.shape, F32)
            st[0:c_real, :] = ref[0].astype(F32)
        q_src, k_src, v_src = stage
    cos = cos_ref[...]
    sin = sin_ref[...]

    def rot(x):
        x1, x2 = x[:, :ROPE_HALF], x[:, ROPE_HALF:]
        return jnp.concatenate([x1 * cos - x2 * sin, x2 * cos + x1 * sin], axis=-1)

    for hd in range(RET_HEADS):
        lg = log_gamma[hd]
        ks = slice(hd * RET_DK, (hd + 1) * RET_DK)
        vs = slice(hd * RET_DV, (hd + 1) * RET_DV)
        qr = rot(q_src[:, ks].astype(F32)).astype(BF16)
        kr = rot(k_src[:, ks].astype(F32)) * (RET_DK ** -0.5)
        v = v_src[:, vs].astype(BF16)
        scores = _dot_nt(qr, kr.astype(BF16)) * decay_sc[hd]
        s_prev = st_ref[slot, 0, hd]
        o = _dot(scores.astype(BF16), v) + _dot(qr, s_prev.astype(BF16)) * jnp.exp((i + 1.0) * lg)
        kw = (kr * jnp.exp((c_real - 1.0 - i) * lg)).astype(BF16)
        st_ref[slot, 0, hd] = math.exp(c_real * lg) * s_prev + _dot_tn(kw, v)
        o = o * lax.rsqrt(jnp.mean(o * o, axis=-1, keepdims=True) + EPS)
        y_ref[0, :, vs] = (_silu(g_ref[0, :, vs].astype(F32)) * o[0:c_real]).astype(y_ref.dtype)

    if st_ref.shape[0] > 1:
        @pl.when(c == n_chunks - 1)
        def _():
            for other in range(st_ref.shape[0]):
                if other != slot:
                    st_ref[other] = st_ref[slot]


def retention_core(qkvg, cos, sin, s0_stack, j0, s1_stack, j1, n_layers, c_real, out_dtype):
    b, l, _ = qkvg.shape
    C = cos.shape[0] if c_real < RET_CHUNK else RET_CHUNK
    nc = l // c_real
    qk_w, v_w = RET_HEADS * RET_DK, RET_VDIM
    k_blk, v_blk, g_blk = 1, 2 * qk_w // v_w, 2 * qk_w // v_w + 1
    stage = [] if c_real == C else [pltpu.VMEM((C, qk_w), F32), pltpu.VMEM((C, qk_w), F32), pltpu.VMEM((C, v_w), F32)]
    carry = s1_stack is not None
    st_block, st_first, slot = (1, j1, 0) if carry else (n_layers, 0, j1)
    return pl.pallas_call(
        functools.partial(_ret_kernel, C=C, c_real=c_real, carry=int(carry), slot=slot, n_chunks=nc),
        grid=(b, nc),
        in_specs=[
            pl.BlockSpec((1, c_real, qk_w), lambda bi, c: (bi, c, 0)),
            pl.BlockSpec((1, c_real, qk_w), lambda bi, c: (bi, c, k_blk)),
            pl.BlockSpec((1, c_real, v_w), lambda bi, c: (bi, c, v_blk)),
            pl.BlockSpec((1, c_real, v_w), lambda bi, c: (bi, c, g_blk)),
            pl.BlockSpec((C, ROPE_HALF), lambda bi, c: (c, 0)),
            pl.BlockSpec((C, ROPE_HALF), lambda bi, c: (c, 0)),
            pl.BlockSpec((None, 1, RET_HEADS, RET_DK, RET_DV), lambda bi, c: (j0, bi, 0, 0, 0)),
        ] + ([pl.BlockSpec(memory_space=pl.ANY)] if carry else []),
        out_specs=[
            pl.BlockSpec((1, c_real, v_w), lambda bi, c: (bi, c, 0)),
            pl.BlockSpec((st_block, 1, RET_HEADS, RET_DK, RET_DV), lambda bi, c: (st_first, bi, 0, 0, 0)),
        ],
        out_shape=[jax.ShapeDtypeStruct((b, l, RET_VDIM), out_dtype),
                   jax.ShapeDtypeStruct((n_layers, b, RET_HEADS, RET_DK, RET_DV), F32)],
        scratch_shapes=[pltpu.VMEM((RET_HEADS, C, C), F32)] + stage,
        input_output_aliases={7: 1} if carry else {},
        compiler_params=_cparams(("parallel", "arbitrary")),
        name="retention_core",
    )(qkvg, qkvg, qkvg, qkvg, cos, sin, s0_stack, *([s1_stack] if carry else []))


def _softplus(x):
    return jnp.maximum(x, 0.0) + jnp.log1p(jnp.exp(-jnp.abs(x)))


def _cumsum(x, axis):
    n = x.shape[axis]
    idx = lax.broadcasted_iota(jnp.int32, x.shape, axis)
    s = 1
    while s < n:
        x = x + jnp.where(idx >= s, pltpu.roll(x, s, axis), 0.0)
        s *= 2
    return x


def _ssd_kernel(z_ref, x_ref, bc_ref, dt_ref, dtt_ref, cw_ref, cb_ref, dtb_ref, dtbt_ref, al_ref, alt_ref,
                dsk_ref, ng_ref, buf_ref, s0_ref, y_ref, s1_ref, extx, extbc, xs_ref, bcs_ref, *, C, c_real):
    c = pl.program_id(1)
    halo = SUBLANES

    @pl.when(c == 0)
    def _():
        s1_ref[...] = s0_ref[...]
        extx[...] = jnp.zeros(extx.shape, F32)
        extbc[...] = jnp.zeros(extbc.shape, F32)
        extx[0:halo, :] = buf_ref[0, :, 0:SSD_INNER]
        extbc[0:halo, :] = buf_ref[0, :, SSD_INNER:SSD_CONV_CH]

    extx[halo:halo + c_real, :] = x_ref[0].astype(F32)
    extbc[halo:halo + c_real, :] = bc_ref[0].astype(F32)

    cw = 512
    first = halo - (SSD_CONV - 1)
    for ext, dst, ch0 in ((extx, xs_ref, 0), (extbc, bcs_ref, SSD_INNER)):
        for col in range(0, SSD_INNER, cw):
            acc = cb_ref[:, ch0 + col:ch0 + col + cw]
            for tap in range(SSD_CONV):
                acc = acc + ext[first + tap:first + tap + C, col:col + cw] * cw_ref[tap:tap + 1, ch0 + col:ch0 + col + cw]
            dst[:, col:col + cw] = _silu(acc).astype(dst.dtype)
    extx[0:halo, :] = extx[C:C + halo, :]
    extbc[0:halo, :] = extbc[C:C + halo, :]

    row = lax.broadcasted_iota(jnp.int32, (C, 1), 0)
    col_i = lax.broadcasted_iota(jnp.int32, (1, C), 1)
    dt = jnp.where(row < c_real, _softplus(dt_ref[0, 0] + dtb_ref[...]), 0.0)
    dtt = jnp.where(col_i < c_real, _softplus(dtt_ref[0, 0] + dtbt_ref[...]), 0.0)
    cs = _cumsum(dt * (-jnp.exp(al_ref[...])), 0)
    cst = _cumsum(dtt * (-jnp.exp(alt_ref[...])), 1)
    causal = row >= col_i
    lane_r = lax.broadcasted_iota(jnp.int32, (1, SSD_GW), 1) // SSD_HEADDIM
    sub_r = lax.broadcasted_iota(jnp.int32, (SSD_GW, 1), 0) // SSD_HEADDIM

    def by_head(vals, sel):
        out = vals[SSD_HPG - 1]
        for r in range(SSD_HPG - 2, -1, -1):
            out = jnp.where(sel == r, vals[r], out)
        return out

    for g in range(SSD_GROUPS):
        bm = bcs_ref[:, g * SSD_STATE:(g + 1) * SSD_STATE]
        cm = bcs_ref[:, SSD_BC + g * SSD_STATE:SSD_BC + (g + 1) * SSD_STATE]
        cb = _dot_nt(cm, bm)
        h0 = s1_ref[0, g * SSD_HPG:(g + 1) * SSD_HPG].reshape(SSD_GW, SSD_STATE)
        xg = xs_ref[:, g * SSD_GW:(g + 1) * SSD_GW]
        heads = range(g * SSD_HPG, (g + 1) * SSD_HPG)
        cs_cols = [cs[:, hh:hh + 1] for hh in heads]
        cs_last = [cs[C - 1:C, hh:hh + 1] for hh in heads]
        yg = _dot_nt(cm, h0.astype(BF16)) * by_head([jnp.exp(cc) for cc in cs_cols], lane_r)
        for r, hh in enumerate(heads):
            seg = cs_cols[r] - cst[hh:hh + 1, :]
            w = cb * jnp.exp(jnp.where(causal, seg, -jnp.inf)) * dtt[hh:hh + 1, :]
            yg = yg + _dot(w.astype(BF16), jnp.where(lane_r == r, xg, 0.0).astype(BF16))
        w_end = by_head([jnp.exp(cl - cc) * dt[:, hh:hh + 1] for cl, cc, hh in zip(cs_last, cs_cols, heads)], lane_r)
        h1 = by_head([jnp.exp(cl) for cl in cs_last], sub_r) * h0 + _dot_tn((xg * w_end).astype(BF16), bm)
        s1_ref[0, g * SSD_HPG:(g + 1) * SSD_HPG] = h1.reshape(SSD_HPG, SSD_HEADDIM, SSD_STATE)
        sl = slice(g * SSD_GW, (g + 1) * SSD_GW)
        yo = (yg[0:c_real] + dsk_ref[:, sl] * xg[0:c_real]) * _silu(z_ref[0, :, sl].astype(F32))
        yo = yo * lax.rsqrt(jnp.mean(yo * yo, axis=-1, keepdims=True) + EPS) * ng_ref[:, sl]
        y_ref[0, :, sl] = yo.astype(y_ref.dtype)


def ssd_core(zxbc, dt_raw, conv_buf, s0, conv_w, conv_b, dt_bias, a_log, d_skip, norm_g, c_real, out_dtype):
    b, l, _ = zxbc.shape
    C = SSD_CHUNK
    nc = l // c_real
    halo = SUBLANES
    dt4 = dt_raw.reshape(b, nc, c_real, LANES)
    dtt = jnp.swapaxes(dt4[..., :SSD_HEADS], -1, -2)
    if c_real < C:
        dt4 = jnp.pad(dt4, ((0, 0), (0, 0), (0, C - c_real), (0, 0)))
        dtt = jnp.pad(dtt, ((0, 0), (0, 0), (0, 0), (0, C - c_real)))
    lane_pad = LANES - SSD_HEADS
    buf8 = jnp.pad(conv_buf.astype(F32), ((0, 0), (halo - (SSD_CONV - 1), 0), (0, 0)))
    small = [
        conv_w.astype(F32), conv_b.reshape(1, -1).astype(F32),
        jnp.pad(dt_bias.astype(F32), (0, lane_pad)).reshape(1, LANES), dt_bias.astype(F32).reshape(SSD_HEADS, 1),
        jnp.pad(a_log.astype(F32), (0, lane_pad)).reshape(1, LANES), a_log.astype(F32).reshape(SSD_HEADS, 1),
        jnp.repeat(d_skip.astype(F32), SSD_HEADDIM).reshape(1, SSD_INNER), norm_g.astype(F32).reshape(1, SSD_INNER),
    ]
    w = SSD_INNER
    return pl.pallas_call(
        functools.partial(_ssd_kernel, C=C, c_real=c_real),
        grid=(b, nc),
        in_specs=[
            pl.BlockSpec((1, c_real, w), lambda bi, c: (bi, c, 0)),
            pl.BlockSpec((1, c_real, w), lambda bi, c: (bi, c, 1)),
            pl.BlockSpec((1, c_real, w), lambda bi, c: (bi, c, 2)),
            pl.BlockSpec((1, 1, C, LANES), lambda bi, c: (bi, c, 0, 0)),
            pl.BlockSpec((1, 1, SSD_HEADS, C), lambda bi, c: (bi, c, 0, 0)),
        ] + [_resident(a.shape) for a in small] + [
            pl.BlockSpec((1, halo, SSD_CONV_CH), lambda bi, c: (bi, 0, 0)),
            pl.BlockSpec((1, SSD_HEADS, SSD_HEADDIM, SSD_STATE), lambda bi, c: (bi, 0, 0, 0)),
        ],
        out_specs=[
            pl.BlockSpec((1, c_real, w), lambda bi, c: (bi, c, 0)),
            pl.BlockSpec((1, SSD_HEADS, SSD_HEADDIM, SSD_STATE), lambda bi, c: (bi, 0, 0, 0)),
        ],
        out_shape=[jax.ShapeDtypeStruct((b, l, SSD_INNER), out_dtype),
                   jax.ShapeDtypeStruct((b, SSD_HEADS, SSD_HEADDIM, SSD_STATE), F32)],
        scratch_shapes=[pltpu.VMEM((C + halo, w), F32), pltpu.VMEM((C + halo, w), F32),
                        pltpu.VMEM((C, w), F32), pltpu.VMEM((C, w), BF16)],
        compiler_params=_cparams(("parallel", "arbitrary")),
        name="ssd_core",
    )(zxbc, zxbc, zxbc, dt4, dtt, *small, buf8, s0)


def _t5_bias(dist, table_at):
    n = jnp.maximum(dist, 0)
    exact = REL_BUCKETS // 2
    nf = jnp.maximum(n, 1).astype(F32)
    large = exact + (jnp.log(nf / exact) / math.log(REL_MAX_DIST / exact) * (REL_BUCKETS - exact)).astype(jnp.int32)
    bucket = jnp.where(n < exact, n, jnp.minimum(large, REL_BUCKETS - 1))
    bias = jnp.zeros(dist.shape, F32)
    for bkt in range(REL_BUCKETS):
        bias = jnp.where(bucket == bkt, table_at(bkt), bias)
    return bias


def _lambda(lq1, lk1, lq2, lk2, lam_init):
    s1 = jnp.sum(lq1[...] * lk1[...], axis=-1, keepdims=True)
    s2 = jnp.sum(lq2[...] * lk2[...], axis=-1, keepdims=True)
    return jnp.exp(s1) - jnp.exp(s2) + lam_init


def _diff_in_proj_kernel(x_ref, g_ref, wq_ref, wkt_ref, wv_ref, q_ref, kt32_ref, kt16_ref, v32_ref, v16_ref, *, tn, kb):
    xn = _rms(x_ref[0], g_ref[...]).astype(BF16)
    d = wq_ref.shape[1]
    tm = xn.shape[0]
    for c0 in range(0, d, tn):
        q_ref[0, :, c0:c0 + tn] = (_dot(xn, wq_ref[:, c0:c0 + tn]) * (DIFF_DH ** -0.5)).astype(BF16)
        v = _dot(xn, wv_ref[:, c0:c0 + tn])
        v32_ref[0, :, c0:c0 + tn] = v
        v16_ref[0, :, c0:c0 + tn] = v.astype(BF16)
        kt = _dot_nt(wkt_ref[c0:c0 + tn, :], xn)
        kt32_ref[0, c0:c0 + tn, :] = kt
        for s in range(tm // kb):
            kt16_ref[0, s, c0:c0 + tn, :] = kt[:, s * kb:(s + 1) * kb].astype(BF16)


def diff_in_proj(x, g, wq, wkt, wv, tm, kb):
    b, l, d = x.shape
    assert l % tm == 0 and tm % kb == 0
    tok = lambda: pl.BlockSpec((1, tm, d), lambda bi, i: (bi, i, 0))
    return pl.pallas_call(
        functools.partial(_diff_in_proj_kernel, tn=512, kb=kb),
        grid=(b, l // tm),
        in_specs=[tok(), _resident((1, d)), _resident((d, d)), _resident((d, d)), _resident((d, d))],
        out_specs=[tok(), pl.BlockSpec((1, d, tm), lambda bi, i: (bi, 0, i)),
                   pl.BlockSpec((1, tm // kb, d, kb), lambda bi, i: (bi, i, 0, 0)), tok(), tok()],
        out_shape=[jax.ShapeDtypeStruct((b, l, d), BF16), jax.ShapeDtypeStruct((b, d, l), F32),
                   jax.ShapeDtypeStruct((b, l // kb, d, kb), BF16), jax.ShapeDtypeStruct((b, l, d), F32),
                   jax.ShapeDtypeStruct((b, l, d), BF16)],
        compiler_params=_cparams(("parallel", "parallel")),
        name="diff_in_proj",
    )(x, g.reshape(1, d), wq, wkt, wv)


def _attn_prompt_kernel(tbl_ref, q_ref, kt_ref, v_ref, lq1, lk1, lq2, lk2, sg_ref, o_ref,
                        bias_sc, m_sc, a_sc, *, T, lam_init):
    hd = pl.program_id(0)
    bi = pl.program_id(1)
    qi = pl.program_id(2)

    @pl.when((bi == 0) & (qi == 0))
    def _():
        i = lax.broadcasted_iota(jnp.int32, (T, 2 * T), 0)
        j = lax.broadcasted_iota(jnp.int32, (T, 2 * T), 1)
        dist = i - j + T
        bias_sc[...] = jnp.where(dist >= 0, _t5_bias(dist, lambda bkt: tbl_ref[bkt, hd]), -jnp.inf)

    lane = lax.broadcasted_iota(jnp.int32, (1, DIFF_DV), 1)
    q = q_ref[0]
    zero = jnp.zeros((), q.dtype)
    q2 = jnp.concatenate([jnp.where(lane < DIFF_DH, q, zero), jnp.where(lane >= DIFF_DH, q, zero)], axis=0)
    m_sc[...] = jnp.full(m_sc.shape, -jnp.inf, F32)
    a_sc[...] = jnp.zeros(a_sc.shape, F32)

    def update(kb, bias):
        vt = v_ref[0, pl.ds(pl.multiple_of(kb * T, T), T), :]
        vx = jnp.concatenate([vt, jnp.ones((T, DIFF_DV), BF16)], axis=1)
        s = _dot(q2, kt_ref[0, kb])
        s = s + (jnp.concatenate([bias, bias], axis=0) if getattr(bias, "ndim", 0) == 2 else bias)
        m_prev = m_sc[...]
        m_new = jnp.maximum(m_prev, jnp.max(s, axis=-1, keepdims=True))
        alpha = jnp.exp(m_prev - m_new)
        p = jnp.concatenate([jnp.exp(s[:, c:c + LANES] - m_new) for c in range(0, T, LANES)], axis=1)
        a_sc[...] = jnp.concatenate([alpha, alpha], axis=1) * a_sc[...] + _dot(p.astype(BF16), vx)
        m_sc[...] = m_new

    far_bias = tbl_ref[REL_BUCKETS - 1, hd]

    def far_body(kb, carry):
        update(kb, far_bias)
        return carry

    lax.fori_loop(0, jnp.maximum(qi - 1, 0), far_body, 0)

    @pl.when(qi >= 1)
    def _():
        update(qi - 1, bias_sc[:, 0:T])

    update(qi, bias_sc[:, T:2 * T])

    lam = _lambda(lq1, lk1, lq2, lk2, lam_init)
    o = a_sc[:T, :DIFF_DV] / a_sc[:T, DIFF_DV:] - lam * (a_sc[T:, :DIFF_DV] / a_sc[T:, DIFF_DV:])
    o_ref[0] = (_rms(o, sg_ref[...]) * (1.0 - lam_init)).astype(o_ref.dtype)


def diff_attention_prompt(q, kt, v, rel_table, lams, subln_g, lam_init, T):
    b, l, _ = q.shape
    assert T >= REL_MAX_DIST and l % T == 0 and DIFF_DV == 2 * DIFF_DH
    vec = lambda a: a.astype(F32).reshape(1, -1)
    return pl.pallas_call(
        functools.partial(_attn_prompt_kernel, T=T, lam_init=lam_init),
        grid=(DIFF_HEADS, b, l // T),
        in_specs=[
            pl.BlockSpec(memory_space=pltpu.SMEM),
            pl.BlockSpec((1, T, DIFF_DV), lambda h, bi, qi: (bi, qi, h)),
            pl.BlockSpec((1, l // T, DIFF_DV, T), lambda h, bi, qi: (bi, 0, h, 0)),
            pl.BlockSpec((1, l, DIFF_DV), lambda h, bi, qi: (bi, 0, h)),
        ] + [_resident((1, DIFF_DH))] * 4 + [_resident((1, DIFF_DV))],
        out_specs=pl.BlockSpec((1, T, DIFF_DV), lambda h, bi, qi: (bi, qi, h)),
        out_shape=jax.ShapeDtypeStruct((b, l, DIFF_HEADS * DIFF_DV), BF16),
        scratch_shapes=[pltpu.VMEM((T, 2 * T), F32), pltpu.VMEM((2 * T, LANES), F32), pltpu.VMEM((2 * T, 2 * DIFF_DV), F32)],
        compiler_params=_cparams(("arbitrary", "arbitrary", "arbitrary")),
        name="diff_attention_prompt",
    )(rel_table.astype(F32), q, kt, v, *[vec(a) for a in lams], vec(subln_g))


DEC_RPH = SUBLANES


def _attn_sample_kernel(pt_ref, q_ref, kn_ref, vn_ref, tb_ref, lq1, lk1, lq2, lk2, sg_ref, *rest,
                        pps, n_pages, lq, lam_init):
    kt_refs, v_refs = rest[:pps], rest[pps:2 * pps]
    o_ref, qm_sc, m_sc, l_sc, acc_sc = rest[2 * pps:]
    s = pl.program_id(1)
    n_steps = n_pages // pps
    past = n_pages * PAGE_SIZE
    rows = DIFF_HEADS * DEC_RPH
    width = DIFF_HEADS * DIFF_DV
    rid = lax.broadcasted_iota(jnp.int32, (rows, 1), 0)
    row_tok = (rid % DEC_RPH) // 2

    def q_rows():
        r8 = lax.broadcasted_iota(jnp.int32, (DEC_RPH, width), 0)
        lane_pair = lax.broadcasted_iota(jnp.int32, (DEC_RPH, width), 1) // DIFF_DH
        qrep = jnp.zeros((DEC_RPH, width), F32)
        for t in range(lq):
            qrep = jnp.where(r8 // 2 == t, q_ref[0, t:t + 1, :], qrep)
        qrep = qrep * (DIFF_DH ** -0.5)
        return jnp.concatenate([jnp.where(lane_pair == 2 * hh + r8 % 2, qrep, 0.0) for hh in range(DIFF_HEADS)], axis=0)

    @pl.when(s == 0)
    def _():
        qm_sc[...] = q_rows().astype(BF16)
        m_sc[...] = jnp.full(m_sc.shape, -jnp.inf, F32)
        l_sc[...] = jnp.zeros(l_sc.shape, F32)
        acc_sc[...] = jnp.zeros(acc_sc.shape, F32)

    far_bias = tb_ref[:, REL_BUCKETS - 1:REL_BUCKETS]

    def pages_update(last_near):
        qm = qm_sc[...]
        sc = []
        for i in range(pps):
            sci = _dot(qm, kt_refs[i][0].astype(BF16))
            if last_near and i == pps - 1:
                kpos = (n_pages - 1) * PAGE_SIZE + lax.broadcasted_iota(jnp.int32, (1, PAGE_SIZE), 1)
                sc.append(sci + _t5_bias((past + row_tok) - kpos, lambda bkt: tb_ref[:, bkt:bkt + 1]))
            else:
                sc.append(sci + far_bias)
        m_prev = m_sc[...]
        m_new = m_prev
        for sci in sc:
            m_new = jnp.maximum(m_new, jnp.max(sci, axis=-1, keepdims=True))
        alpha = jnp.exp(m_prev - m_new)
        p = [jnp.exp(sci - m_new) for sci in sc]
        l_new = alpha * l_sc[...]
        for pi in p:
            l_new = l_new + jnp.sum(pi, axis=-1, keepdims=True)
        l_sc[...] = l_new
        m_sc[...] = m_new
        for hh in range(DIFF_HEADS):
            sl = slice(hh * DEC_RPH, (hh + 1) * DEC_RPH)
            ph = jnp.concatenate([pi[sl] for pi in p], axis=1).astype(BF16)
            vh = jnp.concatenate([v_refs[i][0, pl.ds(hh, PAGE_SIZE, stride=DIFF_HEADS), :].astype(BF16)
                                  for i in range(pps)], axis=0)
            acc_sc[sl, :] = alpha[sl] * acc_sc[sl, :] + _dot(ph, vh)

    @pl.when(s < n_steps - 1)
    def _():
        pages_update(False)

    @pl.when(s == n_steps - 1)
    def _():
        pages_update(True)

    @pl.when(s == n_steps)
    def _():
        qf = q_rows()
        sj = []
        for jn in range(lq):
            dist = row_tok - jn
            sc = jnp.sum(qf * kn_ref[0, jn:jn + 1, :], axis=-1, keepdims=True)
            sc = sc + _t5_bias(dist, lambda bkt: tb_ref[:, bkt:bkt + 1])
            sj.append(jnp.where(dist >= 0, sc, -jnp.inf))
        m_prev = m_sc[...]
        m_new = m_prev
        for sc in sj:
            m_new = jnp.maximum(m_new, sc)
        alpha = jnp.exp(m_prev - m_new)
        l_new = alpha * l_sc[...]
        acc = alpha * acc_sc[...]
        for jn, sc in enumerate(sj):
            p = jnp.exp(sc - m_new)
            l_new = l_new + p
            vrow = jnp.concatenate(
                [jnp.broadcast_to(vn_ref[0, jn:jn + 1, hh * DIFF_DV:(hh + 1) * DIFF_DV], (DEC_RPH, DIFF_DV))
                 for hh in range(DIFF_HEADS)], axis=0)
            acc = acc + p * vrow

        lam = _lambda(lq1, lk1, lq2, lk2, lam_init)
        coef = jnp.where(rid % 2 == 0, 1.0, -lam) / l_new
        a = acc * coef
        a = a + pltpu.roll(a, rows - 1, 0)
        res = _rms(a, sg_ref[...]) * (1.0 - lam_init)
        for hh in range(DIFF_HEADS):
            for t in range(lq):
                r = hh * DEC_RPH + 2 * t
                o_ref[0, t:t + 1, hh * DIFF_DV:(hh + 1) * DIFF_DV] = res[r:r + 1, :]


def diff_attention_sample(q, k_new, v_new, cache_kt, cache_v, page_table, rel_table, lams, subln_g, lam_init, pps=16):
    b, lq, width = q.shape
    n_pages = page_table.shape[1]
    assert n_pages % pps == 0 and 2 * lq <= DEC_RPH and n_pages // pps >= 1
    n_steps = n_pages // pps
    rows = DIFF_HEADS * DEC_RPH
    head_of_row = np.arange(rows) // DEC_RPH
    tb = jnp.pad(rel_table.astype(F32).T[head_of_row], ((0, 0), (0, LANES - REL_BUCKETS)))
    vec = lambda a: a.astype(F32).reshape(1, -1)

    def page_map(i):
        return lambda bi, s, pt: (pt[bi * n_pages + jnp.minimum(s, n_steps - 1) * pps + i], 0, 0)

    tok = pl.BlockSpec((1, lq, width), lambda bi, s, pt: (bi, 0, 0))
    const = lambda shape: pl.BlockSpec(shape, lambda bi, s, pt: (0,) * len(shape))
    kpage = [pl.BlockSpec((1, width, PAGE_SIZE), page_map(i)) for i in range(pps)]
    vpage = [pl.BlockSpec((1, PAGE_SIZE * DIFF_HEADS, DIFF_DV), page_map(i)) for i in range(pps)]
    grid_spec = pltpu.PrefetchScalarGridSpec(
        num_scalar_prefetch=1,
        grid=(b, n_steps + 1),
        in_specs=[tok, tok, tok, const((rows, LANES))] + [const((1, DIFF_DH))] * 4 + [const((1, DIFF_DV))] + kpage + vpage,
        out_specs=tok,
        scratch_shapes=[pltpu.VMEM((rows, width), BF16), pltpu.VMEM((rows, 1), F32), pltpu.VMEM((rows, 1), F32),
                        pltpu.VMEM((rows, DIFF_DV), F32)],
    )
    return pl.pallas_call(
        functools.partial(_attn_sample_kernel, pps=pps, n_pages=n_pages, lq=lq, lam_init=lam_init),
        grid_spec=grid_spec,
        out_shape=jax.ShapeDtypeStruct((b, lq, width), F32),
        compiler_params=_cparams(("parallel", "arbitrary")),
        name="diff_attention_sample",
    )(page_table.reshape(-1).astype(jnp.int32), q, k_new, v_new, tb, *[vec(a) for a in lams], vec(subln_g),
      *([cache_kt] * pps), *([cache_v] * pps))


def kernel(x_prompt, x_sample, state_ret, state_ssm, state_conv, cache_k_diff, cache_v_diff, page_table, norm_mix_g, norm_ffn_g, norm_final_g, ret_w_in, ret_w_out, ssd_w_in, ssd_conv_w, ssd_conv_b, ssd_dt_bias, ssd_A_log, ssd_D, ssd_norm_g, ssd_w_out, diff_w_in, diff_lam_q1, diff_lam_k1, diff_lam_q2, diff_lam_k2, diff_subln_g, diff_w_out, rel_bias_table, ffn_w_in, ffn_w_out):
    kinds = tuple(i % 3 for i in range(DEPTH))
    n_pages = page_table.shape[1]
    bf = lambda a: a.astype(BF16)
    ret_wi, ret_wo = bf(ret_w_in), bf(ret_w_out)
    ssd_wmain = bf(ssd_w_in[:, :, :SSD_INNER + SSD_CONV_CH])
    ssd_wdt = bf(jnp.pad(ssd_w_in[:, :, SSD_INNER + SSD_CONV_CH:], ((0, 0), (0, 0), (0, LANES - SSD_HEADS))))
    ssd_wo = bf(ssd_w_out)
    diff_wi, diff_wo = bf(diff_w_in), bf(diff_w_out)
    ffn_wi, ffn_wo = bf(ffn_w_in), bf(ffn_w_out)

    n_ret = kinds.count(0)

    def run_group(x, sample):
        b, l, d = x.shape
        t = b * l
        tm = next((c for c in (1024, 512) if t % c == 0), t)
        ret_c = RET_CHUNK if l % RET_CHUNK == 0 else l
        ssd_c = SSD_CHUNK if l % SSD_CHUNK == 0 else l
        act = F32 if sample else BF16
        offset = n_pages * PAGE_SIZE if sample else 0
        rope_rows = l if l % RET_CHUNK == 0 else -(-l // BF16_ROWS) * BF16_ROWS
        cos, sin = rope_tables(offset, rope_rows)
        ret_new, ssm_new, conv_new, k_new, v_new = None, [], [], [], []
        h = x.reshape(t, d)
        for i in range(DEPTH):
            kind = kinds[i]
            j = kinds[:i].count(kind)
            g = norm_mix_g[i]
            if kind == 0:
                n_in = ret_wi.shape[2]
                (qkvg,) = norm_linear(h, g, ret_wi, j, [(0, n_in, act)], tm)
                s0, j0 = (state_ret, j) if sample else (jnp.zeros((1, b, RET_HEADS, RET_DK, RET_DV), F32), 0)
                y, ret_new = retention_core(qkvg.reshape(b, l, n_in), cos, sin, s0, j0, ret_new, j, n_ret, ret_c, act)
                y, wmix = y.reshape(t, RET_VDIM), ret_wo
            elif kind == 1:
                n_in = SSD_INNER + SSD_CONV_CH
                if sample:
                    zxbc, dt_raw = norm_linear(h, g, ssd_wmain, j, [(0, n_in, act)], tm, side=[(ssd_wdt, F32)])
                    zxbc = zxbc.reshape(b, l, n_in)
                    buf, s0 = state_conv[j], state_ssm[j]
                    conv_new.append(zxbc[:, l - (SSD_CONV - 1):, SSD_INNER:])
                else:
                    zxbc, dt_raw, tails = norm_linear(h, g, ssd_wmain, j, [(0, n_in, act)], tm, side=[(ssd_wdt, F32)],
                                                      tail=(SSD_INNER, n_in))
                    zxbc = zxbc.reshape(b, l, n_in)
                    buf = jnp.zeros((b, SSD_CONV - 1, SSD_CONV_CH), F32)
                    s0 = jnp.zeros((b, SSD_HEADS, SSD_HEADDIM, SSD_STATE), F32)
                    last = tails.reshape(b, l // tm, SUBLANES, SSD_CONV_CH)[:, -1]
                    conv_new.append(last[:, SUBLANES - (SSD_CONV - 1):])
                y, s1 = ssd_core(zxbc, dt_raw.reshape(b, l, LANES), buf, s0, ssd_conv_w[j], ssd_conv_b[j],
                                 ssd_dt_bias[j], ssd_A_log[j], ssd_D[j], ssd_norm_g[j], ssd_c, act)
                ssm_new.append(s1)
                y, wmix = y.reshape(t, SSD_INNER), ssd_wo
            else:
                lam_init = 0.8 - 0.6 * math.exp(-0.3 * i)
                lams = (diff_lam_q1[j], diff_lam_k1[j], diff_lam_q2[j], diff_lam_k2[j])
                if sample:
                    q, kn, vn = norm_linear(h, g, diff_wi, j, [(0, d, F32), (d, 2 * d, F32), (2 * d, 3 * d, F32)], tm)
                    cache_kt = jnp.transpose(cache_k_diff[j], (0, 2, 3, 4, 1)).reshape(-1, d, PAGE_SIZE)
                    cache_v = cache_v_diff[j].reshape(-1, PAGE_SIZE * DIFF_HEADS, DIFF_DV)
                    y = diff_attention_sample(q.reshape(b, l, d), kn.reshape(b, l, d), vn.reshape(b, l, d), cache_kt,
                                              cache_v, page_table, rel_bias_table, lams, diff_subln_g[j], lam_init)
                    k_new.append(kn.reshape(b, l, DIFF_HEADS, 2, DIFF_DH))
                    v_new.append(vn.reshape(b, l, DIFF_HEADS, DIFF_DV))
                else:
                    w = diff_wi[j]
                    q, kt32, kt16, v32, v16 = diff_in_proj(h.reshape(b, l, d), g, w[:, :d], w[:, d:2 * d].T, w[:, 2 * d:],
                                                           ATTN_TILE, ATTN_TILE)
                    y = diff_attention_prompt(q, kt16, v16, rel_bias_table, lams, diff_subln_g[j], lam_init, ATTN_TILE)
                    k_new.append(jnp.transpose(kt32.reshape(b, DIFF_HEADS, 2, DIFF_DH, l), (0, 4, 1, 2, 3)))
                    v_new.append(v32.reshape(b, l, DIFF_HEADS, DIFF_DV))
                y, wmix = y.reshape(t, d), diff_wo
            h = out_ffn(y, wmix, j, h, norm_ffn_g[i], ffn_wi, ffn_wo, i, tm,
                        g_final=norm_final_g if i == DEPTH - 1 else None)
        return h.reshape(b, l, d), ret_new, jnp.stack(ssm_new), jnp.stack(conv_new), jnp.stack(k_new), jnp.stack(v_new)

    y_p, ret_p, ssm_p, conv_p, k_p, v_p = run_group(x_prompt, False)
    y_s, ret_s, ssm_s, conv_s, k_s, v_s = run_group(x_sample, True)
    return (y_p, y_s, ret_p, ssm_p, conv_p, k_p, v_p, ret_s, ssm_s, conv_s, k_s, v_s)
```

```python
import functools
import math

import jax
import jax.numpy as jnp
import numpy as np
from jax import lax
from jax.experimental import pallas as pl
from jax.experimental.pallas import tpu as pltpu

F32 = jnp.float32
BF16 = jnp.bfloat16

D_MODEL = 1024
DEPTH = 4
PAGE_SIZE = 128
EPS = 1e-6
RET_CHUNK = 256
SSD_CHUNK = 256
SSD_SHORT_CHUNK = 128
ATTN_TILE = 512

RET_HEADS = 4
RET_DK = D_MODEL // RET_HEADS
RET_DV = 2 * RET_DK
RET_VDIM = RET_HEADS * RET_DV
ROPE_BASE = 10000.0
ROPE_HALF = RET_DK // 2

SSD_INNER = 2 * D_MODEL
SSD_HEADDIM = 64
SSD_HEADS = SSD_INNER // SSD_HEADDIM
SSD_GROUPS = 8
SSD_HPG = SSD_HEADS // SSD_GROUPS
SSD_STATE = 128
SSD_CONV = 4
SSD_BC = SSD_GROUPS * SSD_STATE
SSD_CONV_CH = SSD_INNER + 2 * SSD_BC
SSD_GW = SSD_HPG * SSD_HEADDIM

DIFF_HEADS = 8
DIFF_DH = D_MODEL // DIFF_HEADS // 2
DIFF_DV = 2 * DIFF_DH
REL_BUCKETS = 32
REL_MAX_DIST = 128

FFN_HIDDEN = -(-8 * D_MODEL // (3 * 256)) * 256

V7X_VMEM_BYTES = 64 * 1024 * 1024
LANES = 128
SUBLANES = 8
BF16_ROWS = 16
VMEM_LIMIT = V7X_VMEM_BYTES * 7 // 8


def _cparams(sem):
    return pltpu.CompilerParams(dimension_semantics=sem, vmem_limit_bytes=VMEM_LIMIT)


def _dot(a, b):
    return jnp.dot(a, b, preferred_element_type=F32)


def _dot_nt(a, b):
    return lax.dot_general(a, b, (((1,), (1,)), ((), ())), preferred_element_type=F32)


def _dot_tn(a, b):
    return lax.dot_general(a, b, (((0,), (0,)), ((), ())), preferred_element_type=F32)


def _rms(x, g):
    return x * lax.rsqrt(jnp.mean(x * x, axis=-1, keepdims=True) + EPS) * g


def _silu(x):
    half = 0.5 * x
    return half * jnp.tanh(half) + half


def _resident(shape):
    return pl.BlockSpec(shape, lambda *_: (0,) * len(shape), pipeline_mode=pl.Buffered(1))


def _resident_layer(stacked, layer):
    tail = stacked.shape[1:]
    return pl.BlockSpec((None,) + tail, lambda *_: (layer,) + (0,) * len(tail), pipeline_mode=pl.Buffered(1))


def _norm_linear_kernel(x_ref, g_ref, w_ref, *rest, outs, tn, side, tail):
    side_w = rest[:side]
    o_refs = rest[side:side + len(outs)]
    side_o = rest[side + len(outs):side + len(outs) + side]
    xn = _rms(x_ref[...], g_ref[...]).astype(BF16)
    n = w_ref.shape[1]
    tm = xn.shape[0]
    for c0 in range(0, n, tn):
        acc = _dot(xn, w_ref[:, c0:c0 + tn])
        for o_ref, (lo, hi, _) in zip(o_refs, outs):
            if lo <= c0 and c0 + tn <= hi:
                o_ref[:, c0 - lo:c0 - lo + tn] = acc.astype(o_ref.dtype)
        if tail is not None and tail[0] <= c0 and c0 + tn <= tail[1]:
            rest[-1][:, c0 - tail[0]:c0 - tail[0] + tn] = acc[tm - SUBLANES:tm]
    for w2_ref, o2_ref in zip(side_w, side_o):
        o2_ref[...] = _dot(xn, w2_ref[...]).astype(o2_ref.dtype)


def norm_linear(x, g, w_stack, layer, outs, tm, tn=512, side=(), tail=None):
    t, d = x.shape
    n = w_stack.shape[2]
    tn = min(tn, n)
    assert t % tm == 0 and n % tn == 0 and all(lo % tn == 0 and hi % tn == 0 for lo, hi, _ in outs)
    row = lambda width: pl.BlockSpec((tm, width), lambda i: (i, 0))
    return pl.pallas_call(
        functools.partial(_norm_linear_kernel, outs=tuple(outs), tn=tn, side=len(side), tail=tail),
        grid=(t // tm,),
        in_specs=[row(d), _resident((1, d)), _resident_layer(w_stack, layer)] + [_resident_layer(w2, layer) for w2, _ in side],
        out_specs=[row(hi - lo) for lo, hi, _ in outs] + [row(w2.shape[2]) for w2, _ in side]
        + ([pl.BlockSpec((SUBLANES, tail[1] - tail[0]), lambda i: (i, 0))] if tail else []),
        out_shape=[jax.ShapeDtypeStruct((t, hi - lo), dt) for lo, hi, dt in outs]
        + [jax.ShapeDtypeStruct((t, w2.shape[2]), dt) for w2, dt in side]
        + ([jax.ShapeDtypeStruct((t // tm * SUBLANES, tail[1] - tail[0]), F32)] if tail else []),
        compiler_params=_cparams(("parallel",)),
        name="norm_linear",
    )(x, g.reshape(1, d), w_stack, *[w2 for w2, _ in side])


def _out_ffn_kernel(y_ref, wmix_ref, h_ref, g_ref, wgu_ref, wo_ref, *rest, th, final):
    o_ref = rest[-1]
    h = h_ref[...] + _dot(y_ref[...].astype(BF16), wmix_ref[...])
    xn = _rms(h, g_ref[...]).astype(BF16)
    hidden = wo_ref.shape[0]
    o_ref[...] = h
    for c0 in range(0, hidden, th):
        gate = _dot(xn, wgu_ref[:, c0:c0 + th])
        up = _dot(xn, wgu_ref[:, hidden + c0:hidden + c0 + th])
        act = (_silu(gate) * up).astype(BF16)
        o_ref[...] += _dot(act, wo_ref[c0:c0 + th, :])
    if final:
        o_ref[...] = _rms(o_ref[...], rest[0][...])


def out_ffn(y, wmix_stack, j, h, g, wgu_stack, wo_stack, i, tm, g_final=None, th=256):
    t, k = y.shape
    d = h.shape[1]
    hidden = wo_stack.shape[1]
    assert hidden % th == 0 and t % tm == 0
    final = g_final is not None
    extra = [g_final.reshape(1, d)] if final else []
    return pl.pallas_call(
        functools.partial(_out_ffn_kernel, th=th, final=final),
        grid=(t // tm,),
        in_specs=[pl.BlockSpec((tm, k), lambda m: (m, 0)), _resident_layer(wmix_stack, j),
                  pl.BlockSpec((tm, d), lambda m: (m, 0)), _resident((1, d)), _resident_layer(wgu_stack, i),
                  _resident_layer(wo_stack, i)] + [_resident((1, d))] * len(extra),
        out_specs=pl.BlockSpec((tm, d), lambda m: (m, 0)),
        out_shape=jax.ShapeDtypeStruct((t, d), F32),
        compiler_params=_cparams(("parallel",)),
        name="out_ffn",
    )(y, wmix_stack, h, g.reshape(1, d), wgu_stack, wo_stack, *extra)


def _rope_kernel(inv_ref, cos_ref, sin_ref, *, offset):
    rows = cos_ref.shape[0]
    pos = (offset + pl.program_id(0) * rows + lax.broadcasted_iota(jnp.int32, (rows, 1), 0)).astype(F32)
    ang = pos * inv_ref[...]
    cos_ref[...] = jnp.cos(ang)
    sin_ref[...] = jnp.sin(ang)


def rope_tables(offset, rows):
    inv = (1.0 / (ROPE_BASE ** jnp.linspace(0.0, 1.0, ROPE_HALF, dtype=F32))).reshape(1, ROPE_HALF)
    tr = min(rows, 512)
    return pl.pallas_call(
        functools.partial(_rope_kernel, offset=offset),
        grid=(rows // tr,),
        in_specs=[_resident((1, ROPE_HALF))],
        out_specs=[pl.BlockSpec((tr, ROPE_HALF), lambda i: (i, 0))] * 2,
        out_shape=[jax.ShapeDtypeStruct((rows, ROPE_HALF), F32)] * 2,
        compiler_params=_cparams(("parallel",)),
        name="rope_tables",
    )(inv)


def _ret_kernel(q_ref, k_ref, v_ref, g_ref, cos_ref, sin_ref, s0_ref, *rest, C, c_real, carry, slot, n_chunks):
    y_ref, st_ref = rest[carry:carry + 2]
    decay_sc = rest[carry + 2]
    stage = rest[carry + 3:]
    c = pl.program_id(1)
    log_gamma = [math.log(1.0 - 2.0 ** (-5.0 - hd)) for hd in range(RET_HEADS)]
    i = lax.broadcasted_iota(jnp.int32, (C, 1), 0).astype(F32)
    j = lax.broadcasted_iota(jnp.int32, (1, C), 1).astype(F32)

    @pl.when(c == 0)
    def _():
        st_ref[slot] = s0_ref[...]
        rel = i - j
        for hd in range(RET_HEADS):
            decay_sc[hd] = jnp.where(rel >= 0, jnp.exp(jnp.maximum(rel, 0.0) * log_gamma[hd]), 0.0)

    if c_real == C:
        q_src, k_src, v_src = q_ref.at[0], k_ref.at[0], v_ref.at[0]
    else:
        for ref, st in zip((q_ref, k_ref, v_ref), stage):
            st[...] = jnp.zeros(st.shape, F32)
            st[0:c_real, :] = ref[0].astype(F32)
        q_src, k_src, v_src = stage
    cos = cos_ref[...]
    sin = sin_ref[...]

    def rot(x):
        x1, x2 = x[:, :ROPE_HALF], x[:, ROPE_HALF:]
        return jnp.concatenate([x1 * cos - x2 * sin, x2 * cos + x1 * sin], axis=-1)

    for hd in range(RET_HEADS):
        lg = log_gamma[hd]
        ks = slice(hd * RET_DK, (hd + 1) * RET_DK)
        vs = slice(hd * RET_DV, (hd + 1) * RET_DV)
        qr = rot(q_src[:, ks].astype(F32)).astype(BF16)
        kr = rot(k_src[:, ks].astype(F32)) * (RET_DK ** -0.5)
        v = v_src[:, vs].astype(BF16)
        scores = _dot_nt(qr, kr.astype(BF16)) * decay_sc[hd]
        s_prev = st_ref[slot, 0, hd]
        o = _dot(scores.astype(BF16), v) + _dot(qr, s_prev.astype(BF16)) * jnp.exp((i + 1.0) * lg)
        kw = (kr * jnp.exp((c_real - 1.0 - i) * lg)).astype(BF16)
        st_ref[slot, 0, hd] = math.exp(c_real * lg) * s_prev + _dot_tn(kw, v)
        o = o * lax.rsqrt(jnp.mean(o * o, axis=-1, keepdims=True) + EPS)
        y_ref[0, :, vs] = (_silu(g_ref[0, :, vs].astype(F32)) * o[0:c_real]).astype(y_ref.dtype)

    if st_ref.shape[0] > 1:
        @pl.when(c == n_chunks - 1)
        def _():
            for other in range(st_ref.shape[0]):
                if other != slot:
                    st_ref[other] = st_ref[slot]


def retention_core(qkvg, cos, sin, s0_stack, j0, s1_stack, j1, n_layers, c_real, out_dtype):
    b, l, _ = qkvg.shape
    C = cos.shape[0] if c_real < RET_CHUNK else RET_CHUNK
    nc = l // c_real
    qk_w, v_w = RET_HEADS * RET_DK, RET_VDIM
    k_blk, v_blk, g_blk = 1, 2 * qk_w // v_w, 2 * qk_w // v_w + 1
    stage = [] if c_real == C else [pltpu.VMEM((C, qk_w), F32), pltpu.VMEM((C, qk_w), F32), pltpu.VMEM((C, v_w), F32)]
    carry = s1_stack is not None
    st_block, st_first, slot = (1, j1, 0) if carry else (n_layers, 0, j1)
    return pl.pallas_call(
        functools.partial(_ret_kernel, C=C, c_real=c_real, carry=int(carry), slot=slot, n_chunks=nc),
        grid=(b, nc),
        in_specs=[
            pl.BlockSpec((1, c_real, qk_w), lambda bi, c: (bi, c, 0)),
            pl.BlockSpec((1, c_real, qk_w), lambda bi, c: (bi, c, k_blk)),
            pl.BlockSpec((1, c_real, v_w), lambda bi, c: (bi, c, v_blk)),
            pl.BlockSpec((1, c_real, v_w), lambda bi, c: (bi, c, g_blk)),
            pl.BlockSpec((C, ROPE_HALF), lambda bi, c: (c, 0)),
            pl.BlockSpec((C, ROPE_HALF), lambda bi, c: (c, 0)),
            pl.BlockSpec((None, 1, RET_HEADS, RET_DK, RET_DV), lambda bi, c: (j0, bi, 0, 0, 0)),
        ] + ([pl.BlockSpec(memory_space=pl.ANY)] if carry else []),
        out_specs=[
            pl.BlockSpec((1, c_real, v_w), lambda bi, c: (bi, c, 0)),
            pl.BlockSpec((st_block, 1, RET_HEADS, RET_DK, RET_DV), lambda bi, c: (st_first, bi, 0, 0, 0)),
        ],
        out_shape=[jax.ShapeDtypeStruct((b, l, RET_VDIM), out_dtype),
                   jax.ShapeDtypeStruct((n_layers, b, RET_HEADS, RET_DK, RET_DV), F32)],
        scratch_shapes=[pltpu.VMEM((RET_HEADS, C, C), F32)] + stage,
        input_output_aliases={7: 1} if carry else {},
        compiler_params=_cparams(("parallel", "arbitrary")),
        name="retention_core",
    )(qkvg, qkvg, qkvg, qkvg, cos, sin, s0_stack, *([s1_stack] if carry else []))


def _softplus(x):
    return jnp.maximum(x, 0.0) + jnp.log1p(jnp.exp(-jnp.abs(x)))


def _cumsum(x, axis):
    n = x.shape[axis]
    idx = lax.broadcasted_iota(jnp.int32, x.shape, axis)
    s = 1
    while s < n:
        x = x + jnp.where(idx >= s, pltpu.roll(x, s, axis), 0.0)
        s *= 2
    return x


def _ssd_kernel(z_ref, x_ref, bc_ref, dt_ref, dtt_ref, cw_ref, cb_ref, dtb_ref, dtbt_ref, al_ref, alt_ref,
                dsk_ref, ng_ref, buf_ref, s0_ref, y_ref, s1_ref, extx, extbc, xs_ref, bcs_ref, *, C, c_real):
    c = pl.program_id(1)
    halo = SUBLANES

    @pl.when(c == 0)
    def _():
        s1_ref[...] = s0_ref[...]
        extx[...] = jnp.zeros(extx.shape, F32)
        extbc[...] = jnp.zeros(extbc.shape, F32)
        extx[0:halo, :] = buf_ref[0, :, 0:SSD_INNER]
        extbc[0:halo, :] = buf_ref[0, :, SSD_INNER:SSD_CONV_CH]

    extx[halo:halo + c_real, :] = x_ref[0].astype(F32)
    extbc[halo:halo + c_real, :] = bc_ref[0].astype(F32)

    cw = 512
    first = halo - (SSD_CONV - 1)
    for ext, dst, ch0 in ((extx, xs_ref, 0), (extbc, bcs_ref, SSD_INNER)):
        for col in range(0, SSD_INNER, cw):
            acc = cb_ref[:, ch0 + col:ch0 + col + cw]
            for tap in range(SSD_CONV):
                acc = acc + ext[first + tap:first + tap + C, col:col + cw] * cw_ref[tap:tap + 1, ch0 + col:ch0 + col + cw]
            dst[:, col:col + cw] = _silu(acc).astype(dst.dtype)
    extx[0:halo, :] = extx[C:C + halo, :]
    extbc[0:halo, :] = extbc[C:C + halo, :]

    row = lax.broadcasted_iota(jnp.int32, (C, 1), 0)
    col_i = lax.broadcasted_iota(jnp.int32, (1, C), 1)
    dt = jnp.where(row < c_real, _softplus(dt_ref[0, 0] + dtb_ref[...]), 0.0)
    dtt = jnp.where(col_i < c_real, _softplus(dtt_ref[0, 0] + dtbt_ref[...]), 0.0)
    cs = _cumsum(dt * (-jnp.exp(al_ref[...])), 0)
    cst = _cumsum(dtt * (-jnp.exp(alt_ref[...])), 1)
    causal = row >= col_i
    lane_r = lax.broadcasted_iota(jnp.int32, (1, SSD_GW), 1) // SSD_HEADDIM
    sub_r = lax.broadcasted_iota(jnp.int32, (SSD_GW, 1), 0) // SSD_HEADDIM

    def by_head(vals, sel):
        out = vals[SSD_HPG - 1]
        for r in range(SSD_HPG - 2, -1, -1):
            out = jnp.where(sel == r, vals[r], out)
        return out

    for g in range(SSD_GROUPS):
        bm = bcs_ref[:, g * SSD_STATE:(g + 1) * SSD_STATE]
        cm = bcs_ref[:, SSD_BC + g * SSD_STATE:SSD_BC + (g + 1) * SSD_STATE]
        cb = _dot_nt(cm, bm)
        h0 = s1_ref[0, g * SSD_HPG:(g + 1) * SSD_HPG].reshape(SSD_GW, SSD_STATE)
        xg = xs_ref[:, g * SSD_GW:(g + 1) * SSD_GW]
        heads = range(g * SSD_HPG, (g + 1) * SSD_HPG)
        cs_cols = [cs[:, hh:hh + 1] for hh in heads]
        cs_last = [cs[C - 1:C, hh:hh + 1] for hh in heads]
        yg = _dot_nt(cm, h0.astype(BF16)) * by_head([jnp.exp(cc) for cc in cs_cols], lane_r)
        for r, hh in enumerate(heads):
            seg = cs_cols[r] - cst[hh:hh + 1, :]
            w = cb * jnp.exp(jnp.where(causal, seg, -jnp.inf)) * dtt[hh:hh + 1, :]
            yg = yg + _dot(w.astype(BF16), jnp.where(lane_r == r, xg, 0.0).astype(BF16))
        w_end = by_head([jnp.exp(cl - cc) * dt[:, hh:hh + 1] for cl, cc, hh in zip(cs_last, cs_cols, heads)], lane_r)
        h1 = by_head([jnp.exp(cl) for cl in cs_last], sub_r) * h0 + _dot_tn((xg * w_end).astype(BF16), bm)
        s1_ref[0, g * SSD_HPG:(g + 1) * SSD_HPG] = h1.reshape(SSD_HPG, SSD_HEADDIM, SSD_STATE)
        sl = slice(g * SSD_GW, (g + 1) * SSD_GW)
        yo = (yg[0:c_real] + dsk_ref[:, sl] * xg[0:c_real]) * _silu(z_ref[0, :, sl].astype(F32))
        yo = yo * lax.rsqrt(jnp.mean(yo * yo, axis=-1, keepdims=True) + EPS) * ng_ref[:, sl]
        y_ref[0, :, sl] = yo.astype(y_ref.dtype)


def ssd_core(zxbc, dt_raw, conv_buf, s0, conv_w, conv_b, dt_bias, a_log, d_skip, norm_g, c_real, out_dtype):
    b, l, _ = zxbc.shape
    C = SSD_CHUNK if c_real == SSD_CHUNK else SSD_SHORT_CHUNK
    nc = l // c_real
    halo = SUBLANES
    dt4 = dt_raw.reshape(b, nc, c_real, LANES)
    dtt = jnp.swapaxes(dt4[..., :SSD_HEADS], -1, -2)
    if c_real < C:
        dt4 = jnp.pad(dt4, ((0, 0), (0, 0), (0, C - c_real), (0, 0)))
        dtt = jnp.pad(dtt, ((0, 0), (0, 0), (0, 0), (0, C - c_real)))
    lane_pad = LANES - SSD_HEADS
    buf8 = jnp.pad(conv_buf.astype(F32), ((0, 0), (halo - (SSD_CONV - 1), 0), (0, 0)))
    small = [
        conv_w.astype(F32), conv_b.reshape(1, -1).astype(F32),
        jnp.pad(dt_bias.astype(F32), (0, lane_pad)).reshape(1, LANES), dt_bias.astype(F32).reshape(SSD_HEADS, 1),
        jnp.pad(a_log.astype(F32), (0, lane_pad)).reshape(1, LANES), a_log.astype(F32).reshape(SSD_HEADS, 1),
        jnp.repeat(d_skip.astype(F32), SSD_HEADDIM).reshape(1, SSD_INNER), norm_g.astype(F32).reshape(1, SSD_INNER),
    ]
    w = SSD_INNER
    return pl.pallas_call(
        functools.partial(_ssd_kernel, C=C, c_real=c_real),
        grid=(b, nc),
        in_specs=[
            pl.BlockSpec((1, c_real, w), lambda bi, c: (bi, c, 0)),
            pl.BlockSpec((1, c_real, w), lambda bi, c: (bi, c, 1)),
            pl.BlockSpec((1, c_real, w), lambda bi, c: (bi, c, 2)),
            pl.BlockSpec((1, 1, C, LANES), lambda bi, c: (bi, c, 0, 0)),
            pl.BlockSpec((1, 1, SSD_HEADS, C), lambda bi, c: (bi, c, 0, 0)),
        ] + [_resident(a.shape) for a in small] + [
            pl.BlockSpec((1, halo, SSD_CONV_CH), lambda bi, c: (bi, 0, 0)),
            pl.BlockSpec((1, SSD_HEADS, SSD_HEADDIM, SSD_STATE), lambda bi, c: (bi, 0, 0, 0)),
        ],
        out_specs=[
            pl.BlockSpec((1, c_real, w), lambda bi, c: (bi, c, 0)),
            pl.BlockSpec((1, SSD_HEADS, SSD_HEADDIM, SSD_STATE), lambda bi, c: (bi, 0, 0, 0)),
        ],
        out_shape=[jax.ShapeDtypeStruct((b, l, SSD_INNER), out_dtype),
                   jax.ShapeDtypeStruct((b, SSD_HEADS, SSD_HEADDIM, SSD_STATE), F32)],
        scratch_shapes=[pltpu.VMEM((C + halo, w), F32), pltpu.VMEM((C + halo, w), F32),
                        pltpu.VMEM((C, w), F32), pltpu.VMEM((C, w), BF16)],
        compiler_params=_cparams(("parallel", "arbitrary")),
        name="ssd_core",
    )(zxbc, zxbc, zxbc, dt4, dtt, *small, buf8, s0)


def _t5_bias(dist, table_at):
    n = jnp.maximum(dist, 0)
    exact = REL_BUCKETS // 2
    nf = jnp.maximum(n, 1).astype(F32)
    large = exact + (jnp.log(nf / exact) / math.log(REL_MAX_DIST / exact) * (REL_BUCKETS - exact)).astype(jnp.int32)
    bucket = jnp.where(n < exact, n, jnp.minimum(large, REL_BUCKETS - 1))
    bias = jnp.zeros(dist.shape, F32)
    for bkt in range(REL_BUCKETS):
        bias = jnp.where(bucket == bkt, table_at(bkt), bias)
    return bias


def _lambda(lq1, lk1, lq2, lk2, lam_init):
    s1 = jnp.sum(lq1[...] * lk1[...], axis=-1, keepdims=True)
    s2 = jnp.sum(lq2[...] * lk2[...], axis=-1, keepdims=True)
    return jnp.exp(s1) - jnp.exp(s2) + lam_init


def _diff_in_proj_kernel(x_ref, g_ref, wq_ref, wkt_ref, wv_ref, q_ref, kt32_ref, kt16_ref, v32_ref, v16_ref, *, tn, kb):
    xn = _rms(x_ref[0], g_ref[...]).astype(BF16)
    d = wq_ref.shape[1]
    tm = xn.shape[0]
    for c0 in range(0, d, tn):
        q_ref[0, :, c0:c0 + tn] = (_dot(xn, wq_ref[:, c0:c0 + tn]) * (DIFF_DH ** -0.5)).astype(BF16)
        v = _dot(xn, wv_ref[:, c0:c0 + tn])
        v32_ref[0, :, c0:c0 + tn] = v
        v16_ref[0, :, c0:c0 + tn] = v.astype(BF16)
        kt = _dot_nt(wkt_ref[c0:c0 + tn, :], xn)
        kt32_ref[0, c0:c0 + tn, :] = kt
        for s in range(tm // kb):
            kt16_ref[0, s, c0:c0 + tn, :] = kt[:, s * kb:(s + 1) * kb].astype(BF16)


def diff_in_proj(x, g, wq, wkt, wv, tm, kb):
    b, l, d = x.shape
    assert l % tm == 0 and tm % kb == 0
    tok = lambda: pl.BlockSpec((1, tm, d), lambda bi, i: (bi, i, 0))
    return pl.pallas_call(
        functools.partial(_diff_in_proj_kernel, tn=512, kb=kb),
        grid=(b, l // tm),
        in_specs=[tok(), _resident((1, d)), _resident((d, d)), _resident((d, d)), _resident((d, d))],
        out_specs=[tok(), pl.BlockSpec((1, d, tm), lambda bi, i: (bi, 0, i)),
                   pl.BlockSpec((1, tm // kb, d, kb), lambda bi, i: (bi, i, 0, 0)), tok(), tok()],
        out_shape=[jax.ShapeDtypeStruct((b, l, d), BF16), jax.ShapeDtypeStruct((b, d, l), F32),
                   jax.ShapeDtypeStruct((b, l // kb, d, kb), BF16), jax.ShapeDtypeStruct((b, l, d), F32),
                   jax.ShapeDtypeStruct((b, l, d), BF16)],
        compiler_params=_cparams(("parallel", "parallel")),
        name="diff_in_proj",
    )(x, g.reshape(1, d), wq, wkt, wv)


def _attn_prompt_kernel(tbl_ref, q_ref, kt_ref, v_ref, lq1, lk1, lq2, lk2, sg_ref, o_ref,
                        bias_sc, m_sc, a_sc, *, T, lam_init):
    hd = pl.program_id(0)
    bi = pl.program_id(1)
    qi = pl.program_id(2)

    @pl.when((bi == 0) & (qi == 0))
    def _():
        i = lax.broadcasted_iota(jnp.int32, (T, 2 * T), 0)
        j = lax.broadcasted_iota(jnp.int32, (T, 2 * T), 1)
        dist = i - j + T
        bias_sc[...] = jnp.where(dist >= 0, _t5_bias(dist, lambda bkt: tbl_ref[bkt, hd]), -jnp.inf)

    lane = lax.broadcasted_iota(jnp.int32, (1, DIFF_DV), 1)
    q = q_ref[0]
    zero = jnp.zeros((), q.dtype)
    q2 = jnp.concatenate([jnp.where(lane < DIFF_DH, q, zero), jnp.where(lane >= DIFF_DH, q, zero)], axis=0)
    m_sc[...] = jnp.full(m_sc.shape, -jnp.inf, F32)
    a_sc[...] = jnp.zeros(a_sc.shape, F32)

    def update(kb, bias):
        vt = v_ref[0, pl.ds(pl.multiple_of(kb * T, T), T), :]
        vx = jnp.concatenate([vt, jnp.ones((T, DIFF_DV), BF16)], axis=1)
        s = _dot(q2, kt_ref[0, kb])
        s = s + (jnp.concatenate([bias, bias], axis=0) if getattr(bias, "ndim", 0) == 2 else bias)
        m_prev = m_sc[...]
        m_new = jnp.maximum(m_prev, jnp.max(s, axis=-1, keepdims=True))
        alpha = jnp.exp(m_prev - m_new)
        p = jnp.concatenate([jnp.exp(s[:, c:c + LANES] - m_new) for c in range(0, T, LANES)], axis=1)
        a_sc[...] = jnp.concatenate([alpha, alpha], axis=1) * a_sc[...] + _dot(p.astype(BF16), vx)
        m_sc[...] = m_new

    far_bias = tbl_ref[REL_BUCKETS - 1, hd]

    def far_body(kb, carry):
        update(kb, far_bias)
        return carry

    lax.fori_loop(0, jnp.maximum(qi - 1, 0), far_body, 0)

    @pl.when(qi >= 1)
    def _():
        update(qi - 1, bias_sc[:, 0:T])

    update(qi, bias_sc[:, T:2 * T])

    lam = _lambda(lq1, lk1, lq2, lk2, lam_init)
    o = a_sc[:T, :DIFF_DV] / a_sc[:T, DIFF_DV:] - lam * (a_sc[T:, :DIFF_DV] / a_sc[T:, DIFF_DV:])
    o_ref[0] = (_rms(o, sg_ref[...]) * (1.0 - lam_init)).astype(o_ref.dtype)


def diff_attention_prompt(q, kt, v, rel_table, lams, subln_g, lam_init, T):
    b, l, _ = q.shape
    assert T >= REL_MAX_DIST and l % T == 0 and DIFF_DV == 2 * DIFF_DH
    vec = lambda a: a.astype(F32).reshape(1, -1)
    return pl.pallas_call(
        functools.partial(_attn_prompt_kernel, T=T, lam_init=lam_init),
        grid=(DIFF_HEADS, b, l // T),
        in_specs=[
            pl.BlockSpec(memory_space=pltpu.SMEM),
            pl.BlockSpec((1, T, DIFF_DV), lambda h, bi, qi: (bi, qi, h)),
            pl.BlockSpec((1, l // T, DIFF_DV, T), lambda h, bi, qi: (bi, 0, h, 0)),
            pl.BlockSpec((1, l, DIFF_DV), lambda h, bi, qi: (bi, 0, h)),
        ] + [_resident((1, DIFF_DH))] * 4 + [_resident((1, DIFF_DV))],
        out_specs=pl.BlockSpec((1, T, DIFF_DV), lambda h, bi, qi: (bi, qi, h)),
        out_shape=jax.ShapeDtypeStruct((b, l, DIFF_HEADS * DIFF_DV), BF16),
        scratch_shapes=[pltpu.VMEM((T, 2 * T), F32), pltpu.VMEM((2 * T, LANES), F32), pltpu.VMEM((2 * T, 2 * DIFF_DV), F32)],
        compiler_params=_cparams(("arbitrary", "arbitrary", "arbitrary")),
        name="diff_attention_prompt",
    )(rel_table.astype(F32), q, kt, v, *[vec(a) for a in lams], vec(subln_g))


DEC_RPH = SUBLANES


def _attn_sample_kernel(pt_ref, q_ref, kn_ref, vn_ref, tb_ref, lq1, lk1, lq2, lk2, sg_ref, *rest,
                        pps, n_pages, lq, lam_init):
    kt_refs, v_refs = rest[:pps], rest[pps:2 * pps]
    o_ref, qm_sc, m_sc, l_sc, acc_sc = rest[2 * pps:]
    s = pl.program_id(1)
    n_steps = n_pages // pps
    past = n_pages * PAGE_SIZE
    rows = DIFF_HEADS * DEC_RPH
    width = DIFF_HEADS * DIFF_DV
    rid = lax.broadcasted_iota(jnp.int32, (rows, 1), 0)
    row_tok = (rid % DEC_RPH) // 2

    def q_rows():
        r8 = lax.broadcasted_iota(jnp.int32, (DEC_RPH, width), 0)
        lane_pair = lax.broadcasted_iota(jnp.int32, (DEC_RPH, width), 1) // DIFF_DH
        qrep = jnp.zeros((DEC_RPH, width), F32)
        for t in range(lq):
            qrep = jnp.where(r8 // 2 == t, q_ref[0, t:t + 1, :], qrep)
        qrep = qrep * (DIFF_DH ** -0.5)
        return jnp.concatenate([jnp.where(lane_pair == 2 * hh + r8 % 2, qrep, 0.0) for hh in range(DIFF_HEADS)], axis=0)

    @pl.when(s == 0)
    def _():
        qm_sc[...] = q_rows().astype(BF16)
        m_sc[...] = jnp.full(m_sc.shape, -jnp.inf, F32)
        l_sc[...] = jnp.zeros(l_sc.shape, F32)
        acc_sc[...] = jnp.zeros(acc_sc.shape, F32)

    far_bias = tb_ref[:, REL_BUCKETS - 1:REL_BUCKETS]

    def pages_update(last_near):
        qm = qm_sc[...]
        sc = []
        for i in range(pps):
            sci = _dot(qm, kt_refs[i][0].astype(BF16))
            if last_near and i == pps - 1:
                kpos = (n_pages - 1) * PAGE_SIZE + lax.broadcasted_iota(jnp.int32, (1, PAGE_SIZE), 1)
                sc.append(sci + _t5_bias((past + row_tok) - kpos, lambda bkt: tb_ref[:, bkt:bkt + 1]))
            else:
                sc.append(sci + far_bias)
        m_prev = m_sc[...]
        m_new = m_prev
        for sci in sc:
            m_new = jnp.maximum(m_new, jnp.max(sci, axis=-1, keepdims=True))
        alpha = jnp.exp(m_prev - m_new)
        p = [jnp.exp(sci - m_new) for sci in sc]
        l_new = alpha * l_sc[...]
        for pi in p:
            l_new = l_new + jnp.sum(pi, axis=-1, keepdims=True)
        l_sc[...] = l_new
        m_sc[...] = m_new
        for hh in range(DIFF_HEADS):
            sl = slice(hh * DEC_RPH, (hh + 1) * DEC_RPH)
            ph = jnp.concatenate([pi[sl] for pi in p], axis=1).astype(BF16)
            vh = jnp.concatenate([v_refs[i][0, pl.ds(hh, PAGE_SIZE, stride=DIFF_HEADS), :].astype(BF16)
                                  for i in range(pps)], axis=0)
            acc_sc[sl, :] = alpha[sl] * acc_sc[sl, :] + _dot(ph, vh)

    @pl.when(s < n_steps - 1)
    def _():
        pages_update(False)

    @pl.when(s == n_steps - 1)
    def _():
        pages_update(True)

    @pl.when(s == n_steps)
    def _():
        qf = q_rows()
        sj = []
        for jn in range(lq):
            dist = row_tok - jn
            sc = jnp.sum(qf * kn_ref[0, jn:jn + 1, :], axis=-1, keepdims=True)
            sc = sc + _t5_bias(dist, lambda bkt: tb_ref[:, bkt:bkt + 1])
            sj.append(jnp.where(dist >= 0, sc, -jnp.inf))
        m_prev = m_sc[...]
        m_new = m_prev
        for sc in sj:
            m_new = jnp.maximum(m_new, sc)
        alpha = jnp.exp(m_prev - m_new)
        l_new = alpha * l_sc[...]
        acc = alpha * acc_sc[...]
        for jn, sc in enumerate(sj):
            p = jnp.exp(sc - m_new)
            l_new = l_new + p
            vrow = jnp.concatenate(
                [jnp.broadcast_to(vn_ref[0, jn:jn + 1, hh * DIFF_DV:(hh + 1) * DIFF_DV], (DEC_RPH, DIFF_DV))
                 for hh in range(DIFF_HEADS)], axis=0)
            acc = acc + p * vrow

        lam = _lambda(lq1, lk1, lq2, lk2, lam_init)
        coef = jnp.where(rid % 2 == 0, 1.0, -lam) / l_new
        a = acc * coef
        a = a + pltpu.roll(a, rows - 1, 0)
        res = _rms(a, sg_ref[...]) * (1.0 - lam_init)
        for hh in range(DIFF_HEADS):
            for t in range(lq):
                r = hh * DEC_RPH + 2 * t
                o_ref[0, t:t + 1, hh * DIFF_DV:(hh + 1) * DIFF_DV] = res[r:r + 1, :]


def diff_attention_sample(q, k_new, v_new, cache_kt, cache_v, page_table, rel_table, lams, subln_g, lam_init, pps=16):
    b, lq, width = q.shape
    n_pages = page_table.shape[1]
    assert n_pages % pps == 0 and 2 * lq <= DEC_RPH and n_pages // pps >= 1
    n_steps = n_pages // pps
    rows = DIFF_HEADS * DEC_RPH
    head_of_row = np.arange(rows) // DEC_RPH
    tb = jnp.pad(rel_table.astype(F32).T[head_of_row], ((0, 0), (0, LANES - REL_BUCKETS)))
    vec = lambda a: a.astype(F32).reshape(1, -1)

    def page_map(i):
        return lambda bi, s, pt: (pt[bi * n_pages + jnp.minimum(s, n_steps - 1) * pps + i], 0, 0)

    tok = pl.BlockSpec((1, lq, width), lambda bi, s, pt: (bi, 0, 0))
    const = lambda shape: pl.BlockSpec(shape, lambda bi, s, pt: (0,) * len(shape))
    kpage = [pl.BlockSpec((1, width, PAGE_SIZE), page_map(i)) for i in range(pps)]
    vpage = [pl.BlockSpec((1, PAGE_SIZE * DIFF_HEADS, DIFF_DV), page_map(i)) for i in range(pps)]
    grid_spec = pltpu.PrefetchScalarGridSpec(
        num_scalar_prefetch=1,
        grid=(b, n_steps + 1),
        in_specs=[tok, tok, tok, const((rows, LANES))] + [const((1, DIFF_DH))] * 4 + [const((1, DIFF_DV))] + kpage + vpage,
        out_specs=tok,
        scratch_shapes=[pltpu.VMEM((rows, width), BF16), pltpu.VMEM((rows, 1), F32), pltpu.VMEM((rows, 1), F32),
                        pltpu.VMEM((rows, DIFF_DV), F32)],
    )
    return pl.pallas_call(
        functools.partial(_attn_sample_kernel, pps=pps, n_pages=n_pages, lq=lq, lam_init=lam_init),
        grid_spec=grid_spec,
        out_shape=jax.ShapeDtypeStruct((b, lq, width), F32),
        compiler_params=_cparams(("parallel", "arbitrary")),
        name="diff_attention_sample",
    )(page_table.reshape(-1).astype(jnp.int32), q, k_new, v_new, tb, *[vec(a) for a in lams], vec(subln_g),
      *([cache_kt] * pps), *([cache_v] * pps))


def kernel(x_prompt, x_sample, state_ret, state_ssm, state_conv, cache_k_diff, cache_v_diff, page_table, norm_mix_g, norm_ffn_g, norm_final_g, ret_w_in, ret_w_out, ssd_w_in, ssd_conv_w, ssd_conv_b, ssd_dt_bias, ssd_A_log, ssd_D, ssd_norm_g, ssd_w_out, diff_w_in, diff_lam_q1, diff_lam_k1, diff_lam_q2, diff_lam_k2, diff_subln_g, diff_w_out, rel_bias_table, ffn_w_in, ffn_w_out):
    kinds = tuple(i % 3 for i in range(DEPTH))
    n_pages = page_table.shape[1]
    bf = lambda a: a.astype(BF16)
    ret_wi, ret_wo = bf(ret_w_in), bf(ret_w_out)
    ssd_wmain = bf(ssd_w_in[:, :, :SSD_INNER + SSD_CONV_CH])
    ssd_wdt = bf(jnp.pad(ssd_w_in[:, :, SSD_INNER + SSD_CONV_CH:], ((0, 0), (0, 0), (0, LANES - SSD_HEADS))))
    ssd_wo = bf(ssd_w_out)
    diff_wi, diff_wo = bf(diff_w_in), bf(diff_w_out)
    ffn_wi, ffn_wo = bf(ffn_w_in), bf(ffn_w_out)

    n_ret = kinds.count(0)

    def run_group(x, sample):
        b, l, d = x.shape
        t = b * l
        tm = next((c for c in (1024, 512) if t % c == 0), t)
        ret_c = RET_CHUNK if l % RET_CHUNK == 0 else l
        ssd_c = SSD_CHUNK if l % SSD_CHUNK == 0 else l
        act = F32 if sample else BF16
        offset = n_pages * PAGE_SIZE if sample else 0
        rope_rows = l if l % RET_CHUNK == 0 else -(-l // BF16_ROWS) * BF16_ROWS
        cos, sin = rope_tables(offset, rope_rows)
        ret_new, ssm_new, conv_new, k_new, v_new = None, [], [], [], []
        h = x.reshape(t, d)
        for i in range(DEPTH):
            kind = kinds[i]
            j = kinds[:i].count(kind)
            g = norm_mix_g[i]
            if kind == 0:
                n_in = ret_wi.shape[2]
                (qkvg,) = norm_linear(h, g, ret_wi, j, [(0, n_in, act)], tm)
                s0, j0 = (state_ret, j) if sample else (jnp.zeros((1, b, RET_HEADS, RET_DK, RET_DV), F32), 0)
                y, ret_new = retention_core(qkvg.reshape(b, l, n_in), cos, sin, s0, j0, ret_new, j, n_ret, ret_c, act)
                y, wmix = y.reshape(t, RET_VDIM), ret_wo
            elif kind == 1:
                n_in = SSD_INNER + SSD_CONV_CH
                if sample:
                    zxbc, dt_raw = norm_linear(h, g, ssd_wmain, j, [(0, n_in, act)], tm, side=[(ssd_wdt, F32)])
                    zxbc = zxbc.reshape(b, l, n_in)
                    buf, s0 = state_conv[j], state_ssm[j]
                    conv_new.append(zxbc[:, l - (SSD_CONV - 1):, SSD_INNER:])
                else:
                    zxbc, dt_raw, tails = norm_linear(h, g, ssd_wmain, j, [(0, n_in, act)], tm, side=[(ssd_wdt, F32)],
                                                      tail=(SSD_INNER, n_in))
                    zxbc = zxbc.reshape(b, l, n_in)
                    buf = jnp.zeros((b, SSD_CONV - 1, SSD_CONV_CH), F32)
                    s0 = jnp.zeros((b, SSD_HEADS, SSD_HEADDIM, SSD_STATE), F32)
                    last = tails.reshape(b, l // tm, SUBLANES, SSD_CONV_CH)[:, -1]
                    conv_new.append(last[:, SUBLANES - (SSD_CONV - 1):])
                y, s1 = ssd_core(zxbc, dt_raw.reshape(b, l, LANES), buf, s0, ssd_conv_w[j], ssd_conv_b[j],
                                 ssd_dt_bias[j], ssd_A_log[j], ssd_D[j], ssd_norm_g[j], ssd_c, act)
                ssm_new.append(s1)
                y, wmix = y.reshape(t, SSD_INNER), ssd_wo
            else:
                lam_init = 0.8 - 0.6 * math.exp(-0.3 * i)
                lams = (diff_lam_q1[j], diff_lam_k1[j], diff_lam_q2[j], diff_lam_k2[j])
                if sample:
                    q, kn, vn = norm_linear(h, g, diff_wi, j, [(0, d, F32), (d, 2 * d, F32), (2 * d, 3 * d, F32)], tm)
                    cache_kt = jnp.transpose(cache_k_diff[j], (0, 2, 3, 4, 1)).reshape(-1, d, PAGE_SIZE)
                    cache_v = cache_v_diff[j].reshape(-1, PAGE_SIZE * DIFF_HEADS, DIFF_DV)
                    y = diff_attention_sample(q.reshape(b, l, d), kn.reshape(b, l, d), vn.reshape(b, l, d), cache_kt,
                                              cache_v, page_table, rel_bias_table, lams, diff_subln_g[j], lam_init)
                    k_new.append(kn.reshape(b, l, DIFF_HEADS, 2, DIFF_DH))
                    v_new.append(vn.reshape(b, l, DIFF_HEADS, DIFF_DV))
                else:
                    w = diff_wi[j]
                    q, kt32, kt16, v32, v16 = diff_in_proj(h.reshape(b, l, d), g, w[:, :d], w[:, d:2 * d].T, w[:, 2 * d:],
                                                           ATTN_TILE, ATTN_TILE)
                    y = diff_attention_prompt(q, kt16, v16, rel_bias_table, lams, diff_subln_g[j], lam_init, ATTN_TILE)
                    k_new.append(jnp.transpose(kt32.reshape(b, DIFF_HEADS, 2, DIFF_DH, l), (0, 4, 1, 2, 3)))
                    v_new.append(v32.reshape(b, l, DIFF_HEADS, DIFF_DV))
                y, wmix = y.reshape(t, d), diff_wo
            h = out_ffn(y, wmix, j, h, norm_ffn_g[i], ffn_wi, ffn_wo, i, tm,
                        g_final=norm_final_g if i == DEPTH - 1 else None)
        return h.reshape(b, l, d), ret_new, jnp.stack(ssm_new), jnp.stack(conv_new), jnp.stack(k_new), jnp.stack(v_new)

    y_p, ret_p, ssm_p, conv_p, k_p, v_p = run_group(x_prompt, False)
    y_s, ret_s, ssm_s, conv_s, k_s, v_s = run_group(x_sample, True)
    return (y_p, y_s, ret_p, ssm_p, conv_p, k_p, v_p, ret_s, ssm_s, conv_s, k_s, v_s)
```
